```python
import math
import jax, jax.numpy as jnp
from jax import lax
import numpy as np

D_MODEL = 2048
BATCH = 4
SEQ = 2048
DEPTH = 4
DEC_BATCH = 32
DEC_SEQ = 1
PAST_LEN = 16384
PAGE_SIZE = 128

HEAD_DIM = 64
ATTN_WIDTH = D_MODEL // 2
N_HEADS = ATTN_WIDTH // HEAD_DIM
N_KV_HEADS = N_HEADS // 4
GROUP = N_HEADS // N_KV_HEADS
Q_DIM = N_HEADS * HEAD_DIM
KV_DIM = N_KV_HEADS * HEAD_DIM
CONV_DIM = D_MODEL - ATTN_WIDTH
CONV_WIDTH = 3
MIX_DIM = Q_DIM + CONV_DIM
IN_DIM = Q_DIM + 2 * KV_DIM + 3 * CONV_DIM
WINDOW = 128
BLOCK = 128
NUM_BUCKETS = 32
MAX_DISTANCE = 128
N_META = 16
D_FF = ((8 * D_MODEL // 3 + 255) // 256) * 256
RMS_EPS = 1e-6
SCALE = HEAD_DIM ** -0.5

kernel_name = "hymba_swa_shortconv_step"


def rms_norm(x, g):
    xf = x.astype(jnp.float32)
    y = xf * lax.rsqrt(jnp.mean(xf * xf, axis=-1, keepdims=True) + RMS_EPS)
    return (y * g.astype(jnp.float32)).astype(x.dtype)


def t5_bucket(d):
    max_exact = NUM_BUCKETS // 2
    df = jnp.maximum(d, 1).astype(jnp.float32)
    large = max_exact + (jnp.log(df / max_exact) / math.log(MAX_DISTANCE / max_exact)
                         * (NUM_BUCKETS - max_exact)).astype(jnp.int32)
    large = jnp.minimum(large, NUM_BUCKETS - 1)
    return jnp.where(d < max_exact, d, large)


def rel_bias_lookup(table, d):
    b = table[t5_bucket(jnp.maximum(d, 0))]
    return jnp.transpose(b, (2, 0, 1)).reshape(N_KV_HEADS, GROUP, *d.shape).astype(jnp.float32)


def sink_softmax(s, sink):
    sk = sink.astype(jnp.float32).reshape(N_KV_HEADS, GROUP)[:, :, None, None]
    m = jnp.maximum(jnp.max(s, axis=-1, keepdims=True), sk)
    e = jnp.exp(s - m)
    return e / (jnp.sum(e, axis=-1, keepdims=True) + jnp.exp(sk - m))


def split_proj(z):
    o1 = Q_DIM
    o2 = o1 + KV_DIM
    o3 = o2 + KV_DIM
    o4 = o3 + CONV_DIM
    o5 = o4 + CONV_DIM
    return jnp.split(z, [o1, o2, o3, o4, o5], axis=-1)


def band_pattern(L):
    pad = (-L) % BLOCK
    nb = (L + pad) // BLOCK
    i = jnp.arange(BLOCK)[:, None]
    j = jnp.arange(2 * BLOCK)[None, :]
    d = BLOCK + i - j
    n = jnp.arange(nb)[:, None, None]
    k_abs = (n - 1) * BLOCK + j
    mask = (d >= 0) & (d <= WINDOW) & (k_abs >= pad)
    return d, mask


def swa_prompt(q, k, v, sink, bias, mask):
    B, L = q.shape[:2]
    pad = (-L) % BLOCK
    nb = (L + pad) // BLOCK
    qp = (jnp.pad(q, ((0, 0), (pad, 0), (0, 0))) * SCALE).reshape(
        B, nb, BLOCK, N_KV_HEADS, GROUP, HEAD_DIM)

    def band(t):
        tp = jnp.pad(t, ((0, 0), (pad, 0), (0, 0))).reshape(B, nb, BLOCK, N_KV_HEADS, HEAD_DIM)
        prev = jnp.pad(tp[:, :-1], ((0, 0), (1, 0), (0, 0), (0, 0), (0, 0)))
        return jnp.concatenate([prev, tp], axis=2)

    kk, vv = band(k), band(v)
    s = jnp.einsum('bnqkgd,bnskd->bnkgqs', qp, kk).astype(jnp.float32) + bias
    s = jnp.where(mask[None, :, None, None], s, -jnp.inf)
    p = sink_softmax(s, sink).astype(v.dtype)
    o = jnp.einsum('bnkgqs,bnskd->bnqkgd', p, vv).reshape(B, nb * BLOCK, Q_DIM)
    return o[:, pad:]


def swa_sample(q, k, v, kbuf, vbuf, sink, table):
    DB, T = q.shape[:2]
    WB = kbuf.shape[1]
    qs = (q * SCALE).reshape(DB, T, N_KV_HEADS, GROUP, HEAD_DIM)
    kk = jnp.concatenate([kbuf.astype(k.dtype), k.reshape(DB, T, N_KV_HEADS, HEAD_DIM)], axis=1)
    vv = jnp.concatenate([vbuf.astype(v.dtype), v.reshape(DB, T, N_KV_HEADS, HEAD_DIM)], axis=1)
    d = (WB + jnp.arange(T))[:, None] - jnp.arange(WB + T)[None, :]
    mask = (d >= 0) & (d <= WINDOW)
    s = jnp.einsum('btkgd,bskd->bkgts', qs, kk).astype(jnp.float32) + rel_bias_lookup(table, d)
    s = jnp.where(mask, s, -jnp.inf)
    p = sink_softmax(s, sink).astype(v.dtype)
    o = jnp.einsum('bkgts,bskd->btkgd', p, vv).reshape(DB, T, Q_DIM)
    return o, kk[:, -WB:], vv[:, -WB:]


def causal_conv(u_ext, w, T):
    return sum(w[i] * u_ext[:, i:i + T] for i in range(CONV_WIDTH))


def swiglu(x, wg, wu, wd):
    return (jax.nn.silu(x @ wg) * (x @ wu)) @ wd


def setup_inputs(seed: int = 0) -> dict:
    key = jax.random.key(seed)
    ks = jax.random.split(key, 20)
    W_BUF = min(WINDOW, PAST_LEN)
    nrm = jax.random.normal
    return {
        "x_prompt": nrm(ks[0], (BATCH, SEQ, D_MODEL), jnp.float32),
        "x_sample": nrm(ks[1], (DEC_BATCH, DEC_SEQ, D_MODEL), jnp.float32),
        "cache_k": nrm(ks[2], (DEPTH, DEC_BATCH, W_BUF, N_KV_HEADS, HEAD_DIM), jnp.float32),
        "cache_v": nrm(ks[3], (DEPTH, DEC_BATCH, W_BUF, N_KV_HEADS, HEAD_DIM), jnp.float32),
        "state_conv": nrm(ks[4], (DEPTH, DEC_BATCH, CONV_WIDTH - 1, CONV_DIM), jnp.float32),
        "meta_tokens": nrm(ks[5], (N_META, D_MODEL), jnp.float32),
        "rel_bias": 0.5 * nrm(ks[6], (NUM_BUCKETS, N_HEADS), jnp.float32),
        "w_in": nrm(ks[7], (DEPTH, D_MODEL, IN_DIM), jnp.float32) * D_MODEL ** -0.5,
        "conv_w": nrm(ks[8], (DEPTH, CONV_WIDTH, CONV_DIM), jnp.float32) * CONV_WIDTH ** -0.5,
        "attn_sinks": nrm(ks[9], (DEPTH, N_HEADS), jnp.float32),
        "w_out": nrm(ks[10], (DEPTH, MIX_DIM, D_MODEL), jnp.float32) * MIX_DIM ** -0.5,
        "norm_pre_mix": 1.0 + 0.05 * nrm(ks[11], (DEPTH, D_MODEL), jnp.float32),
        "norm_post_mix": 1.0 + 0.05 * nrm(ks[12], (DEPTH, D_MODEL), jnp.float32),
        "norm_pre_ffn": 1.0 + 0.05 * nrm(ks[13], (DEPTH, D_MODEL), jnp.float32),
        "norm_post_ffn": 1.0 + 0.05 * nrm(ks[14], (DEPTH, D_MODEL), jnp.float32),
        "w_gate": nrm(ks[15], (DEPTH, D_MODEL, D_FF), jnp.float32) * D_MODEL ** -0.5,
        "w_up": nrm(ks[16], (DEPTH, D_MODEL, D_FF), jnp.float32) * D_MODEL ** -0.5,
        "w_down": nrm(ks[17], (DEPTH, D_FF, D_MODEL), jnp.float32) * D_FF ** -0.5,
    }


def reference(x_prompt, x_sample, cache_k, cache_v, state_conv, meta_tokens, rel_bias,
              w_in, conv_w, attn_sinks, w_out, norm_pre_mix, norm_post_mix,
              norm_pre_ffn, norm_post_ffn, w_gate, w_up, w_down):
    B = x_prompt.shape[0]
    meta = jnp.broadcast_to(meta_tokens[None].astype(x_prompt.dtype), (B, N_META, D_MODEL))
    hp = jnp.concatenate([meta, x_prompt], axis=1)
    hs = x_sample
    L = hp.shape[1]
    T = hs.shape[1]
    d_band, mask_band = band_pattern(L)
    bias_band = rel_bias_lookup(rel_bias, d_band)

    kp_l, vp_l, cp_l, ks_l, vs_l, cs_l = [], [], [], [], [], []
    for l in range(DEPTH):
        xn = rms_norm(hp, norm_pre_mix[l])
        q, k, v, gb, gc, hc = split_proj(xn @ w_in[l])
        a = swa_prompt(q, k, v, attn_sinks[l], bias_band, mask_band)
        u = gc * hc
        u_ext = jnp.pad(u, ((0, 0), (CONV_WIDTH - 1, 0), (0, 0)))
        c = gb * causal_conv(u_ext, conv_w[l], L)
        mix = jnp.concatenate([a, c], axis=-1) @ w_out[l]
        hp = hp + rms_norm(mix, norm_post_mix[l])
        kp_l.append(k.reshape(B, L, N_KV_HEADS, HEAD_DIM)[:, -WINDOW:])
        vp_l.append(v.reshape(B, L, N_KV_HEADS, HEAD_DIM)[:, -WINDOW:])
        cp_l.append(u[:, -(CONV_WIDTH - 1):])
        hp = hp + rms_norm(swiglu(rms_norm(hp, norm_pre_ffn[l]), w_gate[l], w_up[l], w_down[l]),
                           norm_post_ffn[l])

        xn = rms_norm(hs, norm_pre_mix[l])
        q, k, v, gb, gc, hc = split_proj(xn @ w_in[l])
        a, kb_new, vb_new = swa_sample(q, k, v, cache_k[l], cache_v[l], attn_sinks[l], rel_bias)
        u = gc * hc
        u_ext = jnp.concatenate([state_conv[l].astype(u.dtype), u], axis=1)
        c = gb * causal_conv(u_ext, conv_w[l], T)
        mix = jnp.concatenate([a, c], axis=-1) @ w_out[l]
        hs = hs + rms_norm(mix, norm_post_mix[l])
        ks_l.append(kb_new)
        vs_l.append(vb_new)
        cs_l.append(u_ext[:, -(CONV_WIDTH - 1):])
        hs = hs + rms_norm(swiglu(rms_norm(hs, norm_pre_ffn[l]), w_gate[l], w_up[l], w_down[l]),
                           norm_post_ffn[l])

    y_prompt = hp[:, N_META:]
    y_sample = hs
    k_prompt = jnp.stack(kp_l)
    v_prompt = jnp.stack(vp_l)
    conv_prompt = jnp.stack(cp_l)
    k_sample = jnp.stack(ks_l)
    v_sample = jnp.stack(vs_l)
    conv_sample = jnp.stack(cs_l)
    return (y_prompt, y_sample, k_prompt, v_prompt, conv_prompt, k_sample, v_sample, conv_sample)
```

```python
import functools
import math

import numpy as np
import jax
import jax.numpy as jnp
from jax import lax
from jax.experimental import pallas as pl
from jax.experimental.pallas import tpu as pltpu

D_MODEL = 2048
BATCH = 4
SEQ = 2048
DEPTH = 4
DEC_BATCH = 32
HEAD_DIM = 64
N_HEADS = 16
N_KV_HEADS = 4
GROUP = 4
Q_DIM = 1024
KV_DIM = 256
CONV_DIM = 1024
CONV_WIDTH = 3
IN_DIM = 4608
WINDOW = 128
BLOCK = 128
NUM_BUCKETS = 32
MAX_DISTANCE = 128
N_META = 16
D_FF = 5632
RMS_EPS = 1e-6
SCALE = HEAD_DIM ** -0.5
W_BUF = 128

M_X = BATCH * SEQ
N_XBLK = M_X // BLOCK
BLK_PER_SEQ = SEQ // BLOCK
M_ALL = M_X + BLOCK
META_ROW0 = BLOCK - N_META
Z1_DIM = Q_DIM + 2 * KV_DIM + CONV_DIM
NEG = -1e30

TM_BIG = 1664
TM_EPI = 832
TN = 512
VMEM_LIMIT = 56 * 1024 * 1024

F32 = jnp.float32
BF16 = jnp.bfloat16


def _cparams(sem):
    return pltpu.CompilerParams(dimension_semantics=sem, vmem_limit_bytes=VMEM_LIMIT)


def _rms(x, g):
    return x * lax.rsqrt(jnp.mean(x * x, axis=-1, keepdims=True) + RMS_EPS) * g


def _norm_kernel(h_ref, g_ref, xn_ref):
    xn_ref[...] = _rms(h_ref[...], g_ref[...]).astype(BF16)


def _norm(h, g):
    return pl.pallas_call(
        _norm_kernel,
        out_shape=jax.ShapeDtypeStruct((M_ALL, D_MODEL), BF16),
        grid=(M_ALL // TM_EPI,),
        in_specs=[pl.BlockSpec((TM_EPI, D_MODEL), lambda i: (i, 0)),
                  pl.BlockSpec((1, D_MODEL), lambda i: (0, 0))],
        out_specs=pl.BlockSpec((TM_EPI, D_MODEL), lambda i: (i, 0)),
        compiler_params=_cparams(("parallel",)),
        name="norm0",
    )(h, g)


def _matmul_kernel(x_ref, w_ref, o_ref):
    o_ref[...] = jnp.dot(x_ref[...], w_ref[...], preferred_element_type=F32)


def _matmul(x, w, n_cols):
    k = x.shape[1]
    return pl.pallas_call(
        _matmul_kernel,
        out_shape=jax.ShapeDtypeStruct((M_ALL, n_cols), F32),
        grid=(M_ALL // TM_BIG, n_cols // TN),
        in_specs=[pl.BlockSpec((TM_BIG, k), lambda i, j: (i, 0)),
                  pl.BlockSpec((k, TN), lambda i, j: (0, j))],
        out_specs=pl.BlockSpec((TM_BIG, TN), lambda i, j: (i, j)),
        compiler_params=_cparams(("parallel", "arbitrary")),
        name="in_proj",
    )(x, w)


def _gated_kernel(x_ref, wa_ref, wb_ref, o_ref, *, silu):
    x = x_ref[...]
    a = jnp.dot(x, wa_ref[...], preferred_element_type=F32)
    b = jnp.dot(x, wb_ref[...], preferred_element_type=F32)
    if silu:
        a = a * (1.0 / (1.0 + jnp.exp(-a)))
    o_ref[...] = (a * b).astype(o_ref.dtype)


def _gated(x, wa, wb, a_col0, b_col0, n_cols, silu, out_dtype, name):
    k = x.shape[1]
    a0 = a_col0 // TN
    b0 = b_col0 // TN
    return pl.pallas_call(
        functools.partial(_gated_kernel, silu=silu),
        out_shape=jax.ShapeDtypeStruct((M_ALL, n_cols), out_dtype),
        grid=(M_ALL // TM_BIG, n_cols // TN),
        in_specs=[pl.BlockSpec((TM_BIG, k), lambda i, j: (i, 0)),
                  pl.BlockSpec((k, TN), lambda i, j: (0, a0 + j)),
                  pl.BlockSpec((k, TN), lambda i, j: (0, b0 + j))],
        out_specs=pl.BlockSpec((TM_BIG, TN), lambda i, j: (i, j)),
        compiler_params=_cparams(("parallel", "arbitrary")),
        name=name,
    )(x, wa, wb)


def _proj_norm_kernel(a_ref, w_ref, hin_ref, gpost_ref, gnext_ref, h_ref, xn_ref):
    kk = pl.program_id(1)
    part = jnp.dot(a_ref[...], w_ref[...], preferred_element_type=F32)

    @pl.when(kk == 0)
    def _():
        h_ref[...] = part

    @pl.when(kk > 0)
    def _():
        h_ref[...] += part

    @pl.when(kk == pl.num_programs(1) - 1)
    def _():
        hn = hin_ref[...] + _rms(h_ref[...], gpost_ref[...])
        h_ref[...] = hn
        xn_ref[...] = _rms(hn, gnext_ref[...]).astype(BF16)


def _proj_norm(a, w, hin, g_post, g_next, tk, name):
    k = a.shape[1]
    return pl.pallas_call(
        _proj_norm_kernel,
        out_shape=(jax.ShapeDtypeStruct((M_ALL, D_MODEL), F32),
                   jax.ShapeDtypeStruct((M_ALL, D_MODEL), BF16)),
        grid=(M_ALL // TM_EPI, k // tk),
        in_specs=[pl.BlockSpec((TM_EPI, tk), lambda i, kk: (i, kk)),
                  pl.BlockSpec((tk, D_MODEL), lambda i, kk: (kk, 0)),
                  pl.BlockSpec((TM_EPI, D_MODEL), lambda i, kk: (i, 0)),
                  pl.BlockSpec((1, D_MODEL), lambda i, kk: (0, 0)),
                  pl.BlockSpec((1, D_MODEL), lambda i, kk: (0, 0))],
        out_specs=(pl.BlockSpec((TM_EPI, D_MODEL), lambda i, kk: (i, 0)),
                   pl.BlockSpec((TM_EPI, D_MODEL), lambda i, kk: (i, 0))),
        compiler_params=_cparams(("parallel", "arbitrary")),
        name=name,
    )(a, w, hin, g_post, g_next)


def _band_bias_kernel(tab_ref, bucket_ref, valid_ref, o_ref):
    h = pl.program_id(1)
    bucket = bucket_ref[...]
    acc = jnp.zeros((BLOCK, 2 * BLOCK), F32)
    for b in range(NUM_BUCKETS):
        acc = jnp.where(bucket == b, tab_ref[b, h], acc)
    o_ref[0, 0] = jnp.where(valid_ref[0] != 0, acc, NEG)


def _band_bias(rel_bias, bucket, valid):
    return pl.pallas_call(
        _band_bias_kernel,
        out_shape=jax.ShapeDtypeStruct((3, N_HEADS, BLOCK, 2 * BLOCK), F32),
        grid=(3, N_HEADS),
        in_specs=[pl.BlockSpec(memory_space=pltpu.SMEM),
                  pl.BlockSpec((BLOCK, 2 * BLOCK), lambda v, h: (0, 0)),
                  pl.BlockSpec((1, BLOCK, 2 * BLOCK), lambda v, h: (v, 0, 0))],
        out_specs=pl.BlockSpec((1, 1, BLOCK, 2 * BLOCK), lambda v, h: (v, h, 0, 0)),
        compiler_params=_cparams(("arbitrary", "arbitrary")),
        name="band_bias",
    )(rel_bias, bucket, valid)


def _mixer_kernel(sink_ref, z_ref, zp_ref, u_ref, up_ref, bias_ref, selk_ref, selv_ref,
                  ones_ref, cw_ref, mixs_ref, mix_ref, ext_ref):
    i = pl.program_id(0)
    is_tail = i == N_XBLK

    q = (z_ref[:, 0:Q_DIM] * SCALE).astype(BF16)
    kk = jnp.concatenate([zp_ref[:, 0:KV_DIM], z_ref[:, Q_DIM:Q_DIM + KV_DIM]],
                         axis=0).astype(BF16)
    vv = jnp.concatenate([zp_ref[:, KV_DIM:2 * KV_DIM],
                          z_ref[:, Q_DIM + KV_DIM:Q_DIM + 2 * KV_DIM]], axis=0).astype(BF16)
    k2 = jnp.dot(kk, selk_ref[...], preferred_element_type=F32).astype(BF16)
    r_all = (jnp.dot(vv, selv_ref[...], preferred_element_type=F32)
             + ones_ref[...]).astype(BF16)

    lane = lax.broadcasted_iota(jnp.int32, (BLOCK, BLOCK), 1)
    lo_half = lane < HEAD_DIM
    zero = jnp.zeros((BLOCK, BLOCK), BF16)
    for p in range(N_HEADS // 2):
        kh = p // 2
        qp = q[:, p * BLOCK:(p + 1) * BLOCK]
        k2h = k2[:, kh * BLOCK:(kh + 1) * BLOCK]
        acc = None
        sink_terms = []
        for half in range(2):
            h = 2 * p + half
            qm = jnp.where(lo_half if half == 0 else jnp.logical_not(lo_half), qp, zero)
            s = lax.dot_general(qm, k2h, (((1,), (1,)), ((), ())),
                                preferred_element_type=F32) + bias_ref[0, h]
            sk = sink_ref[h]
            m = jnp.maximum(jnp.max(s, axis=-1, keepdims=True), sk)
            e = jnp.exp(s - m).astype(BF16)
            c0 = (kh * 2 + half) * 2 * BLOCK
            d = jnp.dot(e, r_all[:, c0:c0 + 2 * BLOCK], preferred_element_type=F32)
            acc = d if acc is None else acc + d
            sink_terms.append(jnp.exp(sk - m))
        den = acc[:, BLOCK:] + jnp.where(lo_half, sink_terms[0], sink_terms[1])
        mix_ref[:, p * BLOCK:(p + 1) * BLOCK] = (acc[:, :BLOCK] / den).astype(BF16)

    ext_ref[0:8, :] = up_ref[...]
    ext_ref[8:8 + BLOCK, :] = u_ref[...]
    row = lax.broadcasted_iota(jnp.int32, (BLOCK, CONV_DIM), 0)
    first = jnp.where(is_tail, META_ROW0, -8)
    u1 = jnp.where(row >= first + 1, ext_ref[7:7 + BLOCK, :], 0.0)
    u2 = jnp.where(row >= first + 2, ext_ref[6:6 + BLOCK, :], 0.0)
    gb = z_ref[:, Q_DIM + 2 * KV_DIM:Z1_DIM]
    c = gb * (cw_ref[0:1, :] * u2 + cw_ref[1:2, :] * u1 + cw_ref[2:3, :] * u_ref[...])
    mix_ref[:, Q_DIM:] = c.astype(BF16)

    @pl.when(is_tail)
    def _():
        mix_ref[0:DEC_BATCH, :] = mixs_ref[...].astype(BF16)


def _prev_blk(i):
    return jnp.where(i % BLK_PER_SEQ == 0, N_XBLK, i - 1)


def _bias_variant(i):
    return jnp.where(i == N_XBLK, 0, jnp.where(i % BLK_PER_SEQ == 0, 1, 2))


def _mixer(sinks, z1, u, bias3, selk, selv, ones_row, cw, mix_s):
    rows8 = BLOCK // 8
    return pl.pallas_call(
        _mixer_kernel,
        out_shape=jax.ShapeDtypeStruct((M_ALL, D_MODEL), BF16),
        grid=(N_XBLK + 1,),
        in_specs=[pl.BlockSpec(memory_space=pltpu.SMEM),
                  pl.BlockSpec((BLOCK, Z1_DIM), lambda i: (i, 0)),
                  pl.BlockSpec((BLOCK, 2 * KV_DIM), lambda i: (_prev_blk(i), Q_DIM // (2 * KV_DIM))),
                  pl.BlockSpec((BLOCK, CONV_DIM), lambda i: (i, 0)),
                  pl.BlockSpec((8, CONV_DIM), lambda i: (_prev_blk(i) * rows8 + rows8 - 1, 0)),
                  pl.BlockSpec((1, N_HEADS, BLOCK, 2 * BLOCK), lambda i: (_bias_variant(i), 0, 0, 0)),
                  pl.BlockSpec((KV_DIM, 2 * KV_DIM), lambda i: (0, 0)),
                  pl.BlockSpec((KV_DIM, 8 * KV_DIM), lambda i: (0, 0)),
                  pl.BlockSpec((1, 8 * KV_DIM), lambda i: (0, 0)),
                  pl.BlockSpec((CONV_WIDTH, CONV_DIM), lambda i: (0, 0)),
                  pl.BlockSpec((DEC_BATCH, D_MODEL), lambda i: (0, 0))],
        out_specs=pl.BlockSpec((BLOCK, D_MODEL), lambda i: (i, 0)),
        scratch_shapes=[pltpu.VMEM((BLOCK + 8, CONV_DIM), F32)],
        compiler_params=_cparams(("arbitrary",)),
        name="mixer",
    )(sinks, z1, z1, u, u, bias3, selk, selv, ones_row, cw, mix_s)


S_CHUNK = 16


def _sample_kernel(zs_ref, us_ref, kc_ref, vc_ref, st_ref, cw_ref, sink_ref, tab_ref, bcol_ref,
                   hsum_ref, hexp_ref, gsel_ref, gselt_ref,
                   mixs_ref, kout_ref, vout_ref, cs_ref,
                   qg_scr, sn_scr, og_scr):
    z = zs_ref[...]
    q = (z[:, 0:Q_DIM] * SCALE).astype(BF16)
    knew = z[:, Q_DIM:Q_DIM + KV_DIM]
    vnew = z[:, Q_DIM + KV_DIM:Q_DIM + 2 * KV_DIM]
    gb = z[:, Q_DIM + 2 * KV_DIM:Z1_DIM]

    sn = jnp.zeros((S_CHUNK, BLOCK), F32)
    for g in range(GROUP):
        qg = jnp.dot(q, gsel_ref[g], preferred_element_type=F32)
        qg_scr[g] = qg
        sn = sn + jnp.dot((knew * qg).astype(BF16), hsum_ref[g], preferred_element_type=F32)
    sn_scr[...] = sn + tab_ref[0:1, :]

    bcol = bcol_ref[...]
    sbias = jnp.zeros((W_BUF, BLOCK), F32)
    for b in range(NUM_BUCKETS):
        sbias = jnp.where(bcol == b, tab_ref[b:b + 1, :], sbias)
    sink = sink_ref[...]
    row = lax.broadcasted_iota(jnp.int32, (W_BUF, KV_DIM), 0)

    def body(s, carry):
        r0 = pl.multiple_of(s * W_BUF, W_BUF)
        ks = kc_ref[pl.ds(r0, W_BUF), :]
        vs = vc_ref[pl.ds(r0, W_BUF), :]
        sc = sbias
        for g in range(GROUP):
            sc = sc + jnp.dot((ks * qg_scr[g, pl.ds(s, 1), :]).astype(BF16), hsum_ref[g],
                              preferred_element_type=F32)
        sn_s = sn_scr[pl.ds(s, 1), :]
        m = jnp.maximum(jnp.maximum(jnp.max(sc, axis=0, keepdims=True), sn_s), sink)
        e = jnp.exp(sc - m)
        en = jnp.exp(sn_s - m)
        den = jnp.sum(e, axis=0, keepdims=True) + en + jnp.exp(sink - m)
        p = (e / den).astype(BF16)
        pn = jnp.broadcast_to(en / den, (8, BLOCK)).astype(BF16)
        k_s = zs_ref[pl.ds(s, 1), Q_DIM:Q_DIM + KV_DIM]
        v_s = zs_ref[pl.ds(s, 1), Q_DIM + KV_DIM:Q_DIM + 2 * KV_DIM]
        for g in range(GROUP):
            pe = jnp.dot(p, hexp_ref[g], preferred_element_type=F32)
            pne = jnp.dot(pn, hexp_ref[g], preferred_element_type=F32)
            og_scr[g, pl.ds(s, 1), :] = (jnp.sum(pe * vs, axis=0, keepdims=True)
                                         + pne[0:1, :] * v_s)
        kout_ref[pl.ds(r0, W_BUF), :] = jnp.where(row == W_BUF - 1, k_s,
                                                 pltpu.roll(ks, W_BUF - 1, 0))
        vout_ref[pl.ds(r0, W_BUF), :] = jnp.where(row == W_BUF - 1, v_s,
                                                 pltpu.roll(vs, W_BUF - 1, 0))
        return carry

    lax.fori_loop(0, S_CHUNK, body, 0)

    a = jnp.zeros((S_CHUNK, Q_DIM), F32)
    for g in range(GROUP):
        a = a + jnp.dot(og_scr[g].astype(BF16), gselt_ref[g], preferred_element_type=F32)
    mixs_ref[:, 0:Q_DIM] = a

    u = us_ref[...]
    s0 = st_ref[:, 0:CONV_DIM]
    s1 = st_ref[:, CONV_DIM:]
    mixs_ref[:, Q_DIM:] = gb * (cw_ref[0:1, :] * s0 + cw_ref[1:2, :] * s1 + cw_ref[2:3, :] * u)
    cs_ref[:, 0:CONV_DIM] = s1
    cs_ref[:, CONV_DIM:] = u


def _sample_mixer(layer, z1, u, cache_k2d, cache_v2d, state2d, cw, sink_row, tab_pad, bcol,
                  hsum, hexp, gsel, gselt):
    n_chunks = DEC_BATCH // S_CHUNK
    row_blk0 = M_X // S_CHUNK
    full = lambda shape: pl.BlockSpec(shape, lambda c: (0,) * len(shape))
    return pl.pallas_call(
        _sample_kernel,
        out_shape=(jax.ShapeDtypeStruct((DEC_BATCH, D_MODEL), F32),
                   jax.ShapeDtypeStruct((DEC_BATCH * W_BUF, KV_DIM), F32),
                   jax.ShapeDtypeStruct((DEC_BATCH * W_BUF, KV_DIM), F32),
                   jax.ShapeDtypeStruct((DEC_BATCH, 2 * CONV_DIM), F32)),
        grid=(n_chunks,),
        in_specs=[pl.BlockSpec((S_CHUNK, Z1_DIM), lambda c: (row_blk0 + c, 0)),
                  pl.BlockSpec((S_CHUNK, CONV_DIM), lambda c: (row_blk0 + c, 0)),
                  pl.BlockSpec((S_CHUNK * W_BUF, KV_DIM), lambda c: (layer * n_chunks + c, 0)),
                  pl.BlockSpec((S_CHUNK * W_BUF, KV_DIM), lambda c: (layer * n_chunks + c, 0)),
                  pl.BlockSpec((S_CHUNK, 2 * CONV_DIM), lambda c: (layer * n_chunks + c, 0)),
                  full((CONV_WIDTH, CONV_DIM)),
                  full((1, BLOCK)),
                  full((NUM_BUCKETS, BLOCK)),
                  full((W_BUF, 1)),
                  full((GROUP, KV_DIM, BLOCK)),
                  full((GROUP, BLOCK, KV_DIM)),
                  full((GROUP, Q_DIM, KV_DIM)),
                  full((GROUP, KV_DIM, Q_DIM))],
        out_specs=(pl.BlockSpec((S_CHUNK, D_MODEL), lambda c: (c, 0)),
                   pl.BlockSpec((S_CHUNK * W_BUF, KV_DIM), lambda c: (c, 0)),
                   pl.BlockSpec((S_CHUNK * W_BUF, KV_DIM), lambda c: (c, 0)),
                   pl.BlockSpec((S_CHUNK, 2 * CONV_DIM), lambda c: (c, 0))),
        scratch_shapes=[pltpu.VMEM((GROUP, S_CHUNK, KV_DIM), F32),
                        pltpu.VMEM((S_CHUNK, BLOCK), F32),
                        pltpu.VMEM((GROUP, S_CHUNK, KV_DIM), F32)],
        compiler_params=_cparams(("arbitrary",)),
        name="sample_mixer",
    )(z1, u, cache_k2d, cache_v2d, state2d, cw, sink_row, tab_pad, bcol, hsum, hexp, gsel, gselt)


def _t5_bucket(d):
    max_exact = NUM_BUCKETS // 2
    df = jnp.maximum(d, 1).astype(F32)
    large = max_exact + (jnp.log(df / max_exact) / math.log(MAX_DISTANCE / max_exact)
                         * (NUM_BUCKETS - max_exact)).astype(jnp.int32)
    large = jnp.minimum(large, NUM_BUCKETS - 1)
    return jnp.where(d < max_exact, d, large)


def _band_tables():
    i = np.arange(BLOCK)[:, None]
    j = np.arange(2 * BLOCK)[None, :]
    d = BLOCK + i - j
    band = (d >= 0) & (d <= WINDOW)
    valid = np.stack([band & (j >= BLOCK + META_ROW0), band & (j >= META_ROW0), band])
    bucket = _t5_bucket(jnp.asarray(np.maximum(d, 0), jnp.int32))
    return bucket, jnp.asarray(valid.astype(np.int32))


def _selection_tables():
    d = np.arange(HEAD_DIM)
    selk = np.zeros((KV_DIM, N_KV_HEADS * BLOCK), np.float32)
    selv = np.zeros((KV_DIM, N_KV_HEADS * 2 * 2 * BLOCK), np.float32)
    ones_row = np.zeros((1, N_KV_HEADS * 2 * 2 * BLOCK), np.float32)
    for kh in range(N_KV_HEADS):
        selk[kh * HEAD_DIM + d, kh * BLOCK + d] = 1.0
        selk[kh * HEAD_DIM + d, kh * BLOCK + HEAD_DIM + d] = 1.0
        for half in range(2):
            c0 = (kh * 2 + half) * 2 * BLOCK
            selv[kh * HEAD_DIM + d, c0 + half * HEAD_DIM + d] = 1.0
            ones_row[0, c0 + BLOCK + half * HEAD_DIM + d] = 1.0
    hsum = np.zeros((GROUP, KV_DIM, BLOCK), np.float32)
    gsel = np.zeros((GROUP, Q_DIM, KV_DIM), np.float32)
    for g in range(GROUP):
        for kh in range(N_KV_HEADS):
            hsum[g, kh * HEAD_DIM + d, kh * GROUP + g] = 1.0
            gsel[g, (kh * GROUP + g) * HEAD_DIM + d, kh * HEAD_DIM + d] = 1.0
    hexp = np.transpose(hsum, (0, 2, 1))
    gselt = np.transpose(gsel, (0, 2, 1))
    bf = lambda a: jnp.asarray(a, BF16)
    return bf(selk), bf(selv), jnp.asarray(ones_row), bf(hsum), bf(hexp), bf(gsel), bf(gselt)


def kernel(x_prompt, x_sample, cache_k, cache_v, state_conv, meta_tokens, rel_bias, w_in, conv_w,
           attn_sinks, w_out, norm_pre_mix, norm_post_mix, norm_pre_ffn, norm_post_ffn,
           w_gate, w_up, w_down):
    tail = jnp.concatenate([x_sample.reshape(DEC_BATCH, D_MODEL),
                            jnp.zeros((META_ROW0 - DEC_BATCH, D_MODEL), F32),
                            meta_tokens.astype(F32)], axis=0)
    h = jnp.concatenate([x_prompt.reshape(M_X, D_MODEL), tail], axis=0)

    bucket, valid = _band_tables()
    selk, selv, ones_row, hsum, hexp, gsel, gselt = _selection_tables()
    bias3 = _band_bias(rel_bias, bucket, valid)
    tab_pad = jnp.pad(rel_bias, ((0, 0), (0, BLOCK - N_HEADS)))
    sink_pad = jnp.pad(attn_sinks, ((0, 0), (0, BLOCK - N_HEADS)))
    bcol = _t5_bucket(jnp.asarray(W_BUF - np.arange(W_BUF), jnp.int32)).reshape(W_BUF, 1)

    cache_k2d = cache_k.reshape(DEPTH * DEC_BATCH * W_BUF, KV_DIM)
    cache_v2d = cache_v.reshape(DEPTH * DEC_BATCH * W_BUF, KV_DIM)
    state2d = state_conv.reshape(DEPTH * DEC_BATCH, 2 * CONV_DIM)

    w_in_b = w_in.astype(BF16)
    w_out_b = w_out.astype(BF16)
    w_gate_b = w_gate.astype(BF16)
    w_up_b = w_up.astype(BF16)
    w_down_b = w_down.astype(BF16)

    g = lambda a, l: a[l].reshape(1, D_MODEL)
    xn = _norm(h, g(norm_pre_mix, 0))

    kp, vp, cp, ks, vs, cs = [], [], [], [], [], []
    for l in range(DEPTH):
        z1 = _matmul(xn, w_in_b[l], Z1_DIM)
        u = _gated(xn, w_in_b[l], w_in_b[l], Z1_DIM, Z1_DIM + CONV_DIM, CONV_DIM,
                   False, F32, "conv_in")
        mix_s, k_new, v_new, c_new = _sample_mixer(
            l, z1, u, cache_k2d, cache_v2d, state2d, conv_w[l],
            sink_pad[l].reshape(1, BLOCK), tab_pad, bcol, hsum, hexp, gsel, gselt)
        mix = _mixer(attn_sinks[l], z1, u, bias3, selk, selv, ones_row, conv_w[l], mix_s)
        h, xn = _proj_norm(mix, w_out_b[l], h, g(norm_post_mix, l), g(norm_pre_ffn, l),
                           D_MODEL, "out_proj")
        act = _gated(xn, w_gate_b[l], w_up_b[l], 0, 0, D_FF, True, BF16, "ffn_up")
        g_next = g(norm_pre_mix, l + 1) if l + 1 < DEPTH else g(norm_pre_mix, 0)
        h, xn = _proj_norm(act, w_down_b[l], h, g(norm_post_ffn, l), g_next, TN, "ffn_down")

        zx = z1[:M_X].reshape(BATCH, SEQ, Z1_DIM)
        kp.append(zx[:, SEQ - WINDOW:, Q_DIM:Q_DIM + KV_DIM].reshape(BATCH, WINDOW, N_KV_HEADS, HEAD_DIM))
        vp.append(zx[:, SEQ - WINDOW:, Q_DIM + KV_DIM:Q_DIM + 2 * KV_DIM]
                  .reshape(BATCH, WINDOW, N_KV_HEADS, HEAD_DIM))
        cp.append(u[:M_X].reshape(BATCH, SEQ, CONV_DIM)[:, SEQ - (CONV_WIDTH - 1):])
        ks.append(k_new.reshape(DEC_BATCH, W_BUF, N_KV_HEADS, HEAD_DIM))
        vs.append(v_new.reshape(DEC_BATCH, W_BUF, N_KV_HEADS, HEAD_DIM))
        cs.append(c_new.reshape(DEC_BATCH, CONV_WIDTH - 1, CONV_DIM))

    y_prompt = h[:M_X].reshape(BATCH, SEQ, D_MODEL)
    y_sample = h[M_X:M_X + DEC_BATCH].reshape(DEC_BATCH, 1, D_MODEL)
    return (y_prompt, y_sample, jnp.stack(kp), jnp.stack(vp), jnp.stack(cp),
            jnp.stack(ks), jnp.stack(vs), jnp.stack(cs))
```

```python
import functools
import math

import numpy as np
import jax
import jax.numpy as jnp
from jax import lax
from jax.experimental import pallas as pl
from jax.experimental.pallas import tpu as pltpu

D_MODEL = 2048
BATCH = 4
SEQ = 2048
DEPTH = 4
DEC_BATCH = 32
HEAD_DIM = 64
N_HEADS = 16
N_KV_HEADS = 4
GROUP = 4
Q_DIM = 1024
KV_DIM = 256
CONV_DIM = 1024
CONV_WIDTH = 3
IN_DIM = 4608
WINDOW = 128
BLOCK = 128
NUM_BUCKETS = 32
MAX_DISTANCE = 128
N_META = 16
D_FF = 5632
RMS_EPS = 1e-6
SCALE = HEAD_DIM ** -0.5
W_BUF = 128

M_X = BATCH * SEQ
N_XBLK = M_X // BLOCK
BLK_PER_SEQ = SEQ // BLOCK
M_ALL = M_X + BLOCK
META_ROW0 = BLOCK - N_META
Z1_DIM = Q_DIM + 2 * KV_DIM + CONV_DIM
NEG = -1e30

TM_BIG = 1664
TM_EPI = 832
TM_PROJ = 416
RC_PROJ = 208
TN = 512
VMEM_LIMIT = 56 * 1024 * 1024

F32 = jnp.float32
BF16 = jnp.bfloat16


def _cparams(sem):
    return pltpu.CompilerParams(dimension_semantics=sem, vmem_limit_bytes=VMEM_LIMIT)


def _rms(x, g):
    return x * lax.rsqrt(jnp.mean(x * x, axis=-1, keepdims=True) + RMS_EPS) * g


def _norm_kernel(h_ref, g_ref, xn_ref):
    xn_ref[...] = _rms(h_ref[...], g_ref[0:1, :]).astype(BF16)


def _norm(h, g):
    return pl.pallas_call(
        _norm_kernel,
        out_shape=jax.ShapeDtypeStruct((M_ALL, D_MODEL), BF16),
        grid=(M_ALL // TM_EPI,),
        in_specs=[pl.BlockSpec((TM_EPI, D_MODEL), lambda i: (i, 0)),
                  pl.BlockSpec((DEPTH, D_MODEL), lambda i: (0, 0))],
        out_specs=pl.BlockSpec((TM_EPI, D_MODEL), lambda i: (i, 0)),
        compiler_params=_cparams(("parallel",)),
        name="norm0",
    )(h, g)


def _matmul_kernel(x_ref, w_ref, o_ref):
    o_ref[...] = jnp.dot(x_ref[...], w_ref[...], preferred_element_type=F32)


def _matmul(x, w, layer, n_cols):
    k = x.shape[1]
    return pl.pallas_call(
        _matmul_kernel,
        out_shape=jax.ShapeDtypeStruct((M_ALL, n_cols), F32),
        grid=(M_ALL // TM_BIG, n_cols // TN),
        in_specs=[pl.BlockSpec((TM_BIG, k), lambda i, j: (i, 0)),
                  pl.BlockSpec((None, k, TN), lambda i, j: (layer, 0, j))],
        out_specs=pl.BlockSpec((TM_BIG, TN), lambda i, j: (i, j)),
        compiler_params=_cparams(("parallel", "arbitrary")),
        name="in_proj",
    )(x, w)


def _gated_kernel(x_ref, wa_ref, wb_ref, o_ref, *, silu):
    x = x_ref[...]
    a = jnp.dot(x, wa_ref[...], preferred_element_type=F32)
    b = jnp.dot(x, wb_ref[...], preferred_element_type=F32)
    if silu:
        a = a * (1.0 / (1.0 + jnp.exp(-a)))
    o_ref[...] = (a * b).astype(o_ref.dtype)


def _gated(x, wa, wb, layer, a_col0, b_col0, n_cols, silu, out_dtype, name):
    k = x.shape[1]
    a0 = a_col0 // TN
    b0 = b_col0 // TN
    return pl.pallas_call(
        functools.partial(_gated_kernel, silu=silu),
        out_shape=jax.ShapeDtypeStruct((M_ALL, n_cols), out_dtype),
        grid=(M_ALL // TM_BIG, n_cols // TN),
        in_specs=[pl.BlockSpec((TM_BIG, k), lambda i, j: (i, 0)),
                  pl.BlockSpec((None, k, TN), lambda i, j: (layer, 0, a0 + j)),
                  pl.BlockSpec((None, k, TN), lambda i, j: (layer, 0, b0 + j))],
        out_specs=pl.BlockSpec((TM_BIG, TN), lambda i, j: (i, j)),
        compiler_params=_cparams(("parallel", "arbitrary")),
        name=name,
    )(x, wa, wb)


def _proj_norm_kernel(a_ref, w_ref, hin_ref, gpost_ref, gnext_ref, h_ref, xn_ref, *, l_post, l_next):
    g_post = gpost_ref[l_post:l_post + 1, :]
    g_next = gnext_ref[l_next:l_next + 1, :]
    for c in range(TM_PROJ // RC_PROJ):
        rows = slice(c * RC_PROJ, (c + 1) * RC_PROJ)
        y = jnp.dot(a_ref[rows, :], w_ref[...], preferred_element_type=F32)
        hn = hin_ref[rows, :] + _rms(y, g_post)
        h_ref[rows, :] = hn
        xn_ref[rows, :] = _rms(hn, g_next).astype(BF16)


def _proj_norm(a, w, layer, hin, g_post, g_next, l_next, name):
    k = a.shape[1]
    return pl.pallas_call(
        functools.partial(_proj_norm_kernel, l_post=layer, l_next=l_next),
        out_shape=(jax.ShapeDtypeStruct((M_ALL, D_MODEL), F32),
                   jax.ShapeDtypeStruct((M_ALL, D_MODEL), BF16)),
        grid=(M_ALL // TM_PROJ,),
        in_specs=[pl.BlockSpec((TM_PROJ, k), lambda i: (i, 0)),
                  pl.BlockSpec((None, k, D_MODEL), lambda i: (layer, 0, 0),
                               pipeline_mode=pl.Buffered(1)),
                  pl.BlockSpec((TM_PROJ, D_MODEL), lambda i: (i, 0)),
                  pl.BlockSpec((DEPTH, D_MODEL), lambda i: (0, 0)),
                  pl.BlockSpec((DEPTH, D_MODEL), lambda i: (0, 0))],
        out_specs=(pl.BlockSpec((TM_PROJ, D_MODEL), lambda i: (i, 0)),
                   pl.BlockSpec((TM_PROJ, D_MODEL), lambda i: (i, 0))),
        compiler_params=_cparams(("parallel",)),
        name=name,
    )(a, w, hin, g_post, g_next)


def _band_bias_kernel(tab_ref, bucket_ref, valid_ref, o_ref):
    h = pl.program_id(1)
    bucket = bucket_ref[...]
    acc = jnp.zeros((BLOCK, 2 * BLOCK), F32)
    for b in range(NUM_BUCKETS):
        acc = jnp.where(bucket == b, tab_ref[b, h], acc)
    o_ref[0, 0] = jnp.where(valid_ref[0] != 0, acc, NEG)


def _band_bias(rel_bias, bucket, valid):
    return pl.pallas_call(
        _band_bias_kernel,
        out_shape=jax.ShapeDtypeStruct((3, N_HEADS, BLOCK, 2 * BLOCK), F32),
        grid=(3, N_HEADS),
        in_specs=[pl.BlockSpec(memory_space=pltpu.SMEM),
                  pl.BlockSpec((BLOCK, 2 * BLOCK), lambda v, h: (0, 0)),
                  pl.BlockSpec((1, BLOCK, 2 * BLOCK), lambda v, h: (v, 0, 0))],
        out_specs=pl.BlockSpec((1, 1, BLOCK, 2 * BLOCK), lambda v, h: (v, h, 0, 0)),
        compiler_params=_cparams(("arbitrary", "arbitrary")),
        name="band_bias",
    )(rel_bias, bucket, valid)


def _mixer_kernel(sink_ref, z_ref, zp_ref, u_ref, up_ref, bias_ref, selk_ref, selv_ref,
                  ones_ref, cw_ref, mixs_ref, mix_ref, kp_ref, vp_ref, cp_ref, ext_ref):
    i = pl.program_id(0)
    is_tail = i == N_XBLK

    @pl.when(jnp.logical_and(i < N_XBLK, i % BLK_PER_SEQ == BLK_PER_SEQ - 1))
    def _():
        kp_ref[0] = z_ref[:, Q_DIM:Q_DIM + KV_DIM].T
        vp_ref[0] = z_ref[:, Q_DIM + KV_DIM:Q_DIM + 2 * KV_DIM].T
        cp_ref[0] = u_ref[BLOCK - (CONV_WIDTH - 1):BLOCK, :]

    q = (z_ref[:, 0:Q_DIM] * SCALE).astype(BF16)
    kk = jnp.concatenate([zp_ref[:, 0:KV_DIM], z_ref[:, Q_DIM:Q_DIM + KV_DIM]],
                         axis=0).astype(BF16)
    vv = jnp.concatenate([zp_ref[:, KV_DIM:2 * KV_DIM],
                          z_ref[:, Q_DIM + KV_DIM:Q_DIM + 2 * KV_DIM]], axis=0).astype(BF16)
    k2 = jnp.dot(kk, selk_ref[...], preferred_element_type=F32).astype(BF16)
    r_all = (jnp.dot(vv, selv_ref[...], preferred_element_type=F32)
             + ones_ref[...]).astype(BF16)

    lane = lax.broadcasted_iota(jnp.int32, (BLOCK, BLOCK), 1)
    lo_half = lane < HEAD_DIM
    zero = jnp.zeros((BLOCK, BLOCK), BF16)
    for p in range(N_HEADS // 2):
        kh = p // 2
        qp = q[:, p * BLOCK:(p + 1) * BLOCK]
        k2h = k2[:, kh * BLOCK:(kh + 1) * BLOCK]
        acc = None
        sink_terms = []
        for half in range(2):
            h = 2 * p + half
            qm = jnp.where(lo_half if half == 0 else jnp.logical_not(lo_half), qp, zero)
            s = lax.dot_general(qm, k2h, (((1,), (1,)), ((), ())),
                                preferred_element_type=F32) + bias_ref[0, h]
            sk = sink_ref[h]
            m = jnp.maximum(jnp.max(s, axis=-1, keepdims=True), sk)
            e = jnp.exp(s - m).astype(BF16)
            c0 = (kh * 2 + half) * 2 * BLOCK
            d = jnp.dot(e, r_all[:, c0:c0 + 2 * BLOCK], preferred_element_type=F32)
            acc = d if acc is None else acc + d
            sink_terms.append(jnp.exp(sk - m))
        den = acc[:, BLOCK:] + jnp.where(lo_half, sink_terms[0], sink_terms[1])
        mix_ref[:, p * BLOCK:(p + 1) * BLOCK] = (acc[:, :BLOCK] / den).astype(BF16)

    ext_ref[0:8, :] = up_ref[...]
    ext_ref[8:8 + BLOCK, :] = u_ref[...]
    row = lax.broadcasted_iota(jnp.int32, (BLOCK, CONV_DIM), 0)
    first = jnp.where(is_tail, META_ROW0, -8)
    u1 = jnp.where(row >= first + 1, ext_ref[7:7 + BLOCK, :], 0.0)
    u2 = jnp.where(row >= first + 2, ext_ref[6:6 + BLOCK, :], 0.0)
    gb = z_ref[:, Q_DIM + 2 * KV_DIM:Z1_DIM]
    c = gb * (cw_ref[0:1, :] * u2 + cw_ref[1:2, :] * u1 + cw_ref[2:3, :] * u_ref[...])
    mix_ref[:, Q_DIM:] = c.astype(BF16)

    @pl.when(is_tail)
    def _():
        mix_ref[0:DEC_BATCH, :] = mixs_ref[...].astype(BF16)


def _prev_blk(i):
    return jnp.where(i % BLK_PER_SEQ == 0, N_XBLK, i - 1)


def _bias_variant(i):
    return jnp.where(i == N_XBLK, 0, jnp.where(i % BLK_PER_SEQ == 0, 1, 2))


def _mixer(sinks, z1, u, bias3, selk, selv, ones_row, cw, mix_s):
    rows8 = BLOCK // 8
    seq_of = lambda i: jnp.minimum(i // BLK_PER_SEQ, BATCH - 1)
    return pl.pallas_call(
        _mixer_kernel,
        out_shape=(jax.ShapeDtypeStruct((M_ALL, D_MODEL), BF16),
                   jax.ShapeDtypeStruct((BATCH, KV_DIM, WINDOW), F32),
                   jax.ShapeDtypeStruct((BATCH, KV_DIM, WINDOW), F32),
                   jax.ShapeDtypeStruct((BATCH, CONV_WIDTH - 1, CONV_DIM), F32)),
        grid=(N_XBLK + 1,),
        in_specs=[pl.BlockSpec(memory_space=pltpu.SMEM),
                  pl.BlockSpec((BLOCK, Z1_DIM), lambda i: (i, 0)),
                  pl.BlockSpec((BLOCK, 2 * KV_DIM), lambda i: (_prev_blk(i), Q_DIM // (2 * KV_DIM))),
                  pl.BlockSpec((BLOCK, CONV_DIM), lambda i: (i, 0)),
                  pl.BlockSpec((8, CONV_DIM), lambda i: (_prev_blk(i) * rows8 + rows8 - 1, 0)),
                  pl.BlockSpec((1, N_HEADS, BLOCK, 2 * BLOCK), lambda i: (_bias_variant(i), 0, 0, 0)),
                  pl.BlockSpec((KV_DIM, 2 * KV_DIM), lambda i: (0, 0)),
                  pl.BlockSpec((KV_DIM, 8 * KV_DIM), lambda i: (0, 0)),
                  pl.BlockSpec((1, 8 * KV_DIM), lambda i: (0, 0)),
                  pl.BlockSpec((CONV_WIDTH, CONV_DIM), lambda i: (0, 0)),
                  pl.BlockSpec((DEC_BATCH, D_MODEL), lambda i: (0, 0))],
        out_specs=(pl.BlockSpec((BLOCK, D_MODEL), lambda i: (i, 0)),
                   pl.BlockSpec((1, KV_DIM, WINDOW), lambda i: (seq_of(i), 0, 0)),
                   pl.BlockSpec((1, KV_DIM, WINDOW), lambda i: (seq_of(i), 0, 0)),
                   pl.BlockSpec((1, CONV_WIDTH - 1, CONV_DIM), lambda i: (seq_of(i), 0, 0))),
        scratch_shapes=[pltpu.VMEM((BLOCK + 8, CONV_DIM), F32)],
        compiler_params=_cparams(("arbitrary",)),
        name="mixer",
    )(sinks, z1, z1, u, u, bias3, selk, selv, ones_row, cw, mix_s)


S_CHUNK = 16


def _sample_kernel(zs_ref, us_ref, kc_ref, vc_ref, st_ref, cw_ref, sink_ref, tab_ref, bcol_ref,
                   hsum_ref, hexp_ref, gsel_ref, gselt_ref,
                   mixs_ref, kout_ref, vout_ref, cs_ref,
                   qg_scr, sn_scr, og_scr):
    z = zs_ref[...]
    q = (z[:, 0:Q_DIM] * SCALE).astype(BF16)
    knew = z[:, Q_DIM:Q_DIM + KV_DIM]
    vnew = z[:, Q_DIM + KV_DIM:Q_DIM + 2 * KV_DIM]
    gb = z[:, Q_DIM + 2 * KV_DIM:Z1_DIM]

    sn = jnp.zeros((S_CHUNK, BLOCK), F32)
    for g in range(GROUP):
        qg = jnp.dot(q, gsel_ref[g], preferred_element_type=F32)
        qg_scr[g] = qg
        sn = sn + jnp.dot((knew * qg).astype(BF16), hsum_ref[g], preferred_element_type=F32)
    sn_scr[...] = sn + tab_ref[0:1, :]

    bcol = bcol_ref[...]
    sbias = jnp.zeros((W_BUF, BLOCK), F32)
    for b in range(NUM_BUCKETS):
        sbias = jnp.where(bcol == b, tab_ref[b:b + 1, :], sbias)
    sink = sink_ref[...]
    row = lax.broadcasted_iota(jnp.int32, (W_BUF, KV_DIM), 0)

    def body(s, carry):
        r0 = pl.multiple_of(s * W_BUF, W_BUF)
        ks = kc_ref[pl.ds(r0, W_BUF), :]
        vs = vc_ref[pl.ds(r0, W_BUF), :]
        sc = sbias
        for g in range(GROUP):
            sc = sc + jnp.dot((ks * qg_scr[g, pl.ds(s, 1), :]).astype(BF16), hsum_ref[g],
                              preferred_element_type=F32)
        sn_s = sn_scr[pl.ds(s, 1), :]
        m = jnp.maximum(jnp.maximum(jnp.max(sc, axis=0, keepdims=True), sn_s), sink)
        e = jnp.exp(sc - m)
        en = jnp.exp(sn_s - m)
        den = jnp.sum(e, axis=0, keepdims=True) + en + jnp.exp(sink - m)
        p = (e / den).astype(BF16)
        pn = jnp.broadcast_to(en / den, (8, BLOCK)).astype(BF16)
        k_s = zs_ref[pl.ds(s, 1), Q_DIM:Q_DIM + KV_DIM]
        v_s = zs_ref[pl.ds(s, 1), Q_DIM + KV_DIM:Q_DIM + 2 * KV_DIM]
        for g in range(GROUP):
            pe = jnp.dot(p, hexp_ref[g], preferred_element_type=F32)
            pne = jnp.dot(pn, hexp_ref[g], preferred_element_type=F32)
            og_scr[g, pl.ds(s, 1), :] = (jnp.sum(pe * vs, axis=0, keepdims=True)
                                         + pne[0:1, :] * v_s)
        kout_ref[pl.ds(r0, W_BUF), :] = jnp.where(row == W_BUF - 1, k_s,
                                                 pltpu.roll(ks, W_BUF - 1, 0))
        vout_ref[pl.ds(r0, W_BUF), :] = jnp.where(row == W_BUF - 1, v_s,
                                                 pltpu.roll(vs, W_BUF - 1, 0))
        return carry

    lax.fori_loop(0, S_CHUNK, body, 0)

    a = jnp.zeros((S_CHUNK, Q_DIM), F32)
    for g in range(GROUP):
        a = a + jnp.dot(og_scr[g].astype(BF16), gselt_ref[g], preferred_element_type=F32)
    mixs_ref[:, 0:Q_DIM] = a

    u = us_ref[...]
    s0 = st_ref[:, 0:CONV_DIM]
    s1 = st_ref[:, CONV_DIM:]
    mixs_ref[:, Q_DIM:] = gb * (cw_ref[0:1, :] * s0 + cw_ref[1:2, :] * s1 + cw_ref[2:3, :] * u)
    cs_ref[:, 0:CONV_DIM] = s1
    cs_ref[:, CONV_DIM:] = u


def _sample_mixer(layer, z1, u, cache_k2d, cache_v2d, state2d, cw, sink_row, tab_pad, bcol,
                  hsum, hexp, gsel, gselt):
    n_chunks = DEC_BATCH // S_CHUNK
    row_blk0 = M_X // S_CHUNK
    full = lambda shape: pl.BlockSpec(shape, lambda c: (0,) * len(shape))
    return pl.pallas_call(
        _sample_kernel,
        out_shape=(jax.ShapeDtypeStruct((DEC_BATCH, D_MODEL), F32),
                   jax.ShapeDtypeStruct((DEC_BATCH * W_BUF, KV_DIM), F32),
                   jax.ShapeDtypeStruct((DEC_BATCH * W_BUF, KV_DIM), F32),
                   jax.ShapeDtypeStruct((DEC_BATCH, 2 * CONV_DIM), F32)),
        grid=(n_chunks,),
        in_specs=[pl.BlockSpec((S_CHUNK, Z1_DIM), lambda c: (row_blk0 + c, 0)),
                  pl.BlockSpec((S_CHUNK, CONV_DIM), lambda c: (row_blk0 + c, 0)),
                  pl.BlockSpec((S_CHUNK * W_BUF, KV_DIM), lambda c: (layer * n_chunks + c, 0)),
                  pl.BlockSpec((S_CHUNK * W_BUF, KV_DIM), lambda c: (layer * n_chunks + c, 0)),
                  pl.BlockSpec((S_CHUNK, 2 * CONV_DIM), lambda c: (layer * n_chunks + c, 0)),
                  full((CONV_WIDTH, CONV_DIM)),
                  full((1, BLOCK)),
                  full((NUM_BUCKETS, BLOCK)),
                  full((W_BUF, 1)),
                  full((GROUP, KV_DIM, BLOCK)),
                  full((GROUP, BLOCK, KV_DIM)),
                  full((GROUP, Q_DIM, KV_DIM)),
                  full((GROUP, KV_DIM, Q_DIM))],
        out_specs=(pl.BlockSpec((S_CHUNK, D_MODEL), lambda c: (c, 0)),
                   pl.BlockSpec((S_CHUNK * W_BUF, KV_DIM), lambda c: (c, 0)),
                   pl.BlockSpec((S_CHUNK * W_BUF, KV_DIM), lambda c: (c, 0)),
                   pl.BlockSpec((S_CHUNK, 2 * CONV_DIM), lambda c: (c, 0))),
        scratch_shapes=[pltpu.VMEM((GROUP, S_CHUNK, KV_DIM), F32),
                        pltpu.VMEM((S_CHUNK, BLOCK), F32),
                        pltpu.VMEM((GROUP, S_CHUNK, KV_DIM), F32)],
        compiler_params=_cparams(("arbitrary",)),
        name="sample_mixer",
    )(z1, u, cache_k2d, cache_v2d, state2d, cw, sink_row, tab_pad, bcol, hsum, hexp, gsel, gselt)


def _t5_bucket(d):
    max_exact = NUM_BUCKETS // 2
    df = jnp.maximum(d, 1).astype(F32)
    large = max_exact + (jnp.log(df / max_exact) / math.log(MAX_DISTANCE / max_exact)
                         * (NUM_BUCKETS - max_exact)).astype(jnp.int32)
    large = jnp.minimum(large, NUM_BUCKETS - 1)
    return jnp.where(d < max_exact, d, large)


def _band_tables():
    i = np.arange(BLOCK)[:, None]
    j = np.arange(2 * BLOCK)[None, :]
    d = BLOCK + i - j
    band = (d >= 0) & (d <= WINDOW)
    valid = np.stack([band & (j >= BLOCK + META_ROW0), band & (j >= META_ROW0), band])
    bucket = _t5_bucket(jnp.asarray(np.maximum(d, 0), jnp.int32))
    return bucket, jnp.asarray(valid.astype(np.int32))


def _selection_tables():
    d = np.arange(HEAD_DIM)
    selk = np.zeros((KV_DIM, N_KV_HEADS * BLOCK), np.float32)
    selv = np.zeros((KV_DIM, N_KV_HEADS * 2 * 2 * BLOCK), np.float32)
    ones_row = np.zeros((1, N_KV_HEADS * 2 * 2 * BLOCK), np.float32)
    for kh in range(N_KV_HEADS):
        selk[kh * HEAD_DIM + d, kh * BLOCK + d] = 1.0
        selk[kh * HEAD_DIM + d, kh * BLOCK + HEAD_DIM + d] = 1.0
        for half in range(2):
            c0 = (kh * 2 + half) * 2 * BLOCK
            selv[kh * HEAD_DIM + d, c0 + half * HEAD_DIM + d] = 1.0
            ones_row[0, c0 + BLOCK + half * HEAD_DIM + d] = 1.0
    hsum = np.zeros((GROUP, KV_DIM, BLOCK), np.float32)
    gsel = np.zeros((GROUP, Q_DIM, KV_DIM), np.float32)
    for g in range(GROUP):
        for kh in range(N_KV_HEADS):
            hsum[g, kh * HEAD_DIM + d, kh * GROUP + g] = 1.0
            gsel[g, (kh * GROUP + g) * HEAD_DIM + d, kh * HEAD_DIM + d] = 1.0
    hexp = np.transpose(hsum, (0, 2, 1))
    gselt = np.transpose(gsel, (0, 2, 1))
    bf = lambda a: jnp.asarray(a, BF16)
    return bf(selk), bf(selv), jnp.asarray(ones_row), bf(hsum), bf(hexp), bf(gsel), bf(gselt)


def kernel(x_prompt, x_sample, cache_k, cache_v, state_conv, meta_tokens, rel_bias, w_in, conv_w,
           attn_sinks, w_out, norm_pre_mix, norm_post_mix, norm_pre_ffn, norm_post_ffn,
           w_gate, w_up, w_down):
    tail = jnp.concatenate([x_sample.reshape(DEC_BATCH, D_MODEL),
                            jnp.zeros((META_ROW0 - DEC_BATCH, D_MODEL), F32),
                            meta_tokens.astype(F32)], axis=0)
    h = jnp.concatenate([x_prompt.reshape(M_X, D_MODEL), tail], axis=0)

    bucket, valid = _band_tables()
    selk, selv, ones_row, hsum, hexp, gsel, gselt = _selection_tables()
    bias3 = _band_bias(rel_bias, bucket, valid)
    tab_pad = jnp.pad(rel_bias, ((0, 0), (0, BLOCK - N_HEADS)))
    sink_pad = jnp.pad(attn_sinks, ((0, 0), (0, BLOCK - N_HEADS)))
    bcol = _t5_bucket(jnp.asarray(W_BUF - np.arange(W_BUF), jnp.int32)).reshape(W_BUF, 1)

    cache_k2d = cache_k.reshape(DEPTH * DEC_BATCH * W_BUF, KV_DIM)
    cache_v2d = cache_v.reshape(DEPTH * DEC_BATCH * W_BUF, KV_DIM)
    state2d = state_conv.reshape(DEPTH * DEC_BATCH, 2 * CONV_DIM)

    w_in_b = w_in.astype(BF16)
    w_out_b = w_out.astype(BF16)
    w_gate_b = w_gate.astype(BF16)
    w_up_b = w_up.astype(BF16)
    w_down_b = w_down.astype(BF16)

    xn = _norm(h, norm_pre_mix)

    def kv_rows(t):
        return jnp.transpose(t.reshape(BATCH, N_KV_HEADS, HEAD_DIM, WINDOW), (0, 3, 1, 2))

    kp, vp, cp, ks, vs, cs = [], [], [], [], [], []
    for l in range(DEPTH):
        z1 = _matmul(xn, w_in_b, l, Z1_DIM)
        u = _gated(xn, w_in_b, w_in_b, l, Z1_DIM, Z1_DIM + CONV_DIM, CONV_DIM,
                   False, F32, "conv_in")
        mix_s, k_new, v_new, c_new = _sample_mixer(
            l, z1, u, cache_k2d, cache_v2d, state2d, conv_w[l],
            sink_pad[l].reshape(1, BLOCK), tab_pad, bcol, hsum, hexp, gsel, gselt)
        mix, kp_l, vp_l, cp_l = _mixer(attn_sinks[l], z1, u, bias3, selk, selv, ones_row,
                                       conv_w[l], mix_s)
        h, xn = _proj_norm(mix, w_out_b, l, h, norm_post_mix, norm_pre_ffn, l, "out_proj")
        act = _gated(xn, w_gate_b, w_up_b, l, 0, 0, D_FF, True, BF16, "ffn_up")
        h, xn = _proj_norm(act, w_down_b, l, h, norm_post_ffn, norm_pre_mix, (l + 1) % DEPTH,
                           "ffn_down")

        kp.append(kv_rows(kp_l))
        vp.append(kv_rows(vp_l))
        cp.append(cp_l)
        ks.append(k_new.reshape(DEC_BATCH, W_BUF, N_KV_HEADS, HEAD_DIM))
        vs.append(v_new.reshape(DEC_BATCH, W_BUF, N_KV_HEADS, HEAD_DIM))
        cs.append(c_new.reshape(DEC_BATCH, CONV_WIDTH - 1, CONV_DIM))

    y_prompt = h[:M_X].reshape(BATCH, SEQ, D_MODEL)
    y_sample = h[M_X:M_X + DEC_BATCH].reshape(DEC_BATCH, 1, D_MODEL)
    return (y_prompt, y_sample, jnp.stack(kp), jnp.stack(vp), jnp.stack(cp),
            jnp.stack(ks), jnp.stack(vs), jnp.stack(cs))
```

```python
import functools
import math

import numpy as np
import jax
import jax.numpy as jnp
from jax import lax
from jax.experimental import pallas as pl
from jax.experimental.pallas import tpu as pltpu

D_MODEL = 2048
BATCH = 4
SEQ = 2048
DEPTH = 4
DEC_BATCH = 32
HEAD_DIM = 64
N_HEADS = 16
N_KV_HEADS = 4
GROUP = 4
Q_DIM = 1024
KV_DIM = 256
CONV_DIM = 1024
CONV_WIDTH = 3
IN_DIM = 4608
WINDOW = 128
BLOCK = 128
NUM_BUCKETS = 32
MAX_DISTANCE = 128
N_META = 16
D_FF = 5632
RMS_EPS = 1e-6
SCALE = HEAD_DIM ** -0.5
W_BUF = 128

M_X = BATCH * SEQ
N_XBLK = M_X // BLOCK
BLK_PER_SEQ = SEQ // BLOCK
M_ALL = M_X + BLOCK
META_ROW0 = BLOCK - N_META
Z1_DIM = Q_DIM + 2 * KV_DIM + CONV_DIM
NEG = -1e30

TM_BIG = 1664
TM_EPI = 832
TM_PROJ = 416
RC_PROJ = 208
TN = 512
VMEM_LIMIT = 56 * 1024 * 1024

F32 = jnp.float32
BF16 = jnp.bfloat16


def _cparams(sem):
    return pltpu.CompilerParams(dimension_semantics=sem, vmem_limit_bytes=VMEM_LIMIT)


def _rms(x, g):
    return x * lax.rsqrt(jnp.mean(x * x, axis=-1, keepdims=True) + RMS_EPS) * g


def _norm_kernel(h_ref, g_ref, xn_ref):
    xn_ref[...] = _rms(h_ref[...], g_ref[0:1, :]).astype(BF16)


def _norm(h, g):
    return pl.pallas_call(
        _norm_kernel,
        out_shape=jax.ShapeDtypeStruct((M_ALL, D_MODEL), BF16),
        grid=(M_ALL // TM_EPI,),
        in_specs=[pl.BlockSpec((TM_EPI, D_MODEL), lambda i: (i, 0)),
                  pl.BlockSpec((DEPTH, D_MODEL), lambda i: (0, 0))],
        out_specs=pl.BlockSpec((TM_EPI, D_MODEL), lambda i: (i, 0)),
        compiler_params=_cparams(("parallel",)),
        name="norm0",
    )(h, g)


RIDER_ROWS = 128


def _rider(wr, layer, n_outer, n_inner):
    n_chunks = wr.shape[1] // RIDER_ROWS
    assert n_chunks <= n_outer * n_inner
    chunk = lambda j, i: jnp.minimum(j * n_inner + i, n_chunks - 1)
    in_spec = pl.BlockSpec((None, RIDER_ROWS, D_MODEL), lambda j, i: (layer, chunk(j, i), 0))
    out_spec = pl.BlockSpec((RIDER_ROWS, D_MODEL), lambda j, i: (chunk(j, i), 0))
    return in_spec, out_spec, jax.ShapeDtypeStruct((wr.shape[1], D_MODEL), BF16)


def _matmul_kernel(x_ref, w_ref, wr_ref, o_ref, wrb_ref, wbf_ref):
    @pl.when(pl.program_id(1) == 0)
    def _():
        wbf_ref[...] = w_ref[...].astype(BF16)

    o_ref[...] = jnp.dot(x_ref[...], wbf_ref[...], preferred_element_type=F32)
    wrb_ref[...] = wr_ref[...].astype(BF16)


def _matmul(x, w, wr, layer, n_cols):
    k = x.shape[1]
    n_outer, n_inner = n_cols // TN, M_ALL // TM_BIG
    r_in, r_out, r_shape = _rider(wr, layer, n_outer, n_inner)
    return pl.pallas_call(
        _matmul_kernel,
        out_shape=(jax.ShapeDtypeStruct((M_ALL, n_cols), F32), r_shape),
        grid=(n_outer, n_inner),
        in_specs=[pl.BlockSpec((TM_BIG, k), lambda j, i: (i, 0)),
                  pl.BlockSpec((None, k, TN), lambda j, i: (layer, 0, j)),
                  r_in],
        out_specs=(pl.BlockSpec((TM_BIG, TN), lambda j, i: (i, j)), r_out),
        scratch_shapes=[pltpu.VMEM((k, TN), BF16)],
        compiler_params=_cparams(("arbitrary", "arbitrary")),
        name="in_proj",
    )(x, w, wr)


def _gated_kernel(*refs, silu, has_rider):
    if has_rider:
        x_ref, wa_ref, wb_ref, wr_ref, o_ref, wrb_ref, wabf_ref, wbbf_ref = refs
    else:
        x_ref, wa_ref, wb_ref, o_ref, wabf_ref, wbbf_ref = refs

    @pl.when(pl.program_id(1) == 0)
    def _():
        wabf_ref[...] = wa_ref[...].astype(BF16)
        wbbf_ref[...] = wb_ref[...].astype(BF16)

    x = x_ref[...]
    a = jnp.dot(x, wabf_ref[...], preferred_element_type=F32)
    b = jnp.dot(x, wbbf_ref[...], preferred_element_type=F32)
    if silu:
        a = a * (1.0 / (1.0 + jnp.exp(-a)))
    o_ref[...] = (a * b).astype(o_ref.dtype)
    if has_rider:
        wrb_ref[...] = wr_ref[...].astype(BF16)


def _gated(x, wa, wb, wr, layer, a_col0, b_col0, n_cols, silu, out_dtype, name):
    k = x.shape[1]
    a0 = a_col0 // TN
    b0 = b_col0 // TN
    n_outer, n_inner = n_cols // TN, M_ALL // TM_BIG
    in_specs = [pl.BlockSpec((TM_BIG, k), lambda j, i: (i, 0)),
                pl.BlockSpec((None, k, TN), lambda j, i: (layer, 0, a0 + j)),
                pl.BlockSpec((None, k, TN), lambda j, i: (layer, 0, b0 + j))]
    out_specs = [pl.BlockSpec((TM_BIG, TN), lambda j, i: (i, j))]
    out_shape = [jax.ShapeDtypeStruct((M_ALL, n_cols), out_dtype)]
    args = [x, wa, wb]
    if wr is not None:
        r_in, r_out, r_shape = _rider(wr, layer, n_outer, n_inner)
        in_specs.append(r_in)
        out_specs.append(r_out)
        out_shape.append(r_shape)
        args.append(wr)
    return pl.pallas_call(
        functools.partial(_gated_kernel, silu=silu, has_rider=wr is not None),
        out_shape=tuple(out_shape),
        grid=(n_outer, n_inner),
        in_specs=in_specs,
        out_specs=tuple(out_specs),
        scratch_shapes=[pltpu.VMEM((k, TN), BF16), pltpu.VMEM((k, TN), BF16)],
        compiler_params=_cparams(("arbitrary", "arbitrary")),
        name=name,
    )(*args)


def _proj_norm_kernel(a_ref, w_ref, hin_ref, gpost_ref, gnext_ref, h_ref, xn_ref, *, l_post, l_next):
    g_post = gpost_ref[l_post:l_post + 1, :]
    g_next = gnext_ref[l_next:l_next + 1, :]
    for c in range(TM_PROJ // RC_PROJ):
        rows = slice(c * RC_PROJ, (c + 1) * RC_PROJ)
        y = jnp.dot(a_ref[rows, :], w_ref[...], preferred_element_type=F32)
        hn = hin_ref[rows, :] + _rms(y, g_post)
        h_ref[rows, :] = hn
        xn_ref[rows, :] = _rms(hn, g_next).astype(BF16)


def _proj_norm(a, w, layer, hin, g_post, g_next, l_next, name):
    k = a.shape[1]
    return pl.pallas_call(
        functools.partial(_proj_norm_kernel, l_post=layer, l_next=l_next),
        out_shape=(jax.ShapeDtypeStruct((M_ALL, D_MODEL), F32),
                   jax.ShapeDtypeStruct((M_ALL, D_MODEL), BF16)),
        grid=(M_ALL // TM_PROJ,),
        in_specs=[pl.BlockSpec((TM_PROJ, k), lambda i: (i, 0)),
                  pl.BlockSpec((k, D_MODEL), lambda i: (0, 0), pipeline_mode=pl.Buffered(1)),
                  pl.BlockSpec((TM_PROJ, D_MODEL), lambda i: (i, 0)),
                  pl.BlockSpec((DEPTH, D_MODEL), lambda i: (0, 0)),
                  pl.BlockSpec((DEPTH, D_MODEL), lambda i: (0, 0))],
        out_specs=(pl.BlockSpec((TM_PROJ, D_MODEL), lambda i: (i, 0)),
                   pl.BlockSpec((TM_PROJ, D_MODEL), lambda i: (i, 0))),
        compiler_params=_cparams(("parallel",)),
        name=name,
    )(a, w, hin, g_post, g_next)


def _band_bias_kernel(tab_ref, bucket_ref, valid_ref, o_ref):
    h = pl.program_id(1)
    bucket = bucket_ref[...]
    acc = jnp.zeros((BLOCK, 2 * BLOCK), F32)
    for b in range(NUM_BUCKETS):
        acc = jnp.where(bucket == b, tab_ref[b, h], acc)
    o_ref[0, 0] = jnp.where(valid_ref[0] != 0, acc, NEG)


def _band_bias(rel_bias, bucket, valid):
    return pl.pallas_call(
        _band_bias_kernel,
        out_shape=jax.ShapeDtypeStruct((3, N_HEADS, BLOCK, 2 * BLOCK), F32),
        grid=(3, N_HEADS),
        in_specs=[pl.BlockSpec(memory_space=pltpu.SMEM),
                  pl.BlockSpec((BLOCK, 2 * BLOCK), lambda v, h: (0, 0)),
                  pl.BlockSpec((1, BLOCK, 2 * BLOCK), lambda v, h: (v, 0, 0))],
        out_specs=pl.BlockSpec((1, 1, BLOCK, 2 * BLOCK), lambda v, h: (v, h, 0, 0)),
        compiler_params=_cparams(("arbitrary", "arbitrary")),
        name="band_bias",
    )(rel_bias, bucket, valid)


def _mixer_kernel(sink_ref, z_ref, zp_ref, u_ref, up_ref, bias_ref, cw_ref, mixs_ref,
                  mix_ref, kp_ref, vp_ref, cp_ref, ext_ref):
    i = pl.program_id(0)
    is_tail = i == N_XBLK

    @pl.when(jnp.logical_and(i < N_XBLK, i % BLK_PER_SEQ == BLK_PER_SEQ - 1))
    def _():
        kp_ref[0] = z_ref[:, Q_DIM:Q_DIM + KV_DIM].T
        vp_ref[0] = z_ref[:, Q_DIM + KV_DIM:Q_DIM + 2 * KV_DIM].T
        cp_ref[0] = u_ref[BLOCK - (CONV_WIDTH - 1):BLOCK, :]

    q = (z_ref[:, 0:Q_DIM] * SCALE).astype(BF16)
    kk = jnp.concatenate([zp_ref[:, 0:KV_DIM], z_ref[:, Q_DIM:Q_DIM + KV_DIM]], axis=0)
    vv = jnp.concatenate([zp_ref[:, KV_DIM:2 * KV_DIM],
                          z_ref[:, Q_DIM + KV_DIM:Q_DIM + 2 * KV_DIM]], axis=0)

    lo_kv = lax.broadcasted_iota(jnp.int32, (2 * BLOCK, BLOCK), 1) < HEAD_DIM
    one_lo = jnp.where(lo_kv, 1.0, 0.0).astype(BF16)
    one_hi = jnp.where(lo_kv, 0.0, 1.0).astype(BF16)
    k2, rv = {}, {}
    for col in range(KV_DIM // BLOCK):
        kc = kk[:, col * BLOCK:(col + 1) * BLOCK]
        vc = vv[:, col * BLOCK:(col + 1) * BLOCK]
        ks = pltpu.roll(kc, HEAD_DIM, 1)
        vs = pltpu.roll(vc, HEAD_DIM, 1)
        for in_hi in range(2):
            kh = 2 * col + in_hi
            own_k, oth_k = (ks, kc) if in_hi else (kc, ks)
            own_v, oth_v = (vs, vc) if in_hi else (vc, vs)
            k2[kh] = jnp.where(lo_kv, own_k, oth_k).astype(BF16)
            rv[kh, 0] = jnp.concatenate([jnp.where(lo_kv, own_v, 0.0).astype(BF16), one_lo], axis=1)
            rv[kh, 1] = jnp.concatenate([jnp.where(lo_kv, 0.0, oth_v).astype(BF16), one_hi], axis=1)

    lane = lax.broadcasted_iota(jnp.int32, (BLOCK, BLOCK), 1)
    lo_half = lane < HEAD_DIM
    zero = jnp.zeros((BLOCK, BLOCK), BF16)
    for p in range(N_HEADS // 2):
        kh = p // 2
        qp = q[:, p * BLOCK:(p + 1) * BLOCK]
        acc = None
        sink_terms = []
        for half in range(2):
            h = 2 * p + half
            qm = jnp.where(lo_half if half == 0 else jnp.logical_not(lo_half), qp, zero)
            s = lax.dot_general(qm, k2[kh], (((1,), (1,)), ((), ())),
                                preferred_element_type=F32) + bias_ref[0, h]
            sk = sink_ref[h]
            m = jnp.maximum(jnp.max(s, axis=-1, keepdims=True), sk)
            e = jnp.exp(s - m).astype(BF16)
            d = jnp.dot(e, rv[kh, half], preferred_element_type=F32)
            acc = d if acc is None else acc + d
            sink_terms.append(jnp.exp(sk - m))
        den = acc[:, BLOCK:] + jnp.where(lo_half, sink_terms[0], sink_terms[1])
        mix_ref[:, p * BLOCK:(p + 1) * BLOCK] = (acc[:, :BLOCK] / den).astype(BF16)

    ext_ref[0:8, :] = up_ref[...]
    ext_ref[8:8 + BLOCK, :] = u_ref[...]
    row = lax.broadcasted_iota(jnp.int32, (BLOCK, CONV_DIM), 0)
    first = jnp.where(is_tail, META_ROW0, -8)
    u1 = jnp.where(row >= first + 1, ext_ref[7:7 + BLOCK, :], 0.0)
    u2 = jnp.where(row >= first + 2, ext_ref[6:6 + BLOCK, :], 0.0)
    gb = z_ref[:, Q_DIM + 2 * KV_DIM:Z1_DIM]
    c = gb * (cw_ref[0:1, :] * u2 + cw_ref[1:2, :] * u1 + cw_ref[2:3, :] * u_ref[...])
    mix_ref[:, Q_DIM:] = c.astype(BF16)

    @pl.when(is_tail)
    def _():
        mix_ref[0:DEC_BATCH, :] = mixs_ref[...].astype(BF16)


def _prev_blk(i):
    return jnp.where(i % BLK_PER_SEQ == 0, N_XBLK, i - 1)


def _bias_variant(i):
    return jnp.where(i == N_XBLK, 0, jnp.where(i % BLK_PER_SEQ == 0, 1, 2))


def _mixer(sinks, z1, u, bias3, cw, mix_s):
    rows8 = BLOCK // 8
    seq_of = lambda i: jnp.minimum(i // BLK_PER_SEQ, BATCH - 1)
    return pl.pallas_call(
        _mixer_kernel,
        out_shape=(jax.ShapeDtypeStruct((M_ALL, D_MODEL), BF16),
                   jax.ShapeDtypeStruct((BATCH, KV_DIM, WINDOW), F32),
                   jax.ShapeDtypeStruct((BATCH, KV_DIM, WINDOW), F32),
                   jax.ShapeDtypeStruct((BATCH, CONV_WIDTH - 1, CONV_DIM), F32)),
        grid=(N_XBLK + 1,),
        in_specs=[pl.BlockSpec(memory_space=pltpu.SMEM),
                  pl.BlockSpec((BLOCK, Z1_DIM), lambda i: (i, 0)),
                  pl.BlockSpec((BLOCK, 2 * KV_DIM), lambda i: (_prev_blk(i), Q_DIM // (2 * KV_DIM))),
                  pl.BlockSpec((BLOCK, CONV_DIM), lambda i: (i, 0)),
                  pl.BlockSpec((8, CONV_DIM), lambda i: (_prev_blk(i) * rows8 + rows8 - 1, 0)),
                  pl.BlockSpec((1, N_HEADS, BLOCK, 2 * BLOCK), lambda i: (_bias_variant(i), 0, 0, 0)),
                  pl.BlockSpec((CONV_WIDTH, CONV_DIM), lambda i: (0, 0)),
                  pl.BlockSpec((DEC_BATCH, D_MODEL), lambda i: (0, 0))],
        out_specs=(pl.BlockSpec((BLOCK, D_MODEL), lambda i: (i, 0)),
                   pl.BlockSpec((1, KV_DIM, WINDOW), lambda i: (seq_of(i), 0, 0)),
                   pl.BlockSpec((1, KV_DIM, WINDOW), lambda i: (seq_of(i), 0, 0)),
                   pl.BlockSpec((1, CONV_WIDTH - 1, CONV_DIM), lambda i: (seq_of(i), 0, 0))),
        scratch_shapes=[pltpu.VMEM((BLOCK + 8, CONV_DIM), F32)],
        compiler_params=_cparams(("arbitrary",)),
        name="mixer",
    )(sinks, z1, z1, u, u, bias3, cw, mix_s)


S_CHUNK = 16


def _sample_kernel(zs_ref, us_ref, kc_ref, vc_ref, st_ref, cw_ref, sink_ref, tab_ref, bcol_ref,
                   hsum_ref, hexp_ref, gsel_ref, gselt_ref,
                   mixs_ref, kout_ref, vout_ref, cs_ref,
                   qg_scr, sn_scr, og_scr):
    z = zs_ref[...]
    q = (z[:, 0:Q_DIM] * SCALE).astype(BF16)
    knew = z[:, Q_DIM:Q_DIM + KV_DIM]
    vnew = z[:, Q_DIM + KV_DIM:Q_DIM + 2 * KV_DIM]
    gb = z[:, Q_DIM + 2 * KV_DIM:Z1_DIM]

    sn = jnp.zeros((S_CHUNK, BLOCK), F32)
    for g in range(GROUP):
        qg = jnp.dot(q, gsel_ref[g], preferred_element_type=F32)
        qg_scr[g] = qg
        sn = sn + jnp.dot((knew * qg).astype(BF16), hsum_ref[g], preferred_element_type=F32)
    sn_scr[...] = sn + tab_ref[0:1, :]

    bcol = bcol_ref[...]
    sbias = jnp.zeros((W_BUF, BLOCK), F32)
    for b in range(NUM_BUCKETS):
        sbias = jnp.where(bcol == b, tab_ref[b:b + 1, :], sbias)
    sink = sink_ref[...]
    row = lax.broadcasted_iota(jnp.int32, (W_BUF, KV_DIM), 0)

    def body(s, carry):
        r0 = pl.multiple_of(s * W_BUF, W_BUF)
        ks = kc_ref[pl.ds(r0, W_BUF), :]
        vs = vc_ref[pl.ds(r0, W_BUF), :]
        sc = sbias
        for g in range(GROUP):
            sc = sc + jnp.dot((ks * qg_scr[g, pl.ds(s, 1), :]).astype(BF16), hsum_ref[g],
                              preferred_element_type=F32)
        sn_s = sn_scr[pl.ds(s, 1), :]
        m = jnp.maximum(jnp.maximum(jnp.max(sc, axis=0, keepdims=True), sn_s), sink)
        e = jnp.exp(sc - m)
        en = jnp.exp(sn_s - m)
        den = jnp.sum(e, axis=0, keepdims=True) + en + jnp.exp(sink - m)
        p = (e / den).astype(BF16)
        pn = jnp.broadcast_to(en / den, (8, BLOCK)).astype(BF16)
        k_s = zs_ref[pl.ds(s, 1), Q_DIM:Q_DIM + KV_DIM]
        v_s = zs_ref[pl.ds(s, 1), Q_DIM + KV_DIM:Q_DIM + 2 * KV_DIM]
        for g in range(GROUP):
            pe = jnp.dot(p, hexp_ref[g], preferred_element_type=F32)
            pne = jnp.dot(pn, hexp_ref[g], preferred_element_type=F32)
            og_scr[g, pl.ds(s, 1), :] = (jnp.sum(pe * vs, axis=0, keepdims=True)
                                         + pne[0:1, :] * v_s)
        kout_ref[pl.ds(r0, W_BUF), :] = jnp.where(row == W_BUF - 1, k_s,
                                                 pltpu.roll(ks, W_BUF - 1, 0))
        vout_ref[pl.ds(r0, W_BUF), :] = jnp.where(row == W_BUF - 1, v_s,
                                                 pltpu.roll(vs, W_BUF - 1, 0))
        return carry

    lax.fori_loop(0, S_CHUNK, body, 0)

    a = jnp.zeros((S_CHUNK, Q_DIM), F32)
    for g in range(GROUP):
        a = a + jnp.dot(og_scr[g].astype(BF16), gselt_ref[g], preferred_element_type=F32)
    mixs_ref[:, 0:Q_DIM] = a

    u = us_ref[...]
    s0 = st_ref[:, 0:CONV_DIM]
    s1 = st_ref[:, CONV_DIM:]
    mixs_ref[:, Q_DIM:] = gb * (cw_ref[0:1, :] * s0 + cw_ref[1:2, :] * s1 + cw_ref[2:3, :] * u)
    cs_ref[:, 0:CONV_DIM] = s1
    cs_ref[:, CONV_DIM:] = u


def _sample_mixer(layer, z1, u, cache_k2d, cache_v2d, state2d, cw, sink_row, tab_pad, bcol,
                  hsum, hexp, gsel, gselt):
    n_chunks = DEC_BATCH // S_CHUNK
    row_blk0 = M_X // S_CHUNK
    full = lambda shape: pl.BlockSpec(shape, lambda c: (0,) * len(shape))
    return pl.pallas_call(
        _sample_kernel,
        out_shape=(jax.ShapeDtypeStruct((DEC_BATCH, D_MODEL), F32),
                   jax.ShapeDtypeStruct((DEC_BATCH * W_BUF, KV_DIM), F32),
                   jax.ShapeDtypeStruct((DEC_BATCH * W_BUF, KV_DIM), F32),
                   jax.ShapeDtypeStruct((DEC_BATCH, 2 * CONV_DIM), F32)),
        grid=(n_chunks,),
        in_specs=[pl.BlockSpec((S_CHUNK, Z1_DIM), lambda c: (row_blk0 + c, 0)),
                  pl.BlockSpec((S_CHUNK, CONV_DIM), lambda c: (row_blk0 + c, 0)),
                  pl.BlockSpec((S_CHUNK * W_BUF, KV_DIM), lambda c: (layer * n_chunks + c, 0)),
                  pl.BlockSpec((S_CHUNK * W_BUF, KV_DIM), lambda c: (layer * n_chunks + c, 0)),
                  pl.BlockSpec((S_CHUNK, 2 * CONV_DIM), lambda c: (layer * n_chunks + c, 0)),
                  full((CONV_WIDTH, CONV_DIM)),
                  full((1, BLOCK)),
                  full((NUM_BUCKETS, BLOCK)),
                  full((W_BUF, 1)),
                  full((GROUP, KV_DIM, BLOCK)),
                  full((GROUP, BLOCK, KV_DIM)),
                  full((GROUP, Q_DIM, KV_DIM)),
                  full((GROUP, KV_DIM, Q_DIM))],
        out_specs=(pl.BlockSpec((S_CHUNK, D_MODEL), lambda c: (c, 0)),
                   pl.BlockSpec((S_CHUNK * W_BUF, KV_DIM), lambda c: (c, 0)),
                   pl.BlockSpec((S_CHUNK * W_BUF, KV_DIM), lambda c: (c, 0)),
                   pl.BlockSpec((S_CHUNK, 2 * CONV_DIM), lambda c: (c, 0))),
        scratch_shapes=[pltpu.VMEM((GROUP, S_CHUNK, KV_DIM), F32),
                        pltpu.VMEM((S_CHUNK, BLOCK), F32),
                        pltpu.VMEM((GROUP, S_CHUNK, KV_DIM), F32)],
        compiler_params=_cparams(("arbitrary",)),
        name="sample_mixer",
    )(z1, u, cache_k2d, cache_v2d, state2d, cw, sink_row, tab_pad, bcol, hsum, hexp, gsel, gselt)


def _t5_bucket(d):
    max_exact = NUM_BUCKETS // 2
    df = jnp.maximum(d, 1).astype(F32)
    large = max_exact + (jnp.log(df / max_exact) / math.log(MAX_DISTANCE / max_exact)
                         * (NUM_BUCKETS - max_exact)).astype(jnp.int32)
    large = jnp.minimum(large, NUM_BUCKETS - 1)
    return jnp.where(d < max_exact, d, large)


def _band_tables():
    i = np.arange(BLOCK)[:, None]
    j = np.arange(2 * BLOCK)[None, :]
    d = BLOCK + i - j
    band = (d >= 0) & (d <= WINDOW)
    valid = np.stack([band & (j >= BLOCK + META_ROW0), band & (j >= META_ROW0), band])
    bucket = _t5_bucket(jnp.asarray(np.maximum(d, 0), jnp.int32))
    return bucket, jnp.asarray(valid.astype(np.int32))


def _selection_tables():
    d = np.arange(HEAD_DIM)
    hsum = np.zeros((GROUP, KV_DIM, BLOCK), np.float32)
    gsel = np.zeros((GROUP, Q_DIM, KV_DIM), np.float32)
    for g in range(GROUP):
        for kh in range(N_KV_HEADS):
            hsum[g, kh * HEAD_DIM + d, kh * GROUP + g] = 1.0
            gsel[g, (kh * GROUP + g) * HEAD_DIM + d, kh * HEAD_DIM + d] = 1.0
    hexp = np.transpose(hsum, (0, 2, 1))
    gselt = np.transpose(gsel, (0, 2, 1))
    bf = lambda a: jnp.asarray(a, BF16)
    return bf(hsum), bf(hexp), bf(gsel), bf(gselt)


def kernel(x_prompt, x_sample, cache_k, cache_v, state_conv, meta_tokens, rel_bias, w_in, conv_w,
           attn_sinks, w_out, norm_pre_mix, norm_post_mix, norm_pre_ffn, norm_post_ffn,
           w_gate, w_up, w_down):
    tail = jnp.concatenate([x_sample.reshape(DEC_BATCH, D_MODEL),
                            jnp.zeros((META_ROW0 - DEC_BATCH, D_MODEL), F32),
                            meta_tokens.astype(F32)], axis=0)
    h = jnp.concatenate([x_prompt.reshape(M_X, D_MODEL), tail], axis=0)

    bucket, valid = _band_tables()
    hsum, hexp, gsel, gselt = _selection_tables()
    bias3 = _band_bias(rel_bias, bucket, valid)
    tab_pad = jnp.pad(rel_bias, ((0, 0), (0, BLOCK - N_HEADS)))
    sink_pad = jnp.pad(attn_sinks, ((0, 0), (0, BLOCK - N_HEADS)))
    bcol = _t5_bucket(jnp.asarray(W_BUF - np.arange(W_BUF), jnp.int32)).reshape(W_BUF, 1)

    cache_k2d = cache_k.reshape(DEPTH * DEC_BATCH * W_BUF, KV_DIM)
    cache_v2d = cache_v.reshape(DEPTH * DEC_BATCH * W_BUF, KV_DIM)
    state2d = state_conv.reshape(DEPTH * DEC_BATCH, 2 * CONV_DIM)

    xn = _norm(h, norm_pre_mix)

    def kv_rows(t):
        return jnp.transpose(t.reshape(BATCH, N_KV_HEADS, HEAD_DIM, WINDOW), (0, 3, 1, 2))

    kp, vp, cp, ks, vs, cs = [], [], [], [], [], []
    for l in range(DEPTH):
        z1, w_out_b = _matmul(xn, w_in, w_out, l, Z1_DIM)
        (u,) = _gated(xn, w_in, w_in, None, l, Z1_DIM, Z1_DIM + CONV_DIM, CONV_DIM,
                      False, F32, "conv_in")
        mix_s, k_new, v_new, c_new = _sample_mixer(
            l, z1, u, cache_k2d, cache_v2d, state2d, conv_w[l],
            sink_pad[l].reshape(1, BLOCK), tab_pad, bcol, hsum, hexp, gsel, gselt)
        mix, kp_l, vp_l, cp_l = _mixer(attn_sinks[l], z1, u, bias3, conv_w[l], mix_s)
        h, xn = _proj_norm(mix, w_out_b, l, h, norm_post_mix, norm_pre_ffn, l, "out_proj")
        act, w_down_b = _gated(xn, w_gate, w_up, w_down, l, 0, 0, D_FF, True, BF16, "ffn_up")
        h, xn = _proj_norm(act, w_down_b, l, h, norm_post_ffn, norm_pre_mix, (l + 1) % DEPTH,
                           "ffn_down")

        kp.append(kv_rows(kp_l))
        vp.append(kv_rows(vp_l))
        cp.append(cp_l)
        ks.append(k_new.reshape(DEC_BATCH, W_BUF, N_KV_HEADS, HEAD_DIM))
        vs.append(v_new.reshape(DEC_BATCH, W_BUF, N_KV_HEADS, HEAD_DIM))
        cs.append(c_new.reshape(DEC_BATCH, CONV_WIDTH - 1, CONV_DIM))

    y_prompt = h[:M_X].reshape(BATCH, SEQ, D_MODEL)
    y_sample = h[M_X:M_X + DEC_BATCH].reshape(DEC_BATCH, 1, D_MODEL)
    return (y_prompt, y_sample, jnp.stack(kp), jnp.stack(vp), jnp.stack(cp),
            jnp.stack(ks), jnp.stack(vs), jnp.stack(cs))
```

```python
import functools
import math

import numpy as np
import jax
import jax.numpy as jnp
from jax import lax
from jax.experimental import pallas as pl
from jax.experimental.pallas import tpu as pltpu

D_MODEL = 2048
BATCH = 4
SEQ = 2048
DEPTH = 4
DEC_BATCH = 32
HEAD_DIM = 64
N_HEADS = 16
N_KV_HEADS = 4
GROUP = 4
Q_DIM = 1024
KV_DIM = 256
CONV_DIM = 1024
CONV_WIDTH = 3
IN_DIM = 4608
WINDOW = 128
BLOCK = 128
NUM_BUCKETS = 32
MAX_DISTANCE = 128
N_META = 16
D_FF = 5632
RMS_EPS = 1e-6
SCALE = HEAD_DIM ** -0.5
W_BUF = 128

M_X = BATCH * SEQ
N_XBLK = M_X // BLOCK
BLK_PER_SEQ = SEQ // BLOCK
M_ALL = M_X + BLOCK
META_ROW0 = BLOCK - N_META
Z1_DIM = Q_DIM + 2 * KV_DIM + CONV_DIM
NEG = -1e30

TM_BIG = 1664
TM_EPI = 640
N_EPI = M_ALL // TM_EPI
EPI_LAST_X = M_X - (N_EPI - 1) * TM_EPI
assert (M_X - EPI_LAST_X) % EPI_LAST_X == 0 and EPI_LAST_X + BLOCK == TM_EPI
TM_PROJ = 416
RC_PROJ = 208
N_PROJ = M_ALL // TM_PROJ
PROJ_LAST_X = M_X - (N_PROJ - 1) * TM_PROJ
assert PROJ_LAST_X + BLOCK == TM_PROJ and PROJ_LAST_X % 8 == 0
TN = 512
VMEM_LIMIT = 56 * 1024 * 1024

F32 = jnp.float32
BF16 = jnp.bfloat16


def _cparams(sem):
    return pltpu.CompilerParams(dimension_semantics=sem, vmem_limit_bytes=VMEM_LIMIT)


def _rms(x, g):
    return x * lax.rsqrt(jnp.mean(x * x, axis=-1, keepdims=True) + RMS_EPS) * g


def _norm_kernel(xa_ref, xb_ref, tail_ref, g_ref, h_ref, xn_ref):
    last = pl.program_id(0) == N_EPI - 1
    h_last = jnp.concatenate([xb_ref[...], tail_ref[...]], axis=0)
    h = jnp.where(last, h_last, xa_ref[...])
    h_ref[...] = h
    xn_ref[...] = _rms(h, g_ref[0:1, :]).astype(BF16)


def _norm(x2d, tail, g):
    return pl.pallas_call(
        _norm_kernel,
        out_shape=(jax.ShapeDtypeStruct((M_ALL, D_MODEL), F32),
                   jax.ShapeDtypeStruct((M_ALL, D_MODEL), BF16)),
        grid=(N_EPI,),
        in_specs=[pl.BlockSpec((TM_EPI, D_MODEL), lambda i: (jnp.minimum(i, N_EPI - 2), 0)),
                  pl.BlockSpec((EPI_LAST_X, D_MODEL), lambda i: ((M_X - EPI_LAST_X) // EPI_LAST_X, 0)),
                  pl.BlockSpec((BLOCK, D_MODEL), lambda i: (0, 0)),
                  pl.BlockSpec((DEPTH, D_MODEL), lambda i: (0, 0))],
        out_specs=(pl.BlockSpec((TM_EPI, D_MODEL), lambda i: (i, 0)),
                   pl.BlockSpec((TM_EPI, D_MODEL), lambda i: (i, 0))),
        compiler_params=_cparams(("arbitrary",)),
        name="norm0",
    )(x2d, x2d, tail, g)


RIDER_ROWS = 128


def _rider(wr, layer, n_outer, n_inner):
    n_chunks = wr.shape[1] // RIDER_ROWS
    assert n_chunks <= n_outer * n_inner
    chunk = lambda j, i: jnp.minimum(j * n_inner + i, n_chunks - 1)
    in_spec = pl.BlockSpec((None, RIDER_ROWS, D_MODEL), lambda j, i: (layer, chunk(j, i), 0))
    out_spec = pl.BlockSpec((RIDER_ROWS, D_MODEL), lambda j, i: (chunk(j, i), 0))
    return in_spec, out_spec, jax.ShapeDtypeStruct((wr.shape[1], D_MODEL), BF16)


def _matmul_kernel(x_ref, w_ref, wr_ref, o_ref, wrb_ref, wbf_ref):
    @pl.when(pl.program_id(1) == 0)
    def _():
        wbf_ref[...] = w_ref[...].astype(BF16)

    o_ref[...] = jnp.dot(x_ref[...], wbf_ref[...], preferred_element_type=F32)
    wrb_ref[...] = wr_ref[...].astype(BF16)


def _matmul(x, w, wr, layer, n_cols):
    k = x.shape[1]
    n_outer, n_inner = n_cols // TN, M_ALL // TM_BIG
    r_in, r_out, r_shape = _rider(wr, layer, n_outer, n_inner)
    return pl.pallas_call(
        _matmul_kernel,
        out_shape=(jax.ShapeDtypeStruct((M_ALL, n_cols), F32), r_shape),
        grid=(n_outer, n_inner),
        in_specs=[pl.BlockSpec((TM_BIG, k), lambda j, i: (i, 0)),
                  pl.BlockSpec((None, k, TN), lambda j, i: (layer, 0, j)),
                  r_in],
        out_specs=(pl.BlockSpec((TM_BIG, TN), lambda j, i: (i, j)), r_out),
        scratch_shapes=[pltpu.VMEM((k, TN), BF16)],
        compiler_params=_cparams(("arbitrary", "arbitrary")),
        name="in_proj",
    )(x, w, wr)


def _gated_kernel(*refs, silu, has_rider):
    if has_rider:
        x_ref, wa_ref, wb_ref, wr_ref, o_ref, wrb_ref, wabf_ref, wbbf_ref = refs
    else:
        x_ref, wa_ref, wb_ref, o_ref, wabf_ref, wbbf_ref = refs

    @pl.when(pl.program_id(1) == 0)
    def _():
        wabf_ref[...] = wa_ref[...].astype(BF16)
        wbbf_ref[...] = wb_ref[...].astype(BF16)

    x = x_ref[...]
    a = jnp.dot(x, wabf_ref[...], preferred_element_type=F32)
    b = jnp.dot(x, wbbf_ref[...], preferred_element_type=F32)
    if silu:
        a = a * (1.0 / (1.0 + jnp.exp(-a)))
    o_ref[...] = (a * b).astype(o_ref.dtype)
    if has_rider:
        wrb_ref[...] = wr_ref[...].astype(BF16)


def _gated(x, wa, wb, wr, layer, a_col0, b_col0, n_cols, silu, out_dtype, name):
    k = x.shape[1]
    a0 = a_col0 // TN
    b0 = b_col0 // TN
    n_outer, n_inner = n_cols // TN, M_ALL // TM_BIG
    in_specs = [pl.BlockSpec((TM_BIG, k), lambda j, i: (i, 0)),
                pl.BlockSpec((None, k, TN), lambda j, i: (layer, 0, a0 + j)),
                pl.BlockSpec((None, k, TN), lambda j, i: (layer, 0, b0 + j))]
    out_specs = [pl.BlockSpec((TM_BIG, TN), lambda j, i: (i, j))]
    out_shape = [jax.ShapeDtypeStruct((M_ALL, n_cols), out_dtype)]
    args = [x, wa, wb]
    if wr is not None:
        r_in, r_out, r_shape = _rider(wr, layer, n_outer, n_inner)
        in_specs.append(r_in)
        out_specs.append(r_out)
        out_shape.append(r_shape)
        args.append(wr)
    return pl.pallas_call(
        functools.partial(_gated_kernel, silu=silu, has_rider=wr is not None),
        out_shape=tuple(out_shape),
        grid=(n_outer, n_inner),
        in_specs=in_specs,
        out_specs=tuple(out_specs),
        scratch_shapes=[pltpu.VMEM((k, TN), BF16), pltpu.VMEM((k, TN), BF16)],
        compiler_params=_cparams(("arbitrary", "arbitrary")),
        name=name,
    )(*args)


def _proj_norm_kernel(a_ref, w_ref, hin_ref, gpost_ref, gnext_ref, h_ref, xn_ref, *, l_post, l_next):
    g_post = gpost_ref[l_post:l_post + 1, :]
    g_next = gnext_ref[l_next:l_next + 1, :]
    for c in range(TM_PROJ // RC_PROJ):
        rows = slice(c * RC_PROJ, (c + 1) * RC_PROJ)
        y = jnp.dot(a_ref[rows, :], w_ref[...], preferred_element_type=F32)
        hn = hin_ref[rows, :] + _rms(y, g_post)
        h_ref[rows, :] = hn
        xn_ref[rows, :] = _rms(hn, g_next).astype(BF16)


def _proj_norm(a, w, layer, hin, g_post, g_next, l_next, name):
    k = a.shape[1]
    return pl.pallas_call(
        functools.partial(_proj_norm_kernel, l_post=layer, l_next=l_next),
        out_shape=(jax.ShapeDtypeStruct((M_ALL, D_MODEL), F32),
                   jax.ShapeDtypeStruct((M_ALL, D_MODEL), BF16)),
        grid=(M_ALL // TM_PROJ,),
        in_specs=[pl.BlockSpec((TM_PROJ, k), lambda i: (i, 0)),
                  pl.BlockSpec((k, D_MODEL), lambda i: (0, 0), pipeline_mode=pl.Buffered(1)),
                  pl.BlockSpec((TM_PROJ, D_MODEL), lambda i: (i, 0)),
                  pl.BlockSpec((DEPTH, D_MODEL), lambda i: (0, 0)),
                  pl.BlockSpec((DEPTH, D_MODEL), lambda i: (0, 0))],
        out_specs=(pl.BlockSpec((TM_PROJ, D_MODEL), lambda i: (i, 0)),
                   pl.BlockSpec((TM_PROJ, D_MODEL), lambda i: (i, 0))),
        compiler_params=_cparams(("parallel",)),
        name=name,
    )(a, w, hin, g_post, g_next)


def _proj_last_kernel(a_ref, w_ref, hin_ref, gpost_ref, y_ref, tail_ref, *, l_post):
    g_post = gpost_ref[l_post:l_post + 1, :]
    for c in range(TM_PROJ // RC_PROJ):
        rows = slice(c * RC_PROJ, (c + 1) * RC_PROJ)
        y = jnp.dot(a_ref[rows, :], w_ref[...], preferred_element_type=F32)
        y_ref[rows, :] = hin_ref[rows, :] + _rms(y, g_post)

    @pl.when(pl.program_id(0) == N_PROJ - 1)
    def _():
        tail_ref[...] = y_ref[PROJ_LAST_X:TM_PROJ, :]


def _proj_last(a, w, layer, hin, g_post):
    k = a.shape[1]
    return pl.pallas_call(
        functools.partial(_proj_last_kernel, l_post=layer),
        out_shape=(jax.ShapeDtypeStruct((M_X, D_MODEL), F32),
                   jax.ShapeDtypeStruct((BLOCK, D_MODEL), F32)),
        grid=(N_PROJ,),
        in_specs=[pl.BlockSpec((TM_PROJ, k), lambda i: (i, 0)),
                  pl.BlockSpec((k, D_MODEL), lambda i: (0, 0), pipeline_mode=pl.Buffered(1)),
                  pl.BlockSpec((TM_PROJ, D_MODEL), lambda i: (i, 0)),
                  pl.BlockSpec((DEPTH, D_MODEL), lambda i: (0, 0))],
        out_specs=(pl.BlockSpec((TM_PROJ, D_MODEL), lambda i: (i, 0)),
                   pl.BlockSpec((BLOCK, D_MODEL), lambda i: (0, 0))),
        compiler_params=_cparams(("arbitrary",)),
        name="ffn_down_last",
    )(a, w, hin, g_post)


def _band_bias_kernel(tab_ref, bucket_ref, valid_ref, o_ref):
    h = pl.program_id(0)
    bucket = bucket_ref[...]
    acc = jnp.zeros((BLOCK, 2 * BLOCK), F32)
    for b in range(NUM_BUCKETS):
        acc = jnp.where(bucket == b, tab_ref[b, h], acc)
    for v in range(3):
        o_ref[v, 0] = jnp.where(valid_ref[v] != 0, acc, NEG)


def _band_bias(rel_bias, bucket, valid):
    return pl.pallas_call(
        _band_bias_kernel,
        out_shape=jax.ShapeDtypeStruct((3, N_HEADS, BLOCK, 2 * BLOCK), F32),
        grid=(N_HEADS,),
        in_specs=[pl.BlockSpec(memory_space=pltpu.SMEM),
                  pl.BlockSpec((BLOCK, 2 * BLOCK), lambda h: (0, 0)),
                  pl.BlockSpec((3, BLOCK, 2 * BLOCK), lambda h: (0, 0, 0))],
        out_specs=pl.BlockSpec((3, 1, BLOCK, 2 * BLOCK), lambda h: (0, h, 0, 0)),
        compiler_params=_cparams(("arbitrary",)),
        name="band_bias",
    )(rel_bias, bucket, valid)


def _mixer_kernel(sink_ref, z_ref, zp_ref, u_ref, up_ref, bias_ref, cw_ref, mixs_ref,
                  mix_ref, kp_ref, vp_ref, cp_ref, ext_ref):
    i = pl.program_id(0)
    is_tail = i == N_XBLK

    @pl.when(jnp.logical_and(i < N_XBLK, i % BLK_PER_SEQ == BLK_PER_SEQ - 1))
    def _():
        kp_ref[0] = z_ref[:, Q_DIM:Q_DIM + KV_DIM].T
        vp_ref[0] = z_ref[:, Q_DIM + KV_DIM:Q_DIM + 2 * KV_DIM].T
        cp_ref[0] = u_ref[BLOCK - (CONV_WIDTH - 1):BLOCK, :]

    q = (z_ref[:, 0:Q_DIM] * SCALE).astype(BF16)
    kk = jnp.concatenate([zp_ref[:, 0:KV_DIM], z_ref[:, Q_DIM:Q_DIM + KV_DIM]], axis=0)
    vv = jnp.concatenate([zp_ref[:, KV_DIM:2 * KV_DIM],
                          z_ref[:, Q_DIM + KV_DIM:Q_DIM + 2 * KV_DIM]], axis=0)

    lo_kv = lax.broadcasted_iota(jnp.int32, (2 * BLOCK, BLOCK), 1) < HEAD_DIM
    one_lo = jnp.where(lo_kv, 1.0, 0.0).astype(BF16)
    one_hi = jnp.where(lo_kv, 0.0, 1.0).astype(BF16)
    k2, rv = {}, {}
    for col in range(KV_DIM // BLOCK):
        kc = kk[:, col * BLOCK:(col + 1) * BLOCK]
        vc = vv[:, col * BLOCK:(col + 1) * BLOCK]
        ks = pltpu.roll(kc, HEAD_DIM, 1)
        vs = pltpu.roll(vc, HEAD_DIM, 1)
        for in_hi in range(2):
            kh = 2 * col + in_hi
            own_k, oth_k = (ks, kc) if in_hi else (kc, ks)
            own_v, oth_v = (vs, vc) if in_hi else (vc, vs)
            k2[kh] = jnp.where(lo_kv, own_k, oth_k).astype(BF16)
            rv[kh, 0] = jnp.concatenate([jnp.where(lo_kv, own_v, 0.0).astype(BF16), one_lo], axis=1)
            rv[kh, 1] = jnp.concatenate([jnp.where(lo_kv, 0.0, oth_v).astype(BF16), one_hi], axis=1)

    lane = lax.broadcasted_iota(jnp.int32, (BLOCK, BLOCK), 1)
    lo_half = lane < HEAD_DIM
    zero = jnp.zeros((BLOCK, BLOCK), BF16)
    for p in range(N_HEADS // 2):
        kh = p // 2
        qp = q[:, p * BLOCK:(p + 1) * BLOCK]
        acc = None
        sink_terms = []
        for half in range(2):
            h = 2 * p + half
            qm = jnp.where(lo_half if half == 0 else jnp.logical_not(lo_half), qp, zero)
            s = lax.dot_general(qm, k2[kh], (((1,), (1,)), ((), ())),
                                preferred_element_type=F32) + bias_ref[0, h]
            sk = sink_ref[h]
            m = jnp.maximum(jnp.max(s, axis=-1, keepdims=True), sk)
            e = jnp.exp(s - m).astype(BF16)
            d = jnp.dot(e, rv[kh, half], preferred_element_type=F32)
            acc = d if acc is None else acc + d
            sink_terms.append(jnp.exp(sk - m))
        den = acc[:, BLOCK:] + jnp.where(lo_half, sink_terms[0], sink_terms[1])
        mix_ref[:, p * BLOCK:(p + 1) * BLOCK] = (acc[:, :BLOCK] / den).astype(BF16)

    ext_ref[0:8, :] = up_ref[...]
    ext_ref[8:8 + BLOCK, :] = u_ref[...]
    row = lax.broadcasted_iota(jnp.int32, (BLOCK, CONV_DIM), 0)
    first = jnp.where(is_tail, META_ROW0, -8)
    u1 = jnp.where(row >= first + 1, ext_ref[7:7 + BLOCK, :], 0.0)
    u2 = jnp.where(row >= first + 2, ext_ref[6:6 + BLOCK, :], 0.0)
    gb = z_ref[:, Q_DIM + 2 * KV_DIM:Z1_DIM]
    c = gb * (cw_ref[0:1, :] * u2 + cw_ref[1:2, :] * u1 + cw_ref[2:3, :] * u_ref[...])
    mix_ref[:, Q_DIM:] = c.astype(BF16)

    @pl.when(is_tail)
    def _():
        mix_ref[0:DEC_BATCH, :] = mixs_ref[...].astype(BF16)


def _prev_blk(i):
    return jnp.where(i % BLK_PER_SEQ == 0, N_XBLK, i - 1)


def _bias_variant(i):
    return jnp.where(i == N_XBLK, 0, jnp.where(i % BLK_PER_SEQ == 0, 1, 2))


def _mixer(sinks, z1, u, bias3, cw, mix_s):
    rows8 = BLOCK // 8
    seq_of = lambda i: jnp.minimum(i // BLK_PER_SEQ, BATCH - 1)
    return pl.pallas_call(
        _mixer_kernel,
        out_shape=(jax.ShapeDtypeStruct((M_ALL, D_MODEL), BF16),
                   jax.ShapeDtypeStruct((BATCH, KV_DIM, WINDOW), F32),
                   jax.ShapeDtypeStruct((BATCH, KV_DIM, WINDOW), F32),
                   jax.ShapeDtypeStruct((BATCH, CONV_WIDTH - 1, CONV_DIM), F32)),
        grid=(N_XBLK + 1,),
        in_specs=[pl.BlockSpec(memory_space=pltpu.SMEM),
                  pl.BlockSpec((BLOCK, Z1_DIM), lambda i: (i, 0)),
                  pl.BlockSpec((BLOCK, 2 * KV_DIM), lambda i: (_prev_blk(i), Q_DIM // (2 * KV_DIM))),
                  pl.BlockSpec((BLOCK, CONV_DIM), lambda i: (i, 0)),
                  pl.BlockSpec((8, CONV_DIM), lambda i: (_prev_blk(i) * rows8 + rows8 - 1, 0)),
                  pl.BlockSpec((1, N_HEADS, BLOCK, 2 * BLOCK), lambda i: (_bias_variant(i), 0, 0, 0)),
                  pl.BlockSpec((CONV_WIDTH, CONV_DIM), lambda i: (0, 0)),
                  pl.BlockSpec((DEC_BATCH, D_MODEL), lambda i: (0, 0))],
        out_specs=(pl.BlockSpec((BLOCK, D_MODEL), lambda i: (i, 0)),
                   pl.BlockSpec((1, KV_DIM, WINDOW), lambda i: (seq_of(i), 0, 0)),
                   pl.BlockSpec((1, KV_DIM, WINDOW), lambda i: (seq_of(i), 0, 0)),
                   pl.BlockSpec((1, CONV_WIDTH - 1, CONV_DIM), lambda i: (seq_of(i), 0, 0))),
        scratch_shapes=[pltpu.VMEM((BLOCK + 8, CONV_DIM), F32)],
        compiler_params=_cparams(("arbitrary",)),
        name="mixer",
    )(sinks, z1, z1, u, u, bias3, cw, mix_s)


S_CHUNK = 8
SH_ROWS = S_CHUNK * N_HEADS


def _sample_kernel(zs_ref, us_ref, kvt_ref, kc_ref, vc_ref, st_ref, cw_ref, sinkt_ref, tabt_ref,
                   brow_ref, fold_ref, unfold_ref, mixs_ref, kout_ref, vout_ref, cs_ref, *, layer):
    z = zs_ref[...]
    q = z[:, 0:Q_DIM] * SCALE
    qrep = jnp.concatenate([jnp.broadcast_to(q[s:s + 1, :], (N_HEADS, Q_DIM))
                            for s in range(S_CHUNK)], axis=0)
    row_head = lax.broadcasted_iota(jnp.int32, (SH_ROWS, Q_DIM), 0) % N_HEADS
    own_q = lax.broadcasted_iota(jnp.int32, (SH_ROWS, Q_DIM), 1) // HEAD_DIM == row_head
    qblk = jnp.dot(jnp.where(own_q, qrep, 0.0).astype(BF16), fold_ref[...],
                   preferred_element_type=F32)
    own_kv = (lax.broadcasted_iota(jnp.int32, (N_HEADS, KV_DIM), 1) // HEAD_DIM
              == lax.broadcasted_iota(jnp.int32, (N_HEADS, KV_DIM), 0) // GROUP)

    brow = brow_ref[...]
    tabt = tabt_ref[...]
    bias_t = jnp.zeros((N_HEADS, W_BUF), F32)
    for b in range(NUM_BUCKETS):
        bias_t = jnp.where(brow == b, tabt[:, b:b + 1], bias_t)
    bias0 = tabt[:, 0:1]
    sink = sinkt_ref[:, layer:layer + 1]
    last_key = lax.broadcasted_iota(jnp.int32, (KV_DIM, W_BUF), 1) == W_BUF - 1

    o_rows = []
    for s in range(S_CHUNK):
        rows = slice(s * KV_DIM, (s + 1) * KV_DIM)
        kt = kc_ref[rows, :]
        vt = vc_ref[rows, :]
        qb = qblk[s * N_HEADS:(s + 1) * N_HEADS, :]
        k_s = z[s:s + 1, Q_DIM:Q_DIM + KV_DIM]
        v_s = z[s:s + 1, Q_DIM + KV_DIM:Q_DIM + 2 * KV_DIM]
        sc = jnp.dot(qb.astype(BF16), kt.astype(BF16), preferred_element_type=F32) + bias_t
        sn = jnp.sum(qb * k_s, axis=-1, keepdims=True) + bias0
        m = jnp.maximum(jnp.maximum(jnp.max(sc, axis=-1, keepdims=True), sn), sink)
        e = jnp.exp(sc - m)
        en = jnp.exp(sn - m)
        den = jnp.sum(e, axis=-1, keepdims=True) + en + jnp.exp(sink - m)
        o = lax.dot_general((e / den).astype(BF16), vt.astype(BF16), (((1,), (1,)), ((), ())),
                            preferred_element_type=F32)
        o_rows.append(jnp.where(own_kv, o + (en / den) * v_s, 0.0))
        kout_ref[rows, :] = jnp.where(last_key, kvt_ref[0, 0:KV_DIM, s:s + 1],
                                      pltpu.roll(kt, W_BUF - 1, 1))
        vout_ref[rows, :] = jnp.where(last_key, kvt_ref[0, KV_DIM:2 * KV_DIM, s:s + 1],
                                      pltpu.roll(vt, W_BUF - 1, 1))

    a_all = jnp.dot(jnp.concatenate(o_rows, axis=0).astype(BF16), unfold_ref[...],
                    preferred_element_type=F32)
    a_all = jnp.where(own_q, a_all, 0.0)
    mixs_ref[:, 0:Q_DIM] = jnp.sum(a_all.reshape(S_CHUNK, N_HEADS, Q_DIM), axis=1)

    gb = z[:, Q_DIM + 2 * KV_DIM:Z1_DIM]
    u = us_ref[...]
    s0 = st_ref[:, 0:CONV_DIM]
    s1 = st_ref[:, CONV_DIM:]
    mixs_ref[:, Q_DIM:] = gb * (cw_ref[0:1, :] * s0 + cw_ref[1:2, :] * s1 + cw_ref[2:3, :] * u)
    cs_ref[:, 0:CONV_DIM] = s1
    cs_ref[:, CONV_DIM:] = u


def _sample_mixer(layer, z1, u, kvt, cache_kt, cache_vt, state2d, cw, sinkt, tabt, brow,
                  fold, unfold):
    n_chunks = DEC_BATCH // S_CHUNK
    row_blk0 = M_X // S_CHUNK
    full = lambda shape: pl.BlockSpec(shape, lambda c: (0,) * len(shape))
    return pl.pallas_call(
        functools.partial(_sample_kernel, layer=layer),
        out_shape=(jax.ShapeDtypeStruct((DEC_BATCH, D_MODEL), F32),
                   jax.ShapeDtypeStruct((DEC_BATCH * KV_DIM, W_BUF), F32),
                   jax.ShapeDtypeStruct((DEC_BATCH * KV_DIM, W_BUF), F32),
                   jax.ShapeDtypeStruct((DEC_BATCH, 2 * CONV_DIM), F32)),
        grid=(n_chunks,),
        in_specs=[pl.BlockSpec((S_CHUNK, Z1_DIM), lambda c: (row_blk0 + c, 0)),
                  pl.BlockSpec((S_CHUNK, CONV_DIM), lambda c: (row_blk0 + c, 0)),
                  pl.BlockSpec((1, 2 * KV_DIM, S_CHUNK), lambda c: (c, 0, 0)),
                  pl.BlockSpec((S_CHUNK * KV_DIM, W_BUF), lambda c: (layer * n_chunks + c, 0)),
                  pl.BlockSpec((S_CHUNK * KV_DIM, W_BUF), lambda c: (layer * n_chunks + c, 0)),
                  pl.BlockSpec((S_CHUNK, 2 * CONV_DIM), lambda c: (layer * n_chunks + c, 0)),
                  full((CONV_WIDTH, CONV_DIM)),
                  full((N_HEADS, DEPTH)),
                  full((N_HEADS, NUM_BUCKETS)),
                  full((1, W_BUF)),
                  full((Q_DIM, KV_DIM)),
                  full((KV_DIM, Q_DIM))],
        out_specs=(pl.BlockSpec((S_CHUNK, D_MODEL), lambda c: (c, 0)),
                   pl.BlockSpec((S_CHUNK * KV_DIM, W_BUF), lambda c: (c, 0)),
                   pl.BlockSpec((S_CHUNK * KV_DIM, W_BUF), lambda c: (c, 0)),
                   pl.BlockSpec((S_CHUNK, 2 * CONV_DIM), lambda c: (c, 0))),
        compiler_params=_cparams(("arbitrary",)),
        name="sample_mixer",
    )(z1, u, kvt, cache_kt, cache_vt, state2d, cw, sinkt, tabt, brow, fold, unfold)


def _t5_bucket(d):
    max_exact = NUM_BUCKETS // 2
    df = jnp.maximum(d, 1).astype(F32)
    large = max_exact + (jnp.log(df / max_exact) / math.log(MAX_DISTANCE / max_exact)
                         * (NUM_BUCKETS - max_exact)).astype(jnp.int32)
    large = jnp.minimum(large, NUM_BUCKETS - 1)
    return jnp.where(d < max_exact, d, large)


def _band_tables():
    i = np.arange(BLOCK)[:, None]
    j = np.arange(2 * BLOCK)[None, :]
    d = BLOCK + i - j
    band = (d >= 0) & (d <= WINDOW)
    valid = np.stack([band & (j >= BLOCK + META_ROW0), band & (j >= META_ROW0), band])
    bucket = _t5_bucket(jnp.asarray(np.maximum(d, 0), jnp.int32))
    return bucket, jnp.asarray(valid.astype(np.int32))


def _fold_tables():
    d = np.arange(HEAD_DIM)
    fold = np.zeros((Q_DIM, KV_DIM), np.float32)
    for h in range(N_HEADS):
        fold[h * HEAD_DIM + d, (h // GROUP) * HEAD_DIM + d] = 1.0
    return jnp.asarray(fold, BF16), jnp.asarray(fold.T, BF16)


def kernel(x_prompt, x_sample, cache_k, cache_v, state_conv, meta_tokens, rel_bias, w_in, conv_w,
           attn_sinks, w_out, norm_pre_mix, norm_post_mix, norm_pre_ffn, norm_post_ffn,
           w_gate, w_up, w_down):
    tail = jnp.concatenate([x_sample.reshape(DEC_BATCH, D_MODEL),
                            jnp.zeros((META_ROW0 - DEC_BATCH, D_MODEL), F32),
                            meta_tokens.astype(F32)], axis=0)

    bucket, valid = _band_tables()
    fold, unfold = _fold_tables()
    bias3 = _band_bias(rel_bias, bucket, valid)
    brow = _t5_bucket(jnp.asarray(W_BUF - np.arange(W_BUF), jnp.int32)).reshape(1, W_BUF)

    kv_lanes = lambda c: jnp.transpose(c, (0, 1, 3, 4, 2)).reshape(DEPTH * DEC_BATCH * KV_DIM, W_BUF)
    cache_kt = kv_lanes(cache_k)
    cache_vt = kv_lanes(cache_v)
    state2d = state_conv.reshape(DEPTH * DEC_BATCH, 2 * CONV_DIM)

    h, xn = _norm(x_prompt.reshape(M_X, D_MODEL), tail, norm_pre_mix)

    def kv_rows(t, n):
        return jnp.transpose(t.reshape(n, N_KV_HEADS, HEAD_DIM, WINDOW), (0, 3, 1, 2))

    kp, vp, cp, ks, vs, cs = [], [], [], [], [], []
    for l in range(DEPTH):
        z1, w_out_b = _matmul(xn, w_in, w_out, l, Z1_DIM)
        (u,) = _gated(xn, w_in, w_in, None, l, Z1_DIM, Z1_DIM + CONV_DIM, CONV_DIM,
                      False, F32, "conv_in")
        kvt = jnp.transpose(z1[M_X:M_X + DEC_BATCH, Q_DIM:Q_DIM + 2 * KV_DIM]
                            .reshape(DEC_BATCH // S_CHUNK, S_CHUNK, 2 * KV_DIM), (0, 2, 1))
        mix_s, k_new, v_new, c_new = _sample_mixer(
            l, z1, u, kvt, cache_kt, cache_vt, state2d, conv_w[l], attn_sinks.T, rel_bias.T,
            brow, fold, unfold)
        mix, kp_l, vp_l, cp_l = _mixer(attn_sinks[l], z1, u, bias3, conv_w[l], mix_s)
        h, xn = _proj_norm(mix, w_out_b, l, h, norm_post_mix, norm_pre_ffn, l, "out_proj")
        act, w_down_b = _gated(xn, w_gate, w_up, w_down, l, 0, 0, D_FF, True, BF16, "ffn_up")
        if l + 1 < DEPTH:
            h, xn = _proj_norm(act, w_down_b, l, h, norm_post_ffn, norm_pre_mix, l + 1, "ffn_down")
        else:
            y2d, y_tail = _proj_last(act, w_down_b, l, h, norm_post_ffn)

        kp.append(kv_rows(kp_l, BATCH))
        vp.append(kv_rows(vp_l, BATCH))
        cp.append(cp_l)
        ks.append(kv_rows(k_new, DEC_BATCH))
        vs.append(kv_rows(v_new, DEC_BATCH))
        cs.append(c_new.reshape(DEC_BATCH, CONV_WIDTH - 1, CONV_DIM))

    y_prompt = y2d.reshape(BATCH, SEQ, D_MODEL)
    y_sample = y_tail[0:DEC_BATCH].reshape(DEC_BATCH, 1, D_MODEL)
    return (y_prompt, y_sample, jnp.stack(kp), jnp.stack(vp), jnp.stack(cp),
            jnp.stack(ks), jnp.stack(vs), jnp.stack(cs))
```

```python
import functools
import math

import numpy as np
import jax
import jax.numpy as jnp
from jax import lax
from jax.experimental import pallas as pl
from jax.experimental.pallas import tpu as pltpu

D_MODEL = 2048
BATCH = 4
SEQ = 2048
DEPTH = 4
DEC_BATCH = 32
HEAD_DIM = 64
N_HEADS = 16
N_KV_HEADS = 4
GROUP = 4
Q_DIM = 1024
KV_DIM = 256
CONV_DIM = 1024
CONV_WIDTH = 3
IN_DIM = 4608
WINDOW = 128
BLOCK = 128
NUM_BUCKETS = 32
MAX_DISTANCE = 128
N_META = 16
D_FF = 5632
RMS_EPS = 1e-6
SCALE = HEAD_DIM ** -0.5
W_BUF = 128

M_X = BATCH * SEQ
N_XBLK = M_X // BLOCK
BLK_PER_SEQ = SEQ // BLOCK
M_ALL = M_X + BLOCK
META_ROW0 = BLOCK - N_META
Z1_DIM = Q_DIM + 2 * KV_DIM + CONV_DIM
NEG = -1e30
LOG2E = math.log2(math.e)

TM_BIG = 1664
TM_EPI = 640
N_EPI = M_ALL // TM_EPI
EPI_LAST_X = M_X - (N_EPI - 1) * TM_EPI
assert (M_X - EPI_LAST_X) % EPI_LAST_X == 0 and EPI_LAST_X + BLOCK == TM_EPI
TM_PROJ = 416
RC_PROJ = 208
N_PROJ = M_ALL // TM_PROJ
PROJ_LAST_X = M_X - (N_PROJ - 1) * TM_PROJ
assert PROJ_LAST_X + BLOCK == TM_PROJ and PROJ_LAST_X % 8 == 0
TN = 512
TM_IN = 832
TN_IN = 1280
VMEM_LIMIT = 56 * 1024 * 1024

F32 = jnp.float32
BF16 = jnp.bfloat16


def _cparams(sem):
    return pltpu.CompilerParams(dimension_semantics=sem, vmem_limit_bytes=VMEM_LIMIT)


def _rms(x, g):
    return x * lax.rsqrt(jnp.mean(x * x, axis=-1, keepdims=True) + RMS_EPS) * g


def _norm_kernel(xa_ref, xb_ref, tail_ref, g_ref, h_ref, xn_ref):
    last = pl.program_id(0) == N_EPI - 1
    h_last = jnp.concatenate([xb_ref[...], tail_ref[...]], axis=0)
    h = jnp.where(last, h_last, xa_ref[...])
    h_ref[...] = h
    xn_ref[...] = _rms(h, g_ref[0:1, :]).astype(BF16)


def _norm(x2d, tail, g):
    return pl.pallas_call(
        _norm_kernel,
        out_shape=(jax.ShapeDtypeStruct((M_ALL, D_MODEL), F32),
                   jax.ShapeDtypeStruct((M_ALL, D_MODEL), BF16)),
        grid=(N_EPI,),
        in_specs=[pl.BlockSpec((TM_EPI, D_MODEL), lambda i: (jnp.minimum(i, N_EPI - 2), 0)),
                  pl.BlockSpec((EPI_LAST_X, D_MODEL), lambda i: ((M_X - EPI_LAST_X) // EPI_LAST_X, 0)),
                  pl.BlockSpec((BLOCK, D_MODEL), lambda i: (0, 0)),
                  pl.BlockSpec((DEPTH, D_MODEL), lambda i: (0, 0))],
        out_specs=(pl.BlockSpec((TM_EPI, D_MODEL), lambda i: (i, 0)),
                   pl.BlockSpec((TM_EPI, D_MODEL), lambda i: (i, 0))),
        compiler_params=_cparams(("arbitrary",)),
        name="norm0",
    )(x2d, x2d, tail, g)


RIDER_ROWS = 128


def _rider(wr, layer, n_outer, n_inner):
    n_chunks = wr.shape[1] // RIDER_ROWS
    assert n_chunks <= n_outer * n_inner
    chunk = lambda j, i: jnp.minimum(j * n_inner + i, n_chunks - 1)
    in_spec = pl.BlockSpec((None, RIDER_ROWS, D_MODEL), lambda j, i: (layer, chunk(j, i), 0))
    out_spec = pl.BlockSpec((RIDER_ROWS, D_MODEL), lambda j, i: (chunk(j, i), 0))
    return in_spec, out_spec, jax.ShapeDtypeStruct((wr.shape[1], D_MODEL), BF16)


def _matmul_kernel(x_ref, w_ref, wr_ref, o_ref, wrb_ref, wbf_ref):
    @pl.when(pl.program_id(1) == 0)
    def _():
        wbf_ref[...] = w_ref[...].astype(BF16)

    o_ref[...] = jnp.dot(x_ref[...], wbf_ref[...], preferred_element_type=F32)
    wrb_ref[...] = wr_ref[...].astype(BF16)


def _matmul(x, w, wr, layer, n_cols):
    k = x.shape[1]
    n_outer, n_inner = n_cols // TN_IN, M_ALL // TM_IN
    r_in, r_out, r_shape = _rider(wr, layer, n_outer, n_inner)
    return pl.pallas_call(
        _matmul_kernel,
        out_shape=(jax.ShapeDtypeStruct((M_ALL, n_cols), F32), r_shape),
        grid=(n_outer, n_inner),
        in_specs=[pl.BlockSpec((TM_IN, k), lambda j, i: (i, 0)),
                  pl.BlockSpec((None, k, TN_IN), lambda j, i: (layer, 0, j)),
                  r_in],
        out_specs=(pl.BlockSpec((TM_IN, TN_IN), lambda j, i: (i, j)), r_out),
        scratch_shapes=[pltpu.VMEM((k, TN_IN), BF16)],
        compiler_params=_cparams(("arbitrary", "arbitrary")),
        name="in_proj",
    )(x, w, wr)


def _gated_kernel(*refs, silu, has_rider):
    if has_rider:
        x_ref, wa_ref, wb_ref, wr_ref, o_ref, wrb_ref, wbf_ref = refs
    else:
        x_ref, wa_ref, wb_ref, o_ref, wbf_ref = refs

    @pl.when(pl.program_id(1) == 0)
    def _():
        wbf_ref[:, 0:TN] = wa_ref[...].astype(BF16)
        wbf_ref[:, TN:2 * TN] = wb_ref[...].astype(BF16)

    ab = jnp.dot(x_ref[...], wbf_ref[...], preferred_element_type=F32)
    a = ab[:, 0:TN]
    b = ab[:, TN:2 * TN]
    if silu:
        a = a * (1.0 / (1.0 + jnp.exp(-a)))
    o_ref[...] = (a * b).astype(o_ref.dtype)
    if has_rider:
        wrb_ref[...] = wr_ref[...].astype(BF16)


def _gated(x, wa, wb, wr, layer, a_col0, b_col0, n_cols, silu, out_dtype, name):
    k = x.shape[1]
    a0 = a_col0 // TN
    b0 = b_col0 // TN
    n_outer, n_inner = n_cols // TN, M_ALL // TM_BIG
    in_specs = [pl.BlockSpec((TM_BIG, k), lambda j, i: (i, 0)),
                pl.BlockSpec((None, k, TN), lambda j, i: (layer, 0, a0 + j)),
                pl.BlockSpec((None, k, TN), lambda j, i: (layer, 0, b0 + j))]
    out_specs = [pl.BlockSpec((TM_BIG, TN), lambda j, i: (i, j))]
    out_shape = [jax.ShapeDtypeStruct((M_ALL, n_cols), out_dtype)]
    args = [x, wa, wb]
    if wr is not None:
        r_in, r_out, r_shape = _rider(wr, layer, n_outer, n_inner)
        in_specs.append(r_in)
        out_specs.append(r_out)
        out_shape.append(r_shape)
        args.append(wr)
    return pl.pallas_call(
        functools.partial(_gated_kernel, silu=silu, has_rider=wr is not None),
        out_shape=tuple(out_shape),
        grid=(n_outer, n_inner),
        in_specs=in_specs,
        out_specs=tuple(out_specs),
        scratch_shapes=[pltpu.VMEM((k, 2 * TN), BF16)],
        compiler_params=_cparams(("arbitrary", "arbitrary")),
        name=name,
    )(*args)


def _proj_norm_kernel(a_ref, w_ref, hin_ref, gpost_ref, gnext_ref, h_ref, xn_ref, *, l_post, l_next):
    g_post = gpost_ref[l_post:l_post + 1, :]
    g_next = gnext_ref[l_next:l_next + 1, :]
    for c in range(TM_PROJ // RC_PROJ):
        rows = slice(c * RC_PROJ, (c + 1) * RC_PROJ)
        y = jnp.dot(a_ref[rows, :], w_ref[...], preferred_element_type=F32)
        hn = hin_ref[rows, :] + _rms(y, g_post)
        h_ref[rows, :] = hn
        xn_ref[rows, :] = _rms(hn, g_next).astype(BF16)


def _proj_norm(a, w, layer, hin, g_post, g_next, l_next, name):
    k = a.shape[1]
    return pl.pallas_call(
        functools.partial(_proj_norm_kernel, l_post=layer, l_next=l_next),
        out_shape=(jax.ShapeDtypeStruct((M_ALL, D_MODEL), F32),
                   jax.ShapeDtypeStruct((M_ALL, D_MODEL), BF16)),
        grid=(M_ALL // TM_PROJ,),
        in_specs=[pl.BlockSpec((TM_PROJ, k), lambda i: (i, 0)),
                  pl.BlockSpec((k, D_MODEL), lambda i: (0, 0), pipeline_mode=pl.Buffered(1)),
                  pl.BlockSpec((TM_PROJ, D_MODEL), lambda i: (i, 0)),
                  pl.BlockSpec((DEPTH, D_MODEL), lambda i: (0, 0)),
                  pl.BlockSpec((DEPTH, D_MODEL), lambda i: (0, 0))],
        out_specs=(pl.BlockSpec((TM_PROJ, D_MODEL), lambda i: (i, 0)),
                   pl.BlockSpec((TM_PROJ, D_MODEL), lambda i: (i, 0))),
        compiler_params=_cparams(("parallel",)),
        name=name,
    )(a, w, hin, g_post, g_next)


def _proj_last_kernel(a_ref, w_ref, hin_ref, gpost_ref, y_ref, tail_ref, *, l_post):
    g_post = gpost_ref[l_post:l_post + 1, :]
    for c in range(TM_PROJ // RC_PROJ):
        rows = slice(c * RC_PROJ, (c + 1) * RC_PROJ)
        y = jnp.dot(a_ref[rows, :], w_ref[...], preferred_element_type=F32)
        y_ref[rows, :] = hin_ref[rows, :] + _rms(y, g_post)

    @pl.when(pl.program_id(0) == N_PROJ - 1)
    def _():
        tail_ref[...] = y_ref[PROJ_LAST_X:TM_PROJ, :]


def _proj_last(a, w, layer, hin, g_post):
    k = a.shape[1]
    return pl.pallas_call(
        functools.partial(_proj_last_kernel, l_post=layer),
        out_shape=(jax.ShapeDtypeStruct((M_X, D_MODEL), F32),
                   jax.ShapeDtypeStruct((BLOCK, D_MODEL), F32)),
        grid=(N_PROJ,),
        in_specs=[pl.BlockSpec((TM_PROJ, k), lambda i: (i, 0)),
                  pl.BlockSpec((k, D_MODEL), lambda i: (0, 0), pipeline_mode=pl.Buffered(1)),
                  pl.BlockSpec((TM_PROJ, D_MODEL), lambda i: (i, 0)),
                  pl.BlockSpec((DEPTH, D_MODEL), lambda i: (0, 0))],
        out_specs=(pl.BlockSpec((TM_PROJ, D_MODEL), lambda i: (i, 0)),
                   pl.BlockSpec((BLOCK, D_MODEL), lambda i: (0, 0))),
        compiler_params=_cparams(("arbitrary",)),
        name="ffn_down_last",
    )(a, w, hin, g_post)


def _band_bias_kernel(tab_ref, bucket_ref, valid_ref, o_ref):
    h = pl.program_id(0)
    bucket = bucket_ref[...]
    acc = jnp.zeros((BLOCK, 2 * BLOCK), F32)
    for b in range(NUM_BUCKETS):
        acc = jnp.where(bucket == b, tab_ref[b, h], acc)
    acc = acc * LOG2E
    for v in range(3):
        o_ref[v, 0] = jnp.where(valid_ref[v] != 0, acc, NEG)


def _band_bias(rel_bias, bucket, valid):
    return pl.pallas_call(
        _band_bias_kernel,
        out_shape=jax.ShapeDtypeStruct((3, N_HEADS, BLOCK, 2 * BLOCK), F32),
        grid=(N_HEADS,),
        in_specs=[pl.BlockSpec(memory_space=pltpu.SMEM),
                  pl.BlockSpec((BLOCK, 2 * BLOCK), lambda h: (0, 0)),
                  pl.BlockSpec((3, BLOCK, 2 * BLOCK), lambda h: (0, 0, 0))],
        out_specs=pl.BlockSpec((3, 1, BLOCK, 2 * BLOCK), lambda h: (0, h, 0, 0)),
        compiler_params=_cparams(("arbitrary",)),
        name="band_bias",
    )(rel_bias, bucket, valid)


def _mixer_kernel(sink_ref, z_ref, zp_ref, u_ref, up_ref, bias_ref, cw_ref, mixs_ref,
                  w_ref, hin_ref, gpost_ref, gnext_ref,
                  h_ref, xn_ref, kp_ref, vp_ref, cp_ref, ext_ref, mix_ref, lhs_ref, *, layer):
    t = pl.program_id(0)
    i = jnp.minimum(t, N_XBLK)
    is_tail = i == N_XBLK

    @pl.when(t == 0)
    def _():
        mix_ref[...] = jnp.zeros_like(mix_ref)

    lhs_ref[...] = mix_ref[...]
    y = jnp.dot(lhs_ref[...], w_ref[...], preferred_element_type=F32)
    hn = hin_ref[...] + _rms(y, gpost_ref[layer:layer + 1, :])
    h_ref[...] = hn
    xn_ref[...] = _rms(hn, gnext_ref[layer:layer + 1, :]).astype(BF16)

    q = (z_ref[:, 0:Q_DIM] * (SCALE * LOG2E)).astype(BF16)
    kk = jnp.concatenate([zp_ref[:, 0:KV_DIM], z_ref[:, Q_DIM:Q_DIM + KV_DIM]], axis=0)
    vv = jnp.concatenate([zp_ref[:, KV_DIM:2 * KV_DIM],
                          z_ref[:, Q_DIM + KV_DIM:Q_DIM + 2 * KV_DIM]], axis=0)

    lo_kv = lax.broadcasted_iota(jnp.int32, (2 * BLOCK, BLOCK), 1) < HEAD_DIM
    one_lo = jnp.where(lo_kv, 1.0, 0.0).astype(BF16)
    one_hi = jnp.where(lo_kv, 0.0, 1.0).astype(BF16)
    k2, rv = {}, {}
    for col in range(KV_DIM // BLOCK):
        kc = kk[:, col * BLOCK:(col + 1) * BLOCK]
        vc = vv[:, col * BLOCK:(col + 1) * BLOCK]
        ks = pltpu.roll(kc, HEAD_DIM, 1)
        vs = pltpu.roll(vc, HEAD_DIM, 1)
        for in_hi in range(2):
            kh = 2 * col + in_hi
            own_k, oth_k = (ks, kc) if in_hi else (kc, ks)
            own_v, oth_v = (vs, vc) if in_hi else (vc, vs)
            k2[kh] = jnp.where(lo_kv, own_k, oth_k).astype(BF16)
            rv[kh, 0] = jnp.concatenate([jnp.where(lo_kv, own_v, 0.0).astype(BF16), one_lo], axis=1)
            rv[kh, 1] = jnp.concatenate([jnp.where(lo_kv, 0.0, oth_v).astype(BF16), one_hi], axis=1)

    lane = lax.broadcasted_iota(jnp.int32, (BLOCK, BLOCK), 1)
    lo_half = lane < HEAD_DIM
    zero = jnp.zeros((BLOCK, BLOCK), BF16)
    for p in range(N_HEADS // 2):
        kh = p // 2
        qp = q[:, p * BLOCK:(p + 1) * BLOCK]
        acc = None
        sink_terms = []
        for half in range(2):
            h = 2 * p + half
            qm = jnp.where(lo_half if half == 0 else jnp.logical_not(lo_half), qp, zero)
            s = lax.dot_general(qm, k2[kh], (((1,), (1,)), ((), ())),
                                preferred_element_type=F32) + bias_ref[0, h]
            sk = sink_ref[h] * LOG2E
            m = jnp.maximum(jnp.max(s, axis=-1, keepdims=True), sk)
            e = jnp.exp2(s - m).astype(BF16)
            d = jnp.dot(e, rv[kh, half], preferred_element_type=F32)
            acc = d if acc is None else acc + d
            sink_terms.append(jnp.exp2(sk - m))
        den = acc[:, BLOCK:] + jnp.where(lo_half, sink_terms[0], sink_terms[1])
        mix_ref[:, p * BLOCK:(p + 1) * BLOCK] = (acc[:, :BLOCK] / den).astype(BF16)

    ext_ref[0:8, :] = up_ref[...]
    ext_ref[8:8 + BLOCK, :] = u_ref[...]
    row = lax.broadcasted_iota(jnp.int32, (BLOCK, CONV_DIM), 0)
    first = jnp.where(is_tail, META_ROW0, -8)
    u1 = jnp.where(row >= first + 1, ext_ref[7:7 + BLOCK, :], 0.0)
    u2 = jnp.where(row >= first + 2, ext_ref[6:6 + BLOCK, :], 0.0)
    gb = z_ref[:, Q_DIM + 2 * KV_DIM:Z1_DIM]
    c = gb * (cw_ref[0:1, :] * u2 + cw_ref[1:2, :] * u1 + cw_ref[2:3, :] * u_ref[...])
    mix_ref[:, Q_DIM:] = c.astype(BF16)

    @pl.when(is_tail)
    def _():
        mix_ref[0:DEC_BATCH, :] = mixs_ref[...].astype(BF16)

    @pl.when(jnp.logical_and(i < N_XBLK, i % BLK_PER_SEQ == BLK_PER_SEQ - 1))
    def _():
        kp_ref[0] = z_ref[:, Q_DIM:Q_DIM + KV_DIM].T
        vp_ref[0] = z_ref[:, Q_DIM + KV_DIM:Q_DIM + 2 * KV_DIM].T
        cp_ref[0] = u_ref[BLOCK - (CONV_WIDTH - 1):BLOCK, :]


def _prev_blk(i):
    return jnp.where(i % BLK_PER_SEQ == 0, N_XBLK, i - 1)


def _bias_variant(i):
    return jnp.where(i == N_XBLK, 0, jnp.where(i % BLK_PER_SEQ == 0, 1, 2))


def _mixer(layer, sinks, z1, u, bias3, cw, mix_s, w_out_b, hin, g_post, g_next):
    rows8 = BLOCK // 8
    blk = lambda t: jnp.minimum(t, N_XBLK)
    prj = lambda t: jnp.maximum(t - 1, 0)
    seq_of = lambda t: jnp.minimum(blk(t) // BLK_PER_SEQ, BATCH - 1)
    return pl.pallas_call(
        functools.partial(_mixer_kernel, layer=layer),
        out_shape=(jax.ShapeDtypeStruct((M_ALL, D_MODEL), F32),
                   jax.ShapeDtypeStruct((M_ALL, D_MODEL), BF16),
                   jax.ShapeDtypeStruct((BATCH, KV_DIM, WINDOW), F32),
                   jax.ShapeDtypeStruct((BATCH, KV_DIM, WINDOW), F32),
                   jax.ShapeDtypeStruct((BATCH, CONV_WIDTH - 1, CONV_DIM), F32)),
        grid=(N_XBLK + 2,),
        in_specs=[pl.BlockSpec(memory_space=pltpu.SMEM),
                  pl.BlockSpec((BLOCK, Z1_DIM), lambda t: (blk(t), 0)),
                  pl.BlockSpec((BLOCK, 2 * KV_DIM),
                               lambda t: (_prev_blk(blk(t)), Q_DIM // (2 * KV_DIM))),
                  pl.BlockSpec((BLOCK, CONV_DIM), lambda t: (blk(t), 0)),
                  pl.BlockSpec((8, CONV_DIM), lambda t: (_prev_blk(blk(t)) * rows8 + rows8 - 1, 0)),
                  pl.BlockSpec((1, N_HEADS, BLOCK, 2 * BLOCK),
                               lambda t: (_bias_variant(blk(t)), 0, 0, 0)),
                  pl.BlockSpec((CONV_WIDTH, CONV_DIM), lambda t: (0, 0)),
                  pl.BlockSpec((DEC_BATCH, D_MODEL), lambda t: (0, 0)),
                  pl.BlockSpec((D_MODEL, D_MODEL), lambda t: (0, 0), pipeline_mode=pl.Buffered(1)),
                  pl.BlockSpec((BLOCK, D_MODEL), lambda t: (prj(t), 0)),
                  pl.BlockSpec((DEPTH, D_MODEL), lambda t: (0, 0)),
                  pl.BlockSpec((DEPTH, D_MODEL), lambda t: (0, 0))],
        out_specs=(pl.BlockSpec((BLOCK, D_MODEL), lambda t: (prj(t), 0)),
                   pl.BlockSpec((BLOCK, D_MODEL), lambda t: (prj(t), 0)),
                   pl.BlockSpec((1, KV_DIM, WINDOW), lambda t: (seq_of(t), 0, 0)),
                   pl.BlockSpec((1, KV_DIM, WINDOW), lambda t: (seq_of(t), 0, 0)),
                   pl.BlockSpec((1, CONV_WIDTH - 1, CONV_DIM), lambda t: (seq_of(t), 0, 0))),
        scratch_shapes=[pltpu.VMEM((BLOCK + 8, CONV_DIM), F32),
                        pltpu.VMEM((BLOCK, D_MODEL), BF16),
                        pltpu.VMEM((BLOCK, D_MODEL), BF16)],
        compiler_params=_cparams(("arbitrary",)),
        name="mixer",
    )(sinks, z1, z1, u, u, bias3, cw, mix_s, w_out_b, hin, g_post, g_next)


S_CHUNK = 8
SH_ROWS = S_CHUNK * N_HEADS


def _sample_kernel(zs_ref, us_ref, kvt_ref, kc_ref, vc_ref, st_ref, cw_ref, sinkt_ref, tabt_ref,
                   brow_ref, fold_ref, unfold_ref, mixs_ref, kout_ref, vout_ref, cs_ref, *, layer):
    z = zs_ref[...]
    q = z[:, 0:Q_DIM] * SCALE
    qrep = jnp.concatenate([jnp.broadcast_to(q[s:s + 1, :], (N_HEADS, Q_DIM))
                            for s in range(S_CHUNK)], axis=0)
    row_head = lax.broadcasted_iota(jnp.int32, (SH_ROWS, Q_DIM), 0) % N_HEADS
    own_q = lax.broadcasted_iota(jnp.int32, (SH_ROWS, Q_DIM), 1) // HEAD_DIM == row_head
    qblk = jnp.dot(jnp.where(own_q, qrep, 0.0).astype(BF16), fold_ref[...],
                   preferred_element_type=F32)
    own_kv = (lax.broadcasted_iota(jnp.int32, (N_HEADS, KV_DIM), 1) // HEAD_DIM
              == lax.broadcasted_iota(jnp.int32, (N_HEADS, KV_DIM), 0) // GROUP)

    brow = brow_ref[...]
    tabt = tabt_ref[...]
    bias_t = jnp.zeros((N_HEADS, W_BUF), F32)
    for b in range(NUM_BUCKETS):
        bias_t = jnp.where(brow == b, tabt[:, b:b + 1], bias_t)
    bias0 = tabt[:, 0:1]
    sink = sinkt_ref[:, layer:layer + 1]
    last_key = lax.broadcasted_iota(jnp.int32, (KV_DIM, W_BUF), 1) == W_BUF - 1

    o_rows = []
    for s in range(S_CHUNK):
        rows = slice(s * KV_DIM, (s + 1) * KV_DIM)
        kt = kc_ref[rows, :]
        vt = vc_ref[rows, :]
        qb = qblk[s * N_HEADS:(s + 1) * N_HEADS, :]
        k_s = z[s:s + 1, Q_DIM:Q_DIM + KV_DIM]
        v_s = z[s:s + 1, Q_DIM + KV_DIM:Q_DIM + 2 * KV_DIM]
        sc = jnp.dot(qb.astype(BF16), kt.astype(BF16), preferred_element_type=F32) + bias_t
        sn = jnp.sum(qb * k_s, axis=-1, keepdims=True) + bias0
        m = jnp.maximum(jnp.maximum(jnp.max(sc, axis=-1, keepdims=True), sn), sink)
        e = jnp.exp(sc - m)
        en = jnp.exp(sn - m)
        den = jnp.sum(e, axis=-1, keepdims=True) + en + jnp.exp(sink - m)
        o = lax.dot_general((e / den).astype(BF16), vt.astype(BF16), (((1,), (1,)), ((), ())),
                            preferred_element_type=F32)
        o_rows.append(jnp.where(own_kv, o + (en / den) * v_s, 0.0))
        kout_ref[rows, :] = jnp.where(last_key, kvt_ref[0, 0:KV_DIM, s:s + 1],
                                      pltpu.roll(kt, W_BUF - 1, 1))
        vout_ref[rows, :] = jnp.where(last_key, kvt_ref[0, KV_DIM:2 * KV_DIM, s:s + 1],
                                      pltpu.roll(vt, W_BUF - 1, 1))

    a_all = jnp.dot(jnp.concatenate(o_rows, axis=0).astype(BF16), unfold_ref[...],
                    preferred_element_type=F32)
    a_all = jnp.where(own_q, a_all, 0.0)
    mixs_ref[:, 0:Q_DIM] = jnp.sum(a_all.reshape(S_CHUNK, N_HEADS, Q_DIM), axis=1)

    gb = z[:, Q_DIM + 2 * KV_DIM:Z1_DIM]
    u = us_ref[...]
    s0 = st_ref[:, 0:CONV_DIM]
    s1 = st_ref[:, CONV_DIM:]
    mixs_ref[:, Q_DIM:] = gb * (cw_ref[0:1, :] * s0 + cw_ref[1:2, :] * s1 + cw_ref[2:3, :] * u)
    cs_ref[:, 0:CONV_DIM] = s1
    cs_ref[:, CONV_DIM:] = u


def _sample_mixer(layer, z1, u, kvt, cache_kt, cache_vt, state2d, cw, sinkt, tabt, brow,
                  fold, unfold):
    n_chunks = DEC_BATCH // S_CHUNK
    row_blk0 = M_X // S_CHUNK
    full = lambda shape: pl.BlockSpec(shape, lambda c: (0,) * len(shape))
    return pl.pallas_call(
        functools.partial(_sample_kernel, layer=layer),
        out_shape=(jax.ShapeDtypeStruct((DEC_BATCH, D_MODEL), F32),
                   jax.ShapeDtypeStruct((DEC_BATCH * KV_DIM, W_BUF), F32),
                   jax.ShapeDtypeStruct((DEC_BATCH * KV_DIM, W_BUF), F32),
                   jax.ShapeDtypeStruct((DEC_BATCH, 2 * CONV_DIM), F32)),
        grid=(n_chunks,),
        in_specs=[pl.BlockSpec((S_CHUNK, Z1_DIM), lambda c: (row_blk0 + c, 0)),
                  pl.BlockSpec((S_CHUNK, CONV_DIM), lambda c: (row_blk0 + c, 0)),
                  pl.BlockSpec((1, 2 * KV_DIM, S_CHUNK), lambda c: (c, 0, 0)),
                  pl.BlockSpec((S_CHUNK * KV_DIM, W_BUF), lambda c: (layer * n_chunks + c, 0)),
                  pl.BlockSpec((S_CHUNK * KV_DIM, W_BUF), lambda c: (layer * n_chunks + c, 0)),
                  pl.BlockSpec((S_CHUNK, 2 * CONV_DIM), lambda c: (layer * n_chunks + c, 0)),
                  full((CONV_WIDTH, CONV_DIM)),
                  full((N_HEADS, DEPTH)),
                  full((N_HEADS, NUM_BUCKETS)),
                  full((1, W_BUF)),
                  full((Q_DIM, KV_DIM)),
                  full((KV_DIM, Q_DIM))],
        out_specs=(pl.BlockSpec((S_CHUNK, D_MODEL), lambda c: (c, 0)),
                   pl.BlockSpec((S_CHUNK * KV_DIM, W_BUF), lambda c: (c, 0)),
                   pl.BlockSpec((S_CHUNK * KV_DIM, W_BUF), lambda c: (c, 0)),
                   pl.BlockSpec((S_CHUNK, 2 * CONV_DIM), lambda c: (c, 0))),
        compiler_params=_cparams(("arbitrary",)),
        name="sample_mixer",
    )(z1, u, kvt, cache_kt, cache_vt, state2d, cw, sinkt, tabt, brow, fold, unfold)


def _t5_bucket(d):
    max_exact = NUM_BUCKETS // 2
    df = jnp.maximum(d, 1).astype(F32)
    large = max_exact + (jnp.log(df / max_exact) / math.log(MAX_DISTANCE / max_exact)
                         * (NUM_BUCKETS - max_exact)).astype(jnp.int32)
    large = jnp.minimum(large, NUM_BUCKETS - 1)
    return jnp.where(d < max_exact, d, large)


def _band_tables():
    i = np.arange(BLOCK)[:, None]
    j = np.arange(2 * BLOCK)[None, :]
    d = BLOCK + i - j
    band = (d >= 0) & (d <= WINDOW)
    valid = np.stack([band & (j >= BLOCK + META_ROW0), band & (j >= META_ROW0), band])
    bucket = _t5_bucket(jnp.asarray(np.maximum(d, 0), jnp.int32))
    return bucket, jnp.asarray(valid.astype(np.int32))


def _fold_tables():
    d = np.arange(HEAD_DIM)
    fold = np.zeros((Q_DIM, KV_DIM), np.float32)
    for h in range(N_HEADS):
        fold[h * HEAD_DIM + d, (h // GROUP) * HEAD_DIM + d] = 1.0
    return jnp.asarray(fold, BF16), jnp.asarray(fold.T, BF16)


def kernel(x_prompt, x_sample, cache_k, cache_v, state_conv, meta_tokens, rel_bias, w_in, conv_w,
           attn_sinks, w_out, norm_pre_mix, norm_post_mix, norm_pre_ffn, norm_post_ffn,
           w_gate, w_up, w_down):
    tail = jnp.concatenate([x_sample.reshape(DEC_BATCH, D_MODEL),
                            jnp.zeros((META_ROW0 - DEC_BATCH, D_MODEL), F32),
                            meta_tokens.astype(F32)], axis=0)

    bucket, valid = _band_tables()
    fold, unfold = _fold_tables()
    bias3 = _band_bias(rel_bias, bucket, valid)
    brow = _t5_bucket(jnp.asarray(W_BUF - np.arange(W_BUF), jnp.int32)).reshape(1, W_BUF)

    kv_lanes = lambda c: jnp.transpose(c, (0, 1, 3, 4, 2)).reshape(DEPTH * DEC_BATCH * KV_DIM, W_BUF)
    cache_kt = kv_lanes(cache_k)
    cache_vt = kv_lanes(cache_v)
    state2d = state_conv.reshape(DEPTH * DEC_BATCH, 2 * CONV_DIM)

    h, xn = _norm(x_prompt.reshape(M_X, D_MODEL), tail, norm_pre_mix)

    def kv_rows(t, n):
        return jnp.transpose(t.reshape(n, N_KV_HEADS, HEAD_DIM, WINDOW), (0, 3, 1, 2))

    kp, vp, cp, ks, vs, cs = [], [], [], [], [], []
    for l in range(DEPTH):
        z1, w_out_b = _matmul(xn, w_in, w_out, l, Z1_DIM)
        (u,) = _gated(xn, w_in, w_in, None, l, Z1_DIM, Z1_DIM + CONV_DIM, CONV_DIM,
                      False, F32, "conv_in")
        kvt = jnp.transpose(z1[M_X:M_X + DEC_BATCH, Q_DIM:Q_DIM + 2 * KV_DIM]
                            .reshape(DEC_BATCH // S_CHUNK, S_CHUNK, 2 * KV_DIM), (0, 2, 1))
        mix_s, k_new, v_new, c_new = _sample_mixer(
            l, z1, u, kvt, cache_kt, cache_vt, state2d, conv_w[l], attn_sinks.T, rel_bias.T,
            brow, fold, unfold)
        h, xn, kp_l, vp_l, cp_l = _mixer(l, attn_sinks[l], z1, u, bias3, conv_w[l], mix_s,
                                         w_out_b, h, norm_post_mix, norm_pre_ffn)
        act, w_down_b = _gated(xn, w_gate, w_up, w_down, l, 0, 0, D_FF, True, BF16, "ffn_up")
        if l + 1 < DEPTH:
            h, xn = _proj_norm(act, w_down_b, l, h, norm_post_ffn, norm_pre_mix, l + 1, "ffn_down")
        else:
            y2d, y_tail = _proj_last(act, w_down_b, l, h, norm_post_ffn)

        kp.append(kv_rows(kp_l, BATCH))
        vp.append(kv_rows(vp_l, BATCH))
        cp.append(cp_l)
        ks.append(kv_rows(k_new, DEC_BATCH))
        vs.append(kv_rows(v_new, DEC_BATCH))
        cs.append(c_new.reshape(DEC_BATCH, CONV_WIDTH - 1, CONV_DIM))

    y_prompt = y2d.reshape(BATCH, SEQ, D_MODEL)
    y_sample = y_tail[0:DEC_BATCH].reshape(DEC_BATCH, 1, D_MODEL)
    return (y_prompt, y_sample, jnp.stack(kp), jnp.stack(vp), jnp.stack(cp),
            jnp.stack(ks), jnp.stack(vs), jnp.stack(cs))
```

```python
import functools
import math

import numpy as np
import jax
import jax.numpy as jnp
from jax import lax
from jax.experimental import pallas as pl
from jax.experimental.pallas import tpu as pltpu

D_MODEL = 2048
BATCH = 4
SEQ = 2048
DEPTH = 4
DEC_BATCH = 32
HEAD_DIM = 64
N_HEADS = 16
N_KV_HEADS = 4
GROUP = 4
Q_DIM = 1024
KV_DIM = 256
CONV_DIM = 1024
CONV_WIDTH = 3
IN_DIM = 4608
WINDOW = 128
BLOCK = 128
NUM_BUCKETS = 32
MAX_DISTANCE = 128
N_META = 16
D_FF = 5632
RMS_EPS = 1e-6
SCALE = HEAD_DIM ** -0.5
W_BUF = 128

M_X = BATCH * SEQ
N_XBLK = M_X // BLOCK
BLK_PER_SEQ = SEQ // BLOCK
M_ALL = M_X + BLOCK
META_ROW0 = BLOCK - N_META
Z1_DIM = Q_DIM + 2 * KV_DIM + CONV_DIM
NEG = -1e30
LOG2E = math.log2(math.e)

TM_BIG = 1664
TM_EPI = 640
N_EPI = M_ALL // TM_EPI
EPI_LAST_X = M_X - (N_EPI - 1) * TM_EPI
assert (M_X - EPI_LAST_X) % EPI_LAST_X == 0 and EPI_LAST_X + BLOCK == TM_EPI
TM_PROJ = 416
RC_PROJ = 208
N_PROJ = M_ALL // TM_PROJ
PROJ_LAST_X = M_X - (N_PROJ - 1) * TM_PROJ
assert PROJ_LAST_X + BLOCK == TM_PROJ and PROJ_LAST_X % 8 == 0
TN = 512
TM_IN = 832
TN_IN = 1280
VMEM_LIMIT = 56 * 1024 * 1024

F32 = jnp.float32
BF16 = jnp.bfloat16


def _cparams(sem):
    return pltpu.CompilerParams(dimension_semantics=sem, vmem_limit_bytes=VMEM_LIMIT)


def _rms(x, g):
    return x * lax.rsqrt(jnp.mean(x * x, axis=-1, keepdims=True) + RMS_EPS) * g


def _norm_kernel(xa_ref, xb_ref, tail_ref, g_ref, h_ref, xn_ref):
    last = pl.program_id(0) == N_EPI - 1
    h_last = jnp.concatenate([xb_ref[...], tail_ref[...]], axis=0)
    h = jnp.where(last, h_last, xa_ref[...])
    h_ref[...] = h
    xn_ref[...] = _rms(h, g_ref[0:1, :]).astype(BF16)


def _norm(x2d, tail, g):
    return pl.pallas_call(
        _norm_kernel,
        out_shape=(jax.ShapeDtypeStruct((M_ALL, D_MODEL), F32),
                   jax.ShapeDtypeStruct((M_ALL, D_MODEL), BF16)),
        grid=(N_EPI,),
        in_specs=[pl.BlockSpec((TM_EPI, D_MODEL), lambda i: (jnp.minimum(i, N_EPI - 2), 0)),
                  pl.BlockSpec((EPI_LAST_X, D_MODEL), lambda i: ((M_X - EPI_LAST_X) // EPI_LAST_X, 0)),
                  pl.BlockSpec((BLOCK, D_MODEL), lambda i: (0, 0)),
                  pl.BlockSpec((DEPTH, D_MODEL), lambda i: (0, 0))],
        out_specs=(pl.BlockSpec((TM_EPI, D_MODEL), lambda i: (i, 0)),
                   pl.BlockSpec((TM_EPI, D_MODEL), lambda i: (i, 0))),
        compiler_params=_cparams(("arbitrary",)),
        name="norm0",
    )(x2d, x2d, tail, g)


RIDER_ROWS = 128


def _pack_bf16(w_f32):
    return pltpu.bitcast(w_f32.astype(BF16), jnp.uint32)


def _unpack_bf16(w_u32):
    return pltpu.bitcast(w_u32, BF16)


def _rider(wr, layer, n_outer, n_inner):
    n_chunks = wr.shape[1] // RIDER_ROWS
    assert n_chunks <= n_outer * n_inner
    chunk = lambda j, i: jnp.minimum(j * n_inner + i, n_chunks - 1)
    in_spec = pl.BlockSpec((None, RIDER_ROWS, D_MODEL), lambda j, i: (layer, chunk(j, i), 0))
    out_spec = pl.BlockSpec((RIDER_ROWS // 2, D_MODEL), lambda j, i: (chunk(j, i), 0))
    return in_spec, out_spec, jax.ShapeDtypeStruct((wr.shape[1] // 2, D_MODEL), jnp.uint32)


def _matmul_kernel(x_ref, w_ref, wr_ref, o_ref, wrb_ref, wbf_ref):
    @pl.when(pl.program_id(1) == 0)
    def _():
        wbf_ref[...] = w_ref[...].astype(BF16)

    o_ref[...] = jnp.dot(x_ref[...], wbf_ref[...], preferred_element_type=F32)
    wrb_ref[...] = _pack_bf16(wr_ref[...])


def _matmul(x, w, wr, layer, n_cols):
    k = x.shape[1]
    n_outer, n_inner = n_cols // TN_IN, M_ALL // TM_IN
    r_in, r_out, r_shape = _rider(wr, layer, n_outer, n_inner)
    return pl.pallas_call(
        _matmul_kernel,
        out_shape=(jax.ShapeDtypeStruct((M_ALL, n_cols), F32), r_shape),
        grid=(n_outer, n_inner),
        in_specs=[pl.BlockSpec((TM_IN, k), lambda j, i: (i, 0)),
                  pl.BlockSpec((None, k, TN_IN), lambda j, i: (layer, 0, j)),
                  r_in],
        out_specs=(pl.BlockSpec((TM_IN, TN_IN), lambda j, i: (i, j)), r_out),
        scratch_shapes=[pltpu.VMEM((k, TN_IN), BF16)],
        compiler_params=_cparams(("arbitrary", "arbitrary")),
        name="in_proj",
    )(x, w, wr)


def _gated_kernel(*refs, silu, has_rider):
    if has_rider:
        x_ref, wa_ref, wb_ref, wr_ref, o_ref, wrb_ref, wbf_ref = refs
    else:
        x_ref, wa_ref, wb_ref, o_ref, wbf_ref = refs

    @pl.when(pl.program_id(1) == 0)
    def _():
        wbf_ref[:, 0:TN] = wa_ref[...].astype(BF16)
        wbf_ref[:, TN:2 * TN] = wb_ref[...].astype(BF16)

    ab = jnp.dot(x_ref[...], wbf_ref[...], preferred_element_type=F32)
    a = ab[:, 0:TN]
    b = ab[:, TN:2 * TN]
    if silu:
        a = a * (1.0 / (1.0 + jnp.exp(-a)))
    o_ref[...] = (a * b).astype(o_ref.dtype)
    if has_rider:
        wrb_ref[...] = _pack_bf16(wr_ref[...])


def _gated(x, wa, wb, wr, layer, a_col0, b_col0, n_cols, silu, out_dtype, name):
    k = x.shape[1]
    a0 = a_col0 // TN
    b0 = b_col0 // TN
    n_outer, n_inner = n_cols // TN, M_ALL // TM_BIG
    in_specs = [pl.BlockSpec((TM_BIG, k), lambda j, i: (i, 0)),
                pl.BlockSpec((None, k, TN), lambda j, i: (layer, 0, a0 + j)),
                pl.BlockSpec((None, k, TN), lambda j, i: (layer, 0, b0 + j))]
    out_specs = [pl.BlockSpec((TM_BIG, TN), lambda j, i: (i, j))]
    out_shape = [jax.ShapeDtypeStruct((M_ALL, n_cols), out_dtype)]
    args = [x, wa, wb]
    if wr is not None:
        r_in, r_out, r_shape = _rider(wr, layer, n_outer, n_inner)
        in_specs.append(r_in)
        out_specs.append(r_out)
        out_shape.append(r_shape)
        args.append(wr)
    return pl.pallas_call(
        functools.partial(_gated_kernel, silu=silu, has_rider=wr is not None),
        out_shape=tuple(out_shape),
        grid=(n_outer, n_inner),
        in_specs=in_specs,
        out_specs=tuple(out_specs),
        scratch_shapes=[pltpu.VMEM((k, 2 * TN), BF16)],
        compiler_params=_cparams(("arbitrary", "arbitrary")),
        name=name,
    )(*args)


def _proj_norm_kernel(a_ref, w_ref, hin_ref, gpost_ref, gnext_ref, h_ref, xn_ref, *, l_post, l_next):
    g_post = gpost_ref[l_post:l_post + 1, :]
    g_next = gnext_ref[l_next:l_next + 1, :]
    for c in range(TM_PROJ // RC_PROJ):
        rows = slice(c * RC_PROJ, (c + 1) * RC_PROJ)
        y = jnp.dot(a_ref[rows, :], _unpack_bf16(w_ref[...]), preferred_element_type=F32)
        hn = hin_ref[rows, :] + _rms(y, g_post)
        h_ref[rows, :] = hn
        xn_ref[rows, :] = _rms(hn, g_next).astype(BF16)


def _proj_norm(a, w, layer, hin, g_post, g_next, l_next, name):
    k = a.shape[1]
    return pl.pallas_call(
        functools.partial(_proj_norm_kernel, l_post=layer, l_next=l_next),
        out_shape=(jax.ShapeDtypeStruct((M_ALL, D_MODEL), F32),
                   jax.ShapeDtypeStruct((M_ALL, D_MODEL), BF16)),
        grid=(M_ALL // TM_PROJ,),
        in_specs=[pl.BlockSpec((TM_PROJ, k), lambda i: (i, 0)),
                  pl.BlockSpec((k // 2, D_MODEL), lambda i: (0, 0), pipeline_mode=pl.Buffered(1)),
                  pl.BlockSpec((TM_PROJ, D_MODEL), lambda i: (i, 0)),
                  pl.BlockSpec((DEPTH, D_MODEL), lambda i: (0, 0)),
                  pl.BlockSpec((DEPTH, D_MODEL), lambda i: (0, 0))],
        out_specs=(pl.BlockSpec((TM_PROJ, D_MODEL), lambda i: (i, 0)),
                   pl.BlockSpec((TM_PROJ, D_MODEL), lambda i: (i, 0))),
        compiler_params=_cparams(("parallel",)),
        name=name,
    )(a, w, hin, g_post, g_next)


def _proj_last_kernel(a_ref, w_ref, hin_ref, gpost_ref, y_ref, tail_ref, *, l_post):
    g_post = gpost_ref[l_post:l_post + 1, :]
    for c in range(TM_PROJ // RC_PROJ):
        rows = slice(c * RC_PROJ, (c + 1) * RC_PROJ)
        y = jnp.dot(a_ref[rows, :], _unpack_bf16(w_ref[...]), preferred_element_type=F32)
        y_ref[rows, :] = hin_ref[rows, :] + _rms(y, g_post)

    @pl.when(pl.program_id(0) == N_PROJ - 1)
    def _():
        tail_ref[...] = y_ref[PROJ_LAST_X:TM_PROJ, :]


def _proj_last(a, w, layer, hin, g_post):
    k = a.shape[1]
    return pl.pallas_call(
        functools.partial(_proj_last_kernel, l_post=layer),
        out_shape=(jax.ShapeDtypeStruct((M_X, D_MODEL), F32),
                   jax.ShapeDtypeStruct((BLOCK, D_MODEL), F32)),
        grid=(N_PROJ,),
        in_specs=[pl.BlockSpec((TM_PROJ, k), lambda i: (i, 0)),
                  pl.BlockSpec((k // 2, D_MODEL), lambda i: (0, 0), pipeline_mode=pl.Buffered(1)),
                  pl.BlockSpec((TM_PROJ, D_MODEL), lambda i: (i, 0)),
                  pl.BlockSpec((DEPTH, D_MODEL), lambda i: (0, 0))],
        out_specs=(pl.BlockSpec((TM_PROJ, D_MODEL), lambda i: (i, 0)),
                   pl.BlockSpec((BLOCK, D_MODEL), lambda i: (0, 0))),
        compiler_params=_cparams(("arbitrary",)),
        name="ffn_down_last",
    )(a, w, hin, g_post)


def _band_bias_kernel(tab_ref, bucket_ref, valid_ref, o_ref):
    h = pl.program_id(0)
    bucket = bucket_ref[...]
    acc = jnp.zeros((BLOCK, 2 * BLOCK), F32)
    for b in range(NUM_BUCKETS):
        acc = jnp.where(bucket == b, tab_ref[b, h], acc)
    acc = acc * LOG2E
    for v in range(3):
        o_ref[v, 0] = jnp.where(valid_ref[v] != 0, acc, NEG)


def _band_bias(rel_bias, bucket, valid):
    return pl.pallas_call(
        _band_bias_kernel,
        out_shape=jax.ShapeDtypeStruct((3, N_HEADS, BLOCK, 2 * BLOCK), F32),
        grid=(N_HEADS,),
        in_specs=[pl.BlockSpec(memory_space=pltpu.SMEM),
                  pl.BlockSpec((BLOCK, 2 * BLOCK), lambda h: (0, 0)),
                  pl.BlockSpec((3, BLOCK, 2 * BLOCK), lambda h: (0, 0, 0))],
        out_specs=pl.BlockSpec((3, 1, BLOCK, 2 * BLOCK), lambda h: (0, h, 0, 0)),
        compiler_params=_cparams(("arbitrary",)),
        name="band_bias",
    )(rel_bias, bucket, valid)


def _mixer_kernel(sink_ref, z_ref, zp_ref, u_ref, up_ref, bias_ref, cw_ref, mixs_ref,
                  w_ref, hin_ref, gpost_ref, gnext_ref,
                  h_ref, xn_ref, kp_ref, vp_ref, cp_ref, ext_ref, mix_ref, lhs_ref, *, layer):
    t = pl.program_id(0)
    i = jnp.minimum(t, N_XBLK)
    is_tail = i == N_XBLK

    @pl.when(t == 0)
    def _():
        mix_ref[...] = jnp.zeros_like(mix_ref)

    lhs_ref[...] = mix_ref[...]
    y = jnp.dot(lhs_ref[...], _unpack_bf16(w_ref[...]), preferred_element_type=F32)
    hn = hin_ref[...] + _rms(y, gpost_ref[layer:layer + 1, :])
    h_ref[...] = hn
    xn_ref[...] = _rms(hn, gnext_ref[layer:layer + 1, :]).astype(BF16)

    q = (z_ref[:, 0:Q_DIM] * (SCALE * LOG2E)).astype(BF16)
    kk = jnp.concatenate([zp_ref[:, 0:KV_DIM], z_ref[:, Q_DIM:Q_DIM + KV_DIM]], axis=0)
    vv = jnp.concatenate([zp_ref[:, KV_DIM:2 * KV_DIM],
                          z_ref[:, Q_DIM + KV_DIM:Q_DIM + 2 * KV_DIM]], axis=0)

    lo_kv = lax.broadcasted_iota(jnp.int32, (2 * BLOCK, BLOCK), 1) < HEAD_DIM
    one_lo = jnp.where(lo_kv, 1.0, 0.0).astype(BF16)
    one_hi = jnp.where(lo_kv, 0.0, 1.0).astype(BF16)
    k2, rv = {}, {}
    for col in range(KV_DIM // BLOCK):
        kc = kk[:, col * BLOCK:(col + 1) * BLOCK]
        vc = vv[:, col * BLOCK:(col + 1) * BLOCK]
        ks = pltpu.roll(kc, HEAD_DIM, 1)
        vs = pltpu.roll(vc, HEAD_DIM, 1)
        for in_hi in range(2):
            kh = 2 * col + in_hi
            own_k, oth_k = (ks, kc) if in_hi else (kc, ks)
            own_v, oth_v = (vs, vc) if in_hi else (vc, vs)
            k2[kh] = jnp.where(lo_kv, own_k, oth_k).astype(BF16)
            rv[kh, 0] = jnp.concatenate([jnp.where(lo_kv, own_v, 0.0).astype(BF16), one_lo], axis=1)
            rv[kh, 1] = jnp.concatenate([jnp.where(lo_kv, 0.0, oth_v).astype(BF16), one_hi], axis=1)

    lane = lax.broadcasted_iota(jnp.int32, (BLOCK, BLOCK), 1)
    lo_half = lane < HEAD_DIM
    zero = jnp.zeros((BLOCK, BLOCK), BF16)
    for p in range(N_HEADS // 2):
        kh = p // 2
        qp = q[:, p * BLOCK:(p + 1) * BLOCK]
        acc = None
        sink_terms = []
        for half in range(2):
            h = 2 * p + half
            qm = jnp.where(lo_half if half == 0 else jnp.logical_not(lo_half), qp, zero)
            s = lax.dot_general(qm, k2[kh], (((1,), (1,)), ((), ())),
                                preferred_element_type=F32) + bias_ref[0, h]
            sk = sink_ref[h] * LOG2E
            m = jnp.maximum(jnp.max(s, axis=-1, keepdims=True), sk)
            e = jnp.exp2(s - m).astype(BF16)
            d = jnp.dot(e, rv[kh, half], preferred_element_type=F32)
            acc = d if acc is None else acc + d
            sink_terms.append(jnp.exp2(sk - m))
        den = acc[:, BLOCK:] + jnp.where(lo_half, sink_terms[0], sink_terms[1])
        mix_ref[:, p * BLOCK:(p + 1) * BLOCK] = (acc[:, :BLOCK] / den).astype(BF16)

    ext_ref[0:8, :] = up_ref[...]
    ext_ref[8:8 + BLOCK, :] = u_ref[...]
    row = lax.broadcasted_iota(jnp.int32, (BLOCK, CONV_DIM), 0)
    first = jnp.where(is_tail, META_ROW0, -8)
    u1 = jnp.where(row >= first + 1, ext_ref[7:7 + BLOCK, :], 0.0)
    u2 = jnp.where(row >= first + 2, ext_ref[6:6 + BLOCK, :], 0.0)
    gb = z_ref[:, Q_DIM + 2 * KV_DIM:Z1_DIM]
    c = gb * (cw_ref[0:1, :] * u2 + cw_ref[1:2, :] * u1 + cw_ref[2:3, :] * u_ref[...])
    mix_ref[:, Q_DIM:] = c.astype(BF16)

    @pl.when(is_tail)
    def _():
        mix_ref[0:DEC_BATCH, :] = mixs_ref[...].astype(BF16)

    @pl.when(jnp.logical_and(i < N_XBLK, i % BLK_PER_SEQ == BLK_PER_SEQ - 1))
    def _():
        kp_ref[0] = z_ref[:, Q_DIM:Q_DIM + KV_DIM].T
        vp_ref[0] = z_ref[:, Q_DIM + KV_DIM:Q_DIM + 2 * KV_DIM].T
        cp_ref[0] = u_ref[BLOCK - (CONV_WIDTH - 1):BLOCK, :]


def _prev_blk(i):
    return jnp.where(i % BLK_PER_SEQ == 0, N_XBLK, i - 1)


def _bias_variant(i):
    return jnp.where(i == N_XBLK, 0, jnp.where(i % BLK_PER_SEQ == 0, 1, 2))


def _mixer(layer, sinks, z1, u, bias3, cw, mix_s, w_out_b, hin, g_post, g_next):
    rows8 = BLOCK // 8
    blk = lambda t: jnp.minimum(t, N_XBLK)
    prj = lambda t: jnp.maximum(t - 1, 0)
    seq_of = lambda t: jnp.minimum(blk(t) // BLK_PER_SEQ, BATCH - 1)
    return pl.pallas_call(
        functools.partial(_mixer_kernel, layer=layer),
        out_shape=(jax.ShapeDtypeStruct((M_ALL, D_MODEL), F32),
                   jax.ShapeDtypeStruct((M_ALL, D_MODEL), BF16),
                   jax.ShapeDtypeStruct((BATCH, KV_DIM, WINDOW), F32),
                   jax.ShapeDtypeStruct((BATCH, KV_DIM, WINDOW), F32),
                   jax.ShapeDtypeStruct((BATCH, CONV_WIDTH - 1, CONV_DIM), F32)),
        grid=(N_XBLK + 2,),
        in_specs=[pl.BlockSpec(memory_space=pltpu.SMEM),
                  pl.BlockSpec((BLOCK, Z1_DIM), lambda t: (blk(t), 0)),
                  pl.BlockSpec((BLOCK, 2 * KV_DIM),
                               lambda t: (_prev_blk(blk(t)), Q_DIM // (2 * KV_DIM))),
                  pl.BlockSpec((BLOCK, CONV_DIM), lambda t: (blk(t), 0)),
                  pl.BlockSpec((8, CONV_DIM), lambda t: (_prev_blk(blk(t)) * rows8 + rows8 - 1, 0)),
                  pl.BlockSpec((1, N_HEADS, BLOCK, 2 * BLOCK),
                               lambda t: (_bias_variant(blk(t)), 0, 0, 0)),
                  pl.BlockSpec((CONV_WIDTH, CONV_DIM), lambda t: (0, 0)),
                  pl.BlockSpec((DEC_BATCH, D_MODEL), lambda t: (0, 0)),
                  pl.BlockSpec((D_MODEL // 2, D_MODEL), lambda t: (0, 0), pipeline_mode=pl.Buffered(1)),
                  pl.BlockSpec((BLOCK, D_MODEL), lambda t: (prj(t), 0)),
                  pl.BlockSpec((DEPTH, D_MODEL), lambda t: (0, 0)),
                  pl.BlockSpec((DEPTH, D_MODEL), lambda t: (0, 0))],
        out_specs=(pl.BlockSpec((BLOCK, D_MODEL), lambda t: (prj(t), 0)),
                   pl.BlockSpec((BLOCK, D_MODEL), lambda t: (prj(t), 0)),
                   pl.BlockSpec((1, KV_DIM, WINDOW), lambda t: (seq_of(t), 0, 0)),
                   pl.BlockSpec((1, KV_DIM, WINDOW), lambda t: (seq_of(t), 0, 0)),
                   pl.BlockSpec((1, CONV_WIDTH - 1, CONV_DIM), lambda t: (seq_of(t), 0, 0))),
        scratch_shapes=[pltpu.VMEM((BLOCK + 8, CONV_DIM), F32),
                        pltpu.VMEM((BLOCK, D_MODEL), BF16),
                        pltpu.VMEM((BLOCK, D_MODEL), BF16)],
        compiler_params=_cparams(("arbitrary",)),
        name="mixer",
    )(sinks, z1, z1, u, u, bias3, cw, mix_s, w_out_b, hin, g_post, g_next)


S_CHUNK = 8
SH_ROWS = S_CHUNK * N_HEADS


def _sample_kernel(zs_ref, us_ref, kvt_ref, kc_ref, vc_ref, st_ref, cw_ref, sinkt_ref, tabt_ref,
                   brow_ref, fold_ref, unfold_ref, mixs_ref, kout_ref, vout_ref, cs_ref, *, layer):
    z = zs_ref[...]
    q = z[:, 0:Q_DIM] * SCALE
    qrep = jnp.concatenate([jnp.broadcast_to(q[s:s + 1, :], (N_HEADS, Q_DIM))
                            for s in range(S_CHUNK)], axis=0)
    row_head = lax.broadcasted_iota(jnp.int32, (SH_ROWS, Q_DIM), 0) % N_HEADS
    own_q = lax.broadcasted_iota(jnp.int32, (SH_ROWS, Q_DIM), 1) // HEAD_DIM == row_head
    qblk = jnp.dot(jnp.where(own_q, qrep, 0.0).astype(BF16), fold_ref[...],
                   preferred_element_type=F32)
    own_kv = (lax.broadcasted_iota(jnp.int32, (N_HEADS, KV_DIM), 1) // HEAD_DIM
              == lax.broadcasted_iota(jnp.int32, (N_HEADS, KV_DIM), 0) // GROUP)

    brow = brow_ref[...]
    tabt = tabt_ref[...]
    bias_t = jnp.zeros((N_HEADS, W_BUF), F32)
    for b in range(NUM_BUCKETS):
        bias_t = jnp.where(brow == b, tabt[:, b:b + 1], bias_t)
    bias0 = tabt[:, 0:1]
    sink = sinkt_ref[:, layer:layer + 1]
    last_key = lax.broadcasted_iota(jnp.int32, (KV_DIM, W_BUF), 1) == W_BUF - 1

    o_rows = []
    for s in range(S_CHUNK):
        rows = slice(s * KV_DIM, (s + 1) * KV_DIM)
        kt = kc_ref[rows, :]
        vt = vc_ref[rows, :]
        qb = qblk[s * N_HEADS:(s + 1) * N_HEADS, :]
        k_s = z[s:s + 1, Q_DIM:Q_DIM + KV_DIM]
        v_s = z[s:s + 1, Q_DIM + KV_DIM:Q_DIM + 2 * KV_DIM]
        sc = jnp.dot(qb.astype(BF16), kt.astype(BF16), preferred_element_type=F32) + bias_t
        sn = jnp.sum(qb * k_s, axis=-1, keepdims=True) + bias0
        m = jnp.maximum(jnp.maximum(jnp.max(sc, axis=-1, keepdims=True), sn), sink)
        e = jnp.exp(sc - m)
        en = jnp.exp(sn - m)
        den = jnp.sum(e, axis=-1, keepdims=True) + en + jnp.exp(sink - m)
        o = lax.dot_general((e / den).astype(BF16), vt.astype(BF16), (((1,), (1,)), ((), ())),
                            preferred_element_type=F32)
        o_rows.append(jnp.where(own_kv, o + (en / den) * v_s, 0.0))
        kout_ref[rows, :] = jnp.where(last_key, kvt_ref[0, 0:KV_DIM, s:s + 1],
                                      pltpu.roll(kt, W_BUF - 1, 1))
        vout_ref[rows, :] = jnp.where(last_key, kvt_ref[0, KV_DIM:2 * KV_DIM, s:s + 1],
                                      pltpu.roll(vt, W_BUF - 1, 1))

    a_all = jnp.dot(jnp.concatenate(o_rows, axis=0).astype(BF16), unfold_ref[...],
                    preferred_element_type=F32)
    a_all = jnp.where(own_q, a_all, 0.0)
    mixs_ref[:, 0:Q_DIM] = jnp.sum(a_all.reshape(S_CHUNK, N_HEADS, Q_DIM), axis=1)

    gb = z[:, Q_DIM + 2 * KV_DIM:Z1_DIM]
    u = us_ref[...]
    s0 = st_ref[:, 0:CONV_DIM]
    s1 = st_ref[:, CONV_DIM:]
    mixs_ref[:, Q_DIM:] = gb * (cw_ref[0:1, :] * s0 + cw_ref[1:2, :] * s1 + cw_ref[2:3, :] * u)
    cs_ref[:, 0:CONV_DIM] = s1
    cs_ref[:, CONV_DIM:] = u


def _sample_mixer(layer, z1, u, kvt, cache_kt, cache_vt, state2d, cw, sinkt, tabt, brow,
                  fold, unfold):
    n_chunks = DEC_BATCH // S_CHUNK
    row_blk0 = M_X // S_CHUNK
    full = lambda shape: pl.BlockSpec(shape, lambda c: (0,) * len(shape))
    return pl.pallas_call(
        functools.partial(_sample_kernel, layer=layer),
        out_shape=(jax.ShapeDtypeStruct((DEC_BATCH, D_MODEL), F32),
                   jax.ShapeDtypeStruct((DEC_BATCH * KV_DIM, W_BUF), F32),
                   jax.ShapeDtypeStruct((DEC_BATCH * KV_DIM, W_BUF), F32),
                   jax.ShapeDtypeStruct((DEC_BATCH, 2 * CONV_DIM), F32)),
        grid=(n_chunks,),
        in_specs=[pl.BlockSpec((S_CHUNK, Z1_DIM), lambda c: (row_blk0 + c, 0)),
                  pl.BlockSpec((S_CHUNK, CONV_DIM), lambda c: (row_blk0 + c, 0)),
                  pl.BlockSpec((1, 2 * KV_DIM, S_CHUNK), lambda c: (c, 0, 0)),
                  pl.BlockSpec((S_CHUNK * KV_DIM, W_BUF), lambda c: (layer * n_chunks + c, 0)),
                  pl.BlockSpec((S_CHUNK * KV_DIM, W_BUF), lambda c: (layer * n_chunks + c, 0)),
                  pl.BlockSpec((S_CHUNK, 2 * CONV_DIM), lambda c: (layer * n_chunks + c, 0)),
                  full((CONV_WIDTH, CONV_DIM)),
                  full((N_HEADS, DEPTH)),
                  full((N_HEADS, NUM_BUCKETS)),
                  full((1, W_BUF)),
                  full((Q_DIM, KV_DIM)),
                  full((KV_DIM, Q_DIM))],
        out_specs=(pl.BlockSpec((S_CHUNK, D_MODEL), lambda c: (c, 0)),
                   pl.BlockSpec((S_CHUNK * KV_DIM, W_BUF), lambda c: (c, 0)),
                   pl.BlockSpec((S_CHUNK * KV_DIM, W_BUF), lambda c: (c, 0)),
                   pl.BlockSpec((S_CHUNK, 2 * CONV_DIM), lambda c: (c, 0))),
        compiler_params=_cparams(("arbitrary",)),
        name="sample_mixer",
    )(z1, u, kvt, cache_kt, cache_vt, state2d, cw, sinkt, tabt, brow, fold, unfold)


def _t5_bucket(d):
    max_exact = NUM_BUCKETS // 2
    df = jnp.maximum(d, 1).astype(F32)
    large = max_exact + (jnp.log(df / max_exact) / math.log(MAX_DISTANCE / max_exact)
                         * (NUM_BUCKETS - max_exact)).astype(jnp.int32)
    large = jnp.minimum(large, NUM_BUCKETS - 1)
    return jnp.where(d < max_exact, d, large)


def _band_tables():
    i = np.arange(BLOCK)[:, None]
    j = np.arange(2 * BLOCK)[None, :]
    d = BLOCK + i - j
    band = (d >= 0) & (d <= WINDOW)
    valid = np.stack([band & (j >= BLOCK + META_ROW0), band & (j >= META_ROW0), band])
    bucket = _t5_bucket(jnp.asarray(np.maximum(d, 0), jnp.int32))
    return bucket, jnp.asarray(valid.astype(np.int32))


def _fold_tables():
    d = np.arange(HEAD_DIM)
    fold = np.zeros((Q_DIM, KV_DIM), np.float32)
    for h in range(N_HEADS):
        fold[h * HEAD_DIM + d, (h // GROUP) * HEAD_DIM + d] = 1.0
    return jnp.asarray(fold, BF16), jnp.asarray(fold.T, BF16)


def kernel(x_prompt, x_sample, cache_k, cache_v, state_conv, meta_tokens, rel_bias, w_in, conv_w,
           attn_sinks, w_out, norm_pre_mix, norm_post_mix, norm_pre_ffn, norm_post_ffn,
           w_gate, w_up, w_down):
    tail = jnp.concatenate([x_sample.reshape(DEC_BATCH, D_MODEL),
                            jnp.zeros((META_ROW0 - DEC_BATCH, D_MODEL), F32),
                            meta_tokens.astype(F32)], axis=0)

    bucket, valid = _band_tables()
    fold, unfold = _fold_tables()
    bias3 = _band_bias(rel_bias, bucket, valid)
    brow = _t5_bucket(jnp.asarray(W_BUF - np.arange(W_BUF), jnp.int32)).reshape(1, W_BUF)

    kv_lanes = lambda c: jnp.transpose(c, (0, 1, 3, 4, 2)).reshape(DEPTH * DEC_BATCH * KV_DIM, W_BUF)
    cache_kt = kv_lanes(cache_k)
    cache_vt = kv_lanes(cache_v)
    state2d = state_conv.reshape(DEPTH * DEC_BATCH, 2 * CONV_DIM)

    h, xn = _norm(x_prompt.reshape(M_X, D_MODEL), tail, norm_pre_mix)

    def kv_rows(t, n):
        return jnp.transpose(t.reshape(n, N_KV_HEADS, HEAD_DIM, WINDOW), (0, 3, 1, 2))

    kp, vp, cp, ks, vs, cs = [], [], [], [], [], []
    for l in range(DEPTH):
        z1, w_out_b = _matmul(xn, w_in, w_out, l, Z1_DIM)
        (u,) = _gated(xn, w_in, w_in, None, l, Z1_DIM, Z1_DIM + CONV_DIM, CONV_DIM,
                      False, F32, "conv_in")
        kvt = jnp.transpose(z1[M_X:M_X + DEC_BATCH, Q_DIM:Q_DIM + 2 * KV_DIM]
                            .reshape(DEC_BATCH // S_CHUNK, S_CHUNK, 2 * KV_DIM), (0, 2, 1))
        mix_s, k_new, v_new, c_new = _sample_mixer(
            l, z1, u, kvt, cache_kt, cache_vt, state2d, conv_w[l], attn_sinks.T, rel_bias.T,
            brow, fold, unfold)
        h, xn, kp_l, vp_l, cp_l = _mixer(l, attn_sinks[l], z1, u, bias3, conv_w[l], mix_s,
                                         w_out_b, h, norm_post_mix, norm_pre_ffn)
        act, w_down_b = _gated(xn, w_gate, w_up, w_down, l, 0, 0, D_FF, True, BF16, "ffn_up")
        if l + 1 < DEPTH:
            h, xn = _proj_norm(act, w_down_b, l, h, norm_post_ffn, norm_pre_mix, l + 1, "ffn_down")
        else:
            y2d, y_tail = _proj_last(act, w_down_b, l, h, norm_post_ffn)

        kp.append(kv_rows(kp_l, BATCH))
        vp.append(kv_rows(vp_l, BATCH))
        cp.append(cp_l)
        ks.append(kv_rows(k_new, DEC_BATCH))
        vs.append(kv_rows(v_new, DEC_BATCH))
        cs.append(c_new.reshape(DEC_BATCH, CONV_WIDTH - 1, CONV_DIM))

    y_prompt = y2d.reshape(BATCH, SEQ, D_MODEL)
    y_sample = y_tail[0:DEC_BATCH].reshape(DEC_BATCH, 1, D_MODEL)
    return (y_prompt, y_sample, jnp.stack(kp), jnp.stack(vp), jnp.stack(cp),
            jnp.stack(ks), jnp.stack(vs), jnp.stack(cs))
```

```python
import functools
import math

import numpy as np
import jax
import jax.numpy as jnp
from jax import lax
from jax.experimental import pallas as pl
from jax.experimental.pallas import tpu as pltpu

D_MODEL = 2048
BATCH = 4
SEQ = 2048
DEPTH = 4
DEC_BATCH = 32
HEAD_DIM = 64
N_HEADS = 16
N_KV_HEADS = 4
GROUP = 4
Q_DIM = 1024
KV_DIM = 256
CONV_DIM = 1024
CONV_WIDTH = 3
IN_DIM = 4608
WINDOW = 128
BLOCK = 128
NUM_BUCKETS = 32
MAX_DISTANCE = 128
N_META = 16
D_FF = 5632
RMS_EPS = 1e-6
SCALE = HEAD_DIM ** -0.5
W_BUF = 128

M_X = BATCH * SEQ
N_XBLK = M_X // BLOCK
BLK_PER_SEQ = SEQ // BLOCK
M_ALL = M_X + BLOCK
META_ROW0 = BLOCK - N_META
Z1_DIM = Q_DIM + 2 * KV_DIM + CONV_DIM
NEG = -1e30
LOG2E = math.log2(math.e)

TM_BIG = 1664
TM_EPI = 640
N_EPI = M_ALL // TM_EPI
EPI_LAST_X = M_X - (N_EPI - 1) * TM_EPI
assert (M_X - EPI_LAST_X) % EPI_LAST_X == 0 and EPI_LAST_X + BLOCK == TM_EPI
TM_PROJ = 416
RC_PROJ = 208
N_PROJ = M_ALL // TM_PROJ
PROJ_LAST_X = M_X - (N_PROJ - 1) * TM_PROJ
assert PROJ_LAST_X + BLOCK == TM_PROJ and PROJ_LAST_X % 8 == 0
TN = 512
TM_IN = 832
TN_IN = 1280
VMEM_LIMIT = 56 * 1024 * 1024

F32 = jnp.float32
BF16 = jnp.bfloat16


def _cparams(sem):
    return pltpu.CompilerParams(dimension_semantics=sem, vmem_limit_bytes=VMEM_LIMIT)


def _rms(x, g):
    return x * lax.rsqrt(jnp.mean(x * x, axis=-1, keepdims=True) + RMS_EPS) * g


def _norm_kernel(xa_ref, xb_ref, tail_ref, g_ref, xn_ref):
    last = pl.program_id(0) == N_EPI - 1
    h_last = jnp.concatenate([xb_ref[...], tail_ref[...]], axis=0)
    h = jnp.where(last, h_last, xa_ref[...])
    xn_ref[...] = _rms(h, g_ref[0:1, :]).astype(BF16)


def _norm(x2d, tail, g):
    return pl.pallas_call(
        _norm_kernel,
        out_shape=jax.ShapeDtypeStruct((M_ALL, D_MODEL), BF16),
        grid=(N_EPI,),
        in_specs=[pl.BlockSpec((TM_EPI, D_MODEL), lambda i: (jnp.minimum(i, N_EPI - 2), 0)),
                  pl.BlockSpec((EPI_LAST_X, D_MODEL), lambda i: ((M_X - EPI_LAST_X) // EPI_LAST_X, 0)),
                  pl.BlockSpec((BLOCK, D_MODEL), lambda i: (0, 0)),
                  pl.BlockSpec((DEPTH, D_MODEL), lambda i: (0, 0))],
        out_specs=pl.BlockSpec((TM_EPI, D_MODEL), lambda i: (i, 0)),
        compiler_params=_cparams(("arbitrary",)),
        name="norm0",
    )(x2d, x2d, tail, g)


RIDER_ROWS = 128


def _pack_bf16(w_f32):
    return pltpu.bitcast(w_f32.astype(BF16), jnp.uint32)


def _unpack_bf16(w_u32):
    return pltpu.bitcast(w_u32, BF16)


def _rider(wr, layer, n_outer, n_inner):
    n_chunks = wr.shape[1] // RIDER_ROWS
    assert n_chunks <= n_outer * n_inner
    chunk = lambda j, i: jnp.minimum(j * n_inner + i, n_chunks - 1)
    in_spec = pl.BlockSpec((None, RIDER_ROWS, D_MODEL), lambda j, i: (layer, chunk(j, i), 0))
    out_spec = pl.BlockSpec((RIDER_ROWS // 2, D_MODEL), lambda j, i: (chunk(j, i), 0))
    return in_spec, out_spec, jax.ShapeDtypeStruct((wr.shape[1] // 2, D_MODEL), jnp.uint32)


def _matmul_kernel(x_ref, w_ref, wr_ref, o_ref, wrb_ref, wbf_ref):
    @pl.when(pl.program_id(1) == 0)
    def _():
        wbf_ref[...] = w_ref[...].astype(BF16)

    o_ref[...] = jnp.dot(x_ref[...], wbf_ref[...], preferred_element_type=F32)
    wrb_ref[...] = _pack_bf16(wr_ref[...])


def _matmul(x, w, wr, layer, n_cols):
    k = x.shape[1]
    n_outer, n_inner = n_cols // TN_IN, M_ALL // TM_IN
    r_in, r_out, r_shape = _rider(wr, layer, n_outer, n_inner)
    return pl.pallas_call(
        _matmul_kernel,
        out_shape=(jax.ShapeDtypeStruct((M_ALL, n_cols), F32), r_shape),
        grid=(n_outer, n_inner),
        in_specs=[pl.BlockSpec((TM_IN, k), lambda j, i: (i, 0)),
                  pl.BlockSpec((None, k, TN_IN), lambda j, i: (layer, 0, j)),
                  r_in],
        out_specs=(pl.BlockSpec((TM_IN, TN_IN), lambda j, i: (i, j)), r_out),
        scratch_shapes=[pltpu.VMEM((k, TN_IN), BF16)],
        compiler_params=_cparams(("arbitrary", "arbitrary")),
        name="in_proj",
    )(x, w, wr)


def _gated_kernel(*refs, silu, has_rider):
    if has_rider:
        x_ref, wa_ref, wb_ref, wr_ref, o_ref, wrb_ref, wbf_ref = refs
    else:
        x_ref, wa_ref, wb_ref, o_ref, wbf_ref = refs

    @pl.when(pl.program_id(1) == 0)
    def _():
        wbf_ref[:, 0:TN] = wa_ref[...].astype(BF16)
        wbf_ref[:, TN:2 * TN] = wb_ref[...].astype(BF16)

    ab = jnp.dot(x_ref[...], wbf_ref[...], preferred_element_type=F32)
    a = ab[:, 0:TN]
    b = ab[:, TN:2 * TN]
    if silu:
        a = a * (1.0 / (1.0 + jnp.exp(-a)))
    o_ref[...] = (a * b).astype(o_ref.dtype)
    if has_rider:
        wrb_ref[...] = _pack_bf16(wr_ref[...])


def _gated(x, wa, wb, wr, layer, a_col0, b_col0, n_cols, silu, out_dtype, name):
    k = x.shape[1]
    a0 = a_col0 // TN
    b0 = b_col0 // TN
    n_outer, n_inner = n_cols // TN, M_ALL // TM_BIG
    in_specs = [pl.BlockSpec((TM_BIG, k), lambda j, i: (i, 0)),
                pl.BlockSpec((None, k, TN), lambda j, i: (layer, 0, a0 + j)),
                pl.BlockSpec((None, k, TN), lambda j, i: (layer, 0, b0 + j))]
    out_specs = [pl.BlockSpec((TM_BIG, TN), lambda j, i: (i, j))]
    out_shape = [jax.ShapeDtypeStruct((M_ALL, n_cols), out_dtype)]
    args = [x, wa, wb]
    if wr is not None:
        r_in, r_out, r_shape = _rider(wr, layer, n_outer, n_inner)
        in_specs.append(r_in)
        out_specs.append(r_out)
        out_shape.append(r_shape)
        args.append(wr)
    return pl.pallas_call(
        functools.partial(_gated_kernel, silu=silu, has_rider=wr is not None),
        out_shape=tuple(out_shape),
        grid=(n_outer, n_inner),
        in_specs=in_specs,
        out_specs=tuple(out_specs),
        scratch_shapes=[pltpu.VMEM((k, 2 * TN), BF16)],
        compiler_params=_cparams(("arbitrary", "arbitrary")),
        name=name,
    )(*args)


def _proj_norm_kernel(a_ref, w_ref, hin_ref, gpost_ref, gnext_ref, h_ref, xn_ref, *, l_post, l_next):
    g_post = gpost_ref[l_post:l_post + 1, :]
    g_next = gnext_ref[l_next:l_next + 1, :]
    for c in range(TM_PROJ // RC_PROJ):
        rows = slice(c * RC_PROJ, (c + 1) * RC_PROJ)
        y = jnp.dot(a_ref[rows, :], _unpack_bf16(w_ref[...]), preferred_element_type=F32)
        hn = hin_ref[rows, :] + _rms(y, g_post)
        h_ref[rows, :] = hn
        xn_ref[rows, :] = _rms(hn, g_next).astype(BF16)


def _proj_norm(a, w, layer, hin, g_post, g_next, l_next, name):
    k = a.shape[1]
    return pl.pallas_call(
        functools.partial(_proj_norm_kernel, l_post=layer, l_next=l_next),
        out_shape=(jax.ShapeDtypeStruct((M_ALL, D_MODEL), F32),
                   jax.ShapeDtypeStruct((M_ALL, D_MODEL), BF16)),
        grid=(M_ALL // TM_PROJ,),
        in_specs=[pl.BlockSpec((TM_PROJ, k), lambda i: (i, 0)),
                  pl.BlockSpec((k // 2, D_MODEL), lambda i: (0, 0), pipeline_mode=pl.Buffered(1)),
                  pl.BlockSpec((TM_PROJ, D_MODEL), lambda i: (i, 0)),
                  pl.BlockSpec((DEPTH, D_MODEL), lambda i: (0, 0)),
                  pl.BlockSpec((DEPTH, D_MODEL), lambda i: (0, 0))],
        out_specs=(pl.BlockSpec((TM_PROJ, D_MODEL), lambda i: (i, 0)),
                   pl.BlockSpec((TM_PROJ, D_MODEL), lambda i: (i, 0))),
        compiler_params=_cparams(("parallel",)),
        name=name,
    )(a, w, hin, g_post, g_next)


def _proj_last_kernel(a_ref, w_ref, hin_ref, gpost_ref, y_ref, tail_ref, *, l_post):
    g_post = gpost_ref[l_post:l_post + 1, :]
    for c in range(TM_PROJ // RC_PROJ):
        rows = slice(c * RC_PROJ, (c + 1) * RC_PROJ)
        y = jnp.dot(a_ref[rows, :], _unpack_bf16(w_ref[...]), preferred_element_type=F32)
        y_ref[rows, :] = hin_ref[rows, :] + _rms(y, g_post)

    @pl.when(pl.program_id(0) == N_PROJ - 1)
    def _():
        tail_ref[...] = y_ref[PROJ_LAST_X:TM_PROJ, :]


def _proj_last(a, w, layer, hin, g_post):
    k = a.shape[1]
    return pl.pallas_call(
        functools.partial(_proj_last_kernel, l_post=layer),
        out_shape=(jax.ShapeDtypeStruct((M_X, D_MODEL), F32),
                   jax.ShapeDtypeStruct((BLOCK, D_MODEL), F32)),
        grid=(N_PROJ,),
        in_specs=[pl.BlockSpec((TM_PROJ, k), lambda i: (i, 0)),
                  pl.BlockSpec((k // 2, D_MODEL), lambda i: (0, 0), pipeline_mode=pl.Buffered(1)),
                  pl.BlockSpec((TM_PROJ, D_MODEL), lambda i: (i, 0)),
                  pl.BlockSpec((DEPTH, D_MODEL), lambda i: (0, 0))],
        out_specs=(pl.BlockSpec((TM_PROJ, D_MODEL), lambda i: (i, 0)),
                   pl.BlockSpec((BLOCK, D_MODEL), lambda i: (0, 0))),
        compiler_params=_cparams(("arbitrary",)),
        name="ffn_down_last",
    )(a, w, hin, g_post)


def _band_bias_kernel(tab_ref, bucket_ref, valid_ref, o_ref):
    h = pl.program_id(0)
    bucket = bucket_ref[...]
    acc = jnp.zeros((BLOCK, 2 * BLOCK), F32)
    for b in range(NUM_BUCKETS):
        acc = jnp.where(bucket == b, tab_ref[b, h], acc)
    acc = acc * LOG2E
    for v in range(3):
        o_ref[v, 0] = jnp.where(valid_ref[v] != 0, acc, NEG)


def _band_bias(rel_bias, bucket, valid):
    return pl.pallas_call(
        _band_bias_kernel,
        out_shape=jax.ShapeDtypeStruct((3, N_HEADS, BLOCK, 2 * BLOCK), F32),
        grid=(N_HEADS,),
        in_specs=[pl.BlockSpec(memory_space=pltpu.SMEM),
                  pl.BlockSpec((BLOCK, 2 * BLOCK), lambda h: (0, 0)),
                  pl.BlockSpec((3, BLOCK, 2 * BLOCK), lambda h: (0, 0, 0))],
        out_specs=pl.BlockSpec((3, 1, BLOCK, 2 * BLOCK), lambda h: (0, h, 0, 0)),
        compiler_params=_cparams(("arbitrary",)),
        name="band_bias",
    )(rel_bias, bucket, valid)


def _mixer_kernel(sink_ref, z_ref, zp_ref, u_ref, up_ref, bias_ref, cw_ref, mixs_ref,
                  w_ref, hin_ref, hint_ref, gpost_ref, gnext_ref,
                  h_ref, xn_ref, kp_ref, vp_ref, cp_ref, ext_ref, mix_ref, lhs_ref, *, layer):
    t = pl.program_id(0)
    i = jnp.minimum(t, N_XBLK)
    is_tail = i == N_XBLK

    @pl.when(t == 0)
    def _():
        mix_ref[...] = jnp.zeros_like(mix_ref)

    lhs_ref[...] = mix_ref[...]
    y = jnp.dot(lhs_ref[...], _unpack_bf16(w_ref[...]), preferred_element_type=F32)
    hin = hin_ref[...]
    if layer == 0:
        hin = jnp.where(t - 1 == N_XBLK, hint_ref[...], hin)
    hn = hin + _rms(y, gpost_ref[layer:layer + 1, :])
    h_ref[...] = hn
    xn_ref[...] = _rms(hn, gnext_ref[layer:layer + 1, :]).astype(BF16)

    q = (z_ref[:, 0:Q_DIM] * (SCALE * LOG2E)).astype(BF16)
    kk = jnp.concatenate([zp_ref[:, 0:KV_DIM], z_ref[:, Q_DIM:Q_DIM + KV_DIM]], axis=0)
    vv = jnp.concatenate([zp_ref[:, KV_DIM:2 * KV_DIM],
                          z_ref[:, Q_DIM + KV_DIM:Q_DIM + 2 * KV_DIM]], axis=0)

    lo_kv = lax.broadcasted_iota(jnp.int32, (2 * BLOCK, BLOCK), 1) < HEAD_DIM
    one_lo = jnp.where(lo_kv, 1.0, 0.0).astype(BF16)
    one_hi = jnp.where(lo_kv, 0.0, 1.0).astype(BF16)
    k2, rv = {}, {}
    for col in range(KV_DIM // BLOCK):
        kc = kk[:, col * BLOCK:(col + 1) * BLOCK]
        vc = vv[:, col * BLOCK:(col + 1) * BLOCK]
        ks = pltpu.roll(kc, HEAD_DIM, 1)
        vs = pltpu.roll(vc, HEAD_DIM, 1)
        for in_hi in range(2):
            kh = 2 * col + in_hi
            own_k, oth_k = (ks, kc) if in_hi else (kc, ks)
            own_v, oth_v = (vs, vc) if in_hi else (vc, vs)
            k2[kh] = jnp.where(lo_kv, own_k, oth_k).astype(BF16)
            rv[kh, 0] = jnp.concatenate([jnp.where(lo_kv, own_v, 0.0).astype(BF16), one_lo], axis=1)
            rv[kh, 1] = jnp.concatenate([jnp.where(lo_kv, 0.0, oth_v).astype(BF16), one_hi], axis=1)

    lane = lax.broadcasted_iota(jnp.int32, (BLOCK, BLOCK), 1)
    lo_half = lane < HEAD_DIM
    zero = jnp.zeros((BLOCK, BLOCK), BF16)
    for p in range(N_HEADS // 2):
        kh = p // 2
        qp = q[:, p * BLOCK:(p + 1) * BLOCK]
        acc = None
        sink_terms = []
        for half in range(2):
            h = 2 * p + half
            qm = jnp.where(lo_half if half == 0 else jnp.logical_not(lo_half), qp, zero)
            s = lax.dot_general(qm, k2[kh], (((1,), (1,)), ((), ())),
                                preferred_element_type=F32) + bias_ref[0, h]
            sk = sink_ref[h] * LOG2E
            m = jnp.maximum(jnp.max(s, axis=-1, keepdims=True), sk)
            e = jnp.exp2(s - m).astype(BF16)
            d = jnp.dot(e, rv[kh, half], preferred_element_type=F32)
            acc = d if acc is None else acc + d
            sink_terms.append(jnp.exp2(sk - m))
        den = acc[:, BLOCK:] + jnp.where(lo_half, sink_terms[0], sink_terms[1])
        mix_ref[:, p * BLOCK:(p + 1) * BLOCK] = (acc[:, :BLOCK] / den).astype(BF16)

    ext_ref[0:8, :] = up_ref[...]
    ext_ref[8:8 + BLOCK, :] = u_ref[...]
    row = lax.broadcasted_iota(jnp.int32, (BLOCK, CONV_DIM), 0)
    first = jnp.where(is_tail, META_ROW0, -8)
    u1 = jnp.where(row >= first + 1, ext_ref[7:7 + BLOCK, :], 0.0)
    u2 = jnp.where(row >= first + 2, ext_ref[6:6 + BLOCK, :], 0.0)
    gb = z_ref[:, Q_DIM + 2 * KV_DIM:Z1_DIM]
    c = gb * (cw_ref[0:1, :] * u2 + cw_ref[1:2, :] * u1 + cw_ref[2:3, :] * u_ref[...])
    mix_ref[:, Q_DIM:] = c.astype(BF16)

    @pl.when(is_tail)
    def _():
        mix_ref[0:DEC_BATCH, :] = mixs_ref[...].astype(BF16)

    @pl.when(jnp.logical_and(i < N_XBLK, i % BLK_PER_SEQ == BLK_PER_SEQ - 1))
    def _():
        kp_ref[0] = z_ref[:, Q_DIM:Q_DIM + KV_DIM].T
        vp_ref[0] = z_ref[:, Q_DIM + KV_DIM:Q_DIM + 2 * KV_DIM].T
        cp_ref[0] = u_ref[BLOCK - (CONV_WIDTH - 1):BLOCK, :]


def _prev_blk(i):
    return jnp.where(i % BLK_PER_SEQ == 0, N_XBLK, i - 1)


def _bias_variant(i):
    return jnp.where(i == N_XBLK, 0, jnp.where(i % BLK_PER_SEQ == 0, 1, 2))


def _mixer(layer, sinks, z1, u, bias3, cw, mix_s, w_out_b, hin, hin_tail, g_post, g_next):
    rows8 = BLOCK // 8
    blk = lambda t: jnp.minimum(t, N_XBLK)
    prj = lambda t: jnp.maximum(t - 1, 0)
    seq_of = lambda t: jnp.minimum(blk(t) // BLK_PER_SEQ, BATCH - 1)
    return pl.pallas_call(
        functools.partial(_mixer_kernel, layer=layer),
        out_shape=(jax.ShapeDtypeStruct((M_ALL, D_MODEL), F32),
                   jax.ShapeDtypeStruct((M_ALL, D_MODEL), BF16),
                   jax.ShapeDtypeStruct((BATCH, KV_DIM, WINDOW), F32),
                   jax.ShapeDtypeStruct((BATCH, KV_DIM, WINDOW), F32),
                   jax.ShapeDtypeStruct((BATCH, CONV_WIDTH - 1, CONV_DIM), F32)),
        grid=(N_XBLK + 2,),
        in_specs=[pl.BlockSpec(memory_space=pltpu.SMEM),
                  pl.BlockSpec((BLOCK, Z1_DIM), lambda t: (blk(t), 0)),
                  pl.BlockSpec((BLOCK, 2 * KV_DIM),
                               lambda t: (_prev_blk(blk(t)), Q_DIM // (2 * KV_DIM))),
                  pl.BlockSpec((BLOCK, CONV_DIM), lambda t: (blk(t), 0)),
                  pl.BlockSpec((8, CONV_DIM), lambda t: (_prev_blk(blk(t)) * rows8 + rows8 - 1, 0)),
                  pl.BlockSpec((1, N_HEADS, BLOCK, 2 * BLOCK),
                               lambda t: (_bias_variant(blk(t)), 0, 0, 0)),
                  pl.BlockSpec((CONV_WIDTH, CONV_DIM), lambda t: (0, 0)),
                  pl.BlockSpec((DEC_BATCH, D_MODEL), lambda t: (0, 0)),
                  pl.BlockSpec((D_MODEL // 2, D_MODEL), lambda t: (0, 0), pipeline_mode=pl.Buffered(1)),
                  pl.BlockSpec((BLOCK, D_MODEL), lambda t: (jnp.minimum(prj(t), hin.shape[0] // BLOCK - 1), 0)),
                  pl.BlockSpec((BLOCK, D_MODEL), lambda t: (hin_tail.shape[0] // BLOCK - 1, 0)),
                  pl.BlockSpec((DEPTH, D_MODEL), lambda t: (0, 0)),
                  pl.BlockSpec((DEPTH, D_MODEL), lambda t: (0, 0))],
        out_specs=(pl.BlockSpec((BLOCK, D_MODEL), lambda t: (prj(t), 0)),
                   pl.BlockSpec((BLOCK, D_MODEL), lambda t: (prj(t), 0)),
                   pl.BlockSpec((1, KV_DIM, WINDOW), lambda t: (seq_of(t), 0, 0)),
                   pl.BlockSpec((1, KV_DIM, WINDOW), lambda t: (seq_of(t), 0, 0)),
                   pl.BlockSpec((1, CONV_WIDTH - 1, CONV_DIM), lambda t: (seq_of(t), 0, 0))),
        scratch_shapes=[pltpu.VMEM((BLOCK + 8, CONV_DIM), F32),
                        pltpu.VMEM((BLOCK, D_MODEL), BF16),
                        pltpu.VMEM((BLOCK, D_MODEL), BF16)],
        compiler_params=_cparams(("arbitrary",)),
        name="mixer",
    )(sinks, z1, z1, u, u, bias3, cw, mix_s, w_out_b, hin, hin_tail, g_post, g_next)


S_CHUNK = 8
SH_ROWS = S_CHUNK * N_HEADS


def _sample_kernel(zs_ref, us_ref, kvt_ref, kc_ref, vc_ref, st_ref, cw_ref, sinkt_ref, tabt_ref,
                   brow_ref, fold_ref, unfold_ref, kall_ref, vall_ref,
                   mixs_ref, kout_ref, vout_ref, cs_ref, *, layer):
    del kall_ref, vall_ref
    z = zs_ref[...]
    q = z[:, 0:Q_DIM] * SCALE
    qrep = jnp.concatenate([jnp.broadcast_to(q[s:s + 1, :], (N_HEADS, Q_DIM))
                            for s in range(S_CHUNK)], axis=0)
    row_head = lax.broadcasted_iota(jnp.int32, (SH_ROWS, Q_DIM), 0) % N_HEADS
    own_q = lax.broadcasted_iota(jnp.int32, (SH_ROWS, Q_DIM), 1) // HEAD_DIM == row_head
    qblk = jnp.dot(jnp.where(own_q, qrep, 0.0).astype(BF16), fold_ref[...],
                   preferred_element_type=F32)
    own_kv = (lax.broadcasted_iota(jnp.int32, (N_HEADS, KV_DIM), 1) // HEAD_DIM
              == lax.broadcasted_iota(jnp.int32, (N_HEADS, KV_DIM), 0) // GROUP)

    brow = brow_ref[...]
    tabt = tabt_ref[...]
    bias_t = jnp.zeros((N_HEADS, W_BUF), F32)
    for b in range(NUM_BUCKETS):
        bias_t = jnp.where(brow == b, tabt[:, b:b + 1], bias_t)
    bias0 = tabt[:, 0:1]
    sink = sinkt_ref[:, layer:layer + 1]
    last_key = lax.broadcasted_iota(jnp.int32, (KV_DIM, W_BUF), 1) == W_BUF - 1

    o_rows = []
    for s in range(S_CHUNK):
        rows = slice(s * KV_DIM, (s + 1) * KV_DIM)
        kt = kc_ref[rows, :]
        vt = vc_ref[rows, :]
        qb = qblk[s * N_HEADS:(s + 1) * N_HEADS, :]
        k_s = z[s:s + 1, Q_DIM:Q_DIM + KV_DIM]
        v_s = z[s:s + 1, Q_DIM + KV_DIM:Q_DIM + 2 * KV_DIM]
        sc = jnp.dot(qb.astype(BF16), kt.astype(BF16), preferred_element_type=F32) + bias_t
        sn = jnp.sum(qb * k_s, axis=-1, keepdims=True) + bias0
        m = jnp.maximum(jnp.maximum(jnp.max(sc, axis=-1, keepdims=True), sn), sink)
        e = jnp.exp(sc - m)
        en = jnp.exp(sn - m)
        den = jnp.sum(e, axis=-1, keepdims=True) + en + jnp.exp(sink - m)
        o = lax.dot_general((e / den).astype(BF16), vt.astype(BF16), (((1,), (1,)), ((), ())),
                            preferred_element_type=F32)
        o_rows.append(jnp.where(own_kv, o + (en / den) * v_s, 0.0))
        kout_ref[rows, :] = jnp.where(last_key, kvt_ref[0, 0:KV_DIM, s:s + 1],
                                      pltpu.roll(kt, W_BUF - 1, 1))
        vout_ref[rows, :] = jnp.where(last_key, kvt_ref[0, KV_DIM:2 * KV_DIM, s:s + 1],
                                      pltpu.roll(vt, W_BUF - 1, 1))

    a_all = jnp.dot(jnp.concatenate(o_rows, axis=0).astype(BF16), unfold_ref[...],
                    preferred_element_type=F32)
    a_all = jnp.where(own_q, a_all, 0.0)
    mixs_ref[:, 0:Q_DIM] = jnp.sum(a_all.reshape(S_CHUNK, N_HEADS, Q_DIM), axis=1)

    gb = z[:, Q_DIM + 2 * KV_DIM:Z1_DIM]
    u = us_ref[...]
    s0 = st_ref[:, 0:CONV_DIM]
    s1 = st_ref[:, CONV_DIM:]
    mixs_ref[:, Q_DIM:] = gb * (cw_ref[0:1, :] * s0 + cw_ref[1:2, :] * s1 + cw_ref[2:3, :] * u)
    cs_ref[:, 0:CONV_DIM] = s1
    cs_ref[:, CONV_DIM:] = u


def _sample_mixer(layer, z1, u, kvt, cache_kt, cache_vt, state2d, cw, sinkt, tabt, brow,
                  fold, unfold, k_all, v_all):
    n_chunks = DEC_BATCH // S_CHUNK
    row_blk0 = M_X // S_CHUNK
    full = lambda shape: pl.BlockSpec(shape, lambda c: (0,) * len(shape))
    n_in = 14
    return pl.pallas_call(
        functools.partial(_sample_kernel, layer=layer),
        out_shape=(jax.ShapeDtypeStruct((DEC_BATCH, D_MODEL), F32),
                   jax.ShapeDtypeStruct(k_all.shape, F32),
                   jax.ShapeDtypeStruct(v_all.shape, F32),
                   jax.ShapeDtypeStruct((DEC_BATCH, 2 * CONV_DIM), F32)),
        input_output_aliases={n_in - 2: 1, n_in - 1: 2} if layer > 0 else {},
        grid=(n_chunks,),
        in_specs=[pl.BlockSpec((S_CHUNK, Z1_DIM), lambda c: (row_blk0 + c, 0)),
                  pl.BlockSpec((S_CHUNK, CONV_DIM), lambda c: (row_blk0 + c, 0)),
                  pl.BlockSpec((1, 2 * KV_DIM, S_CHUNK), lambda c: (c, 0, 0)),
                  pl.BlockSpec((S_CHUNK * KV_DIM, W_BUF), lambda c: (layer * n_chunks + c, 0)),
                  pl.BlockSpec((S_CHUNK * KV_DIM, W_BUF), lambda c: (layer * n_chunks + c, 0)),
                  pl.BlockSpec((S_CHUNK, 2 * CONV_DIM), lambda c: (layer * n_chunks + c, 0)),
                  full((CONV_WIDTH, CONV_DIM)),
                  full((N_HEADS, DEPTH)),
                  full((N_HEADS, NUM_BUCKETS)),
                  full((1, W_BUF)),
                  full((Q_DIM, KV_DIM)),
                  full((KV_DIM, Q_DIM)),
                  pl.BlockSpec(memory_space=pl.ANY),
                  pl.BlockSpec(memory_space=pl.ANY)],
        out_specs=(pl.BlockSpec((S_CHUNK, D_MODEL), lambda c: (c, 0)),
                   pl.BlockSpec((S_CHUNK * KV_DIM, W_BUF), lambda c: (layer * n_chunks + c, 0)),
                   pl.BlockSpec((S_CHUNK * KV_DIM, W_BUF), lambda c: (layer * n_chunks + c, 0)),
                   pl.BlockSpec((S_CHUNK, 2 * CONV_DIM), lambda c: (c, 0))),
        compiler_params=_cparams(("arbitrary",)),
        name="sample_mixer",
    )(z1, u, kvt, cache_kt, cache_vt, state2d, cw, sinkt, tabt, brow, fold, unfold, k_all, v_all)


def _t5_bucket(d):
    max_exact = NUM_BUCKETS // 2
    df = jnp.maximum(d, 1).astype(F32)
    large = max_exact + (jnp.log(df / max_exact) / math.log(MAX_DISTANCE / max_exact)
                         * (NUM_BUCKETS - max_exact)).astype(jnp.int32)
    large = jnp.minimum(large, NUM_BUCKETS - 1)
    return jnp.where(d < max_exact, d, large)


def _band_tables():
    i = np.arange(BLOCK)[:, None]
    j = np.arange(2 * BLOCK)[None, :]
    d = BLOCK + i - j
    band = (d >= 0) & (d <= WINDOW)
    valid = np.stack([band & (j >= BLOCK + META_ROW0), band & (j >= META_ROW0), band])
    bucket = _t5_bucket(jnp.asarray(np.maximum(d, 0), jnp.int32))
    return bucket, jnp.asarray(valid.astype(np.int32))


def _fold_tables():
    d = np.arange(HEAD_DIM)
    fold = np.zeros((Q_DIM, KV_DIM), np.float32)
    for h in range(N_HEADS):
        fold[h * HEAD_DIM + d, (h // GROUP) * HEAD_DIM + d] = 1.0
    return jnp.asarray(fold, BF16), jnp.asarray(fold.T, BF16)


def kernel(x_prompt, x_sample, cache_k, cache_v, state_conv, meta_tokens, rel_bias, w_in, conv_w,
           attn_sinks, w_out, norm_pre_mix, norm_post_mix, norm_pre_ffn, norm_post_ffn,
           w_gate, w_up, w_down):
    tail = jnp.concatenate([x_sample.reshape(DEC_BATCH, D_MODEL),
                            jnp.zeros((META_ROW0 - DEC_BATCH, D_MODEL), F32),
                            meta_tokens.astype(F32)], axis=0)

    bucket, valid = _band_tables()
    fold, unfold = _fold_tables()
    bias3 = _band_bias(rel_bias, bucket, valid)
    brow = _t5_bucket(jnp.asarray(W_BUF - np.arange(W_BUF), jnp.int32)).reshape(1, W_BUF)

    kv_lanes = lambda c: jnp.transpose(c, (0, 1, 3, 4, 2)).reshape(DEPTH * DEC_BATCH * KV_DIM, W_BUF)
    cache_kt = kv_lanes(cache_k)
    cache_vt = kv_lanes(cache_v)
    state2d = state_conv.reshape(DEPTH * DEC_BATCH, 2 * CONV_DIM)

    x2d = x_prompt.reshape(M_X, D_MODEL)
    xn = _norm(x2d, tail, norm_pre_mix)
    h, h_tail = x2d, tail

    def kv_rows(t, n):
        return jnp.transpose(t.reshape(n, N_KV_HEADS, HEAD_DIM, WINDOW), (0, 3, 1, 2))

    k_all, v_all = cache_kt, cache_vt
    kp, vp, cp, cs = [], [], [], []
    for l in range(DEPTH):
        z1, w_out_b = _matmul(xn, w_in, w_out, l, Z1_DIM)
        (u,) = _gated(xn, w_in, w_in, None, l, Z1_DIM, Z1_DIM + CONV_DIM, CONV_DIM,
                      False, F32, "conv_in")
        kvt = jnp.transpose(z1[M_X:M_X + DEC_BATCH, Q_DIM:Q_DIM + 2 * KV_DIM]
                            .reshape(DEC_BATCH // S_CHUNK, S_CHUNK, 2 * KV_DIM), (0, 2, 1))
        mix_s, k_all, v_all, c_new = _sample_mixer(
            l, z1, u, kvt, cache_kt, cache_vt, state2d, conv_w[l], attn_sinks.T, rel_bias.T,
            brow, fold, unfold, k_all, v_all)
        h, xn, kp_l, vp_l, cp_l = _mixer(l, attn_sinks[l], z1, u, bias3, conv_w[l], mix_s,
                                         w_out_b, h, h_tail, norm_post_mix, norm_pre_ffn)
        act, w_down_b = _gated(xn, w_gate, w_up, w_down, l, 0, 0, D_FF, True, BF16, "ffn_up")
        if l + 1 < DEPTH:
            h, xn = _proj_norm(act, w_down_b, l, h, norm_post_ffn, norm_pre_mix, l + 1, "ffn_down")
            h_tail = h
        else:
            y2d, y_tail = _proj_last(act, w_down_b, l, h, norm_post_ffn)

        kp.append(kv_rows(kp_l, BATCH))
        vp.append(kv_rows(vp_l, BATCH))
        cp.append(cp_l)
        cs.append(c_new.reshape(DEC_BATCH, CONV_WIDTH - 1, CONV_DIM))

    y_prompt = y2d.reshape(BATCH, SEQ, D_MODEL)
    y_sample = y_tail[0:DEC_BATCH].reshape(DEC_BATCH, 1, D_MODEL)
    k_sample = kv_rows(k_all, DEPTH * DEC_BATCH).reshape(DEPTH, DEC_BATCH, W_BUF, N_KV_HEADS, HEAD_DIM)
    v_sample = kv_rows(v_all, DEPTH * DEC_BATCH).reshape(DEPTH, DEC_BATCH, W_BUF, N_KV_HEADS, HEAD_DIM)
    return (y_prompt, y_sample, jnp.stack(kp), jnp.stack(vp), jnp.stack(cp),
            k_sample, v_sample, jnp.stack(cs))
```

```python
import functools
import math

import numpy as np
import jax
import jax.numpy as jnp
from jax import lax
from jax.experimental import pallas as pl
from jax.experimental.pallas import tpu as pltpu

D_MODEL = 2048
BATCH = 4
SEQ = 2048
DEPTH = 4
DEC_BATCH = 32
HEAD_DIM = 64
N_HEADS = 16
N_KV_HEADS = 4
GROUP = 4
Q_DIM = 1024
KV_DIM = 256
CONV_DIM = 1024
CONV_WIDTH = 3
IN_DIM = 4608
WINDOW = 128
BLOCK = 128
NUM_BUCKETS = 32
MAX_DISTANCE = 128
N_META = 16
D_FF = 5632
RMS_EPS = 1e-6
SCALE = HEAD_DIM ** -0.5
W_BUF = 128

M_X = BATCH * SEQ
N_XBLK = M_X // BLOCK
BLK_PER_SEQ = SEQ // BLOCK
M_ALL = M_X + BLOCK
META_ROW0 = BLOCK - N_META
Z1_DIM = Q_DIM + 2 * KV_DIM + CONV_DIM
NEG = -1e30
LOG2E = math.log2(math.e)

TM_BIG = 1664
TM_EPI = 640
N_EPI = M_ALL // TM_EPI
EPI_LAST_X = M_X - (N_EPI - 1) * TM_EPI
assert (M_X - EPI_LAST_X) % EPI_LAST_X == 0 and EPI_LAST_X + BLOCK == TM_EPI
TM_PROJ = 416
RC_PROJ = 208
N_PROJ = M_ALL // TM_PROJ
PROJ_LAST_X = M_X - (N_PROJ - 1) * TM_PROJ
assert PROJ_LAST_X + BLOCK == TM_PROJ and PROJ_LAST_X % 8 == 0
TN = 512
TM_IN = 832
TN_IN = 1280
VMEM_LIMIT = 56 * 1024 * 1024

F32 = jnp.float32
BF16 = jnp.bfloat16


def _cparams(sem):
    return pltpu.CompilerParams(dimension_semantics=sem, vmem_limit_bytes=VMEM_LIMIT)


def _rms(x, g):
    return x * lax.rsqrt(jnp.mean(x * x, axis=-1, keepdims=True) + RMS_EPS) * g


def _norm_kernel(xa_ref, xb_ref, tail_ref, g_ref, xn_ref):
    last = pl.program_id(0) == N_EPI - 1
    h_last = jnp.concatenate([xb_ref[...], tail_ref[...]], axis=0)
    h = jnp.where(last, h_last, xa_ref[...])
    xn_ref[...] = _rms(h, g_ref[0:1, :]).astype(BF16)


def _norm(x2d, tail, g):
    return pl.pallas_call(
        _norm_kernel,
        out_shape=jax.ShapeDtypeStruct((M_ALL, D_MODEL), BF16),
        grid=(N_EPI,),
        in_specs=[pl.BlockSpec((TM_EPI, D_MODEL), lambda i: (jnp.minimum(i, N_EPI - 2), 0)),
                  pl.BlockSpec((EPI_LAST_X, D_MODEL), lambda i: ((M_X - EPI_LAST_X) // EPI_LAST_X, 0)),
                  pl.BlockSpec((BLOCK, D_MODEL), lambda i: (0, 0)),
                  pl.BlockSpec((DEPTH, D_MODEL), lambda i: (0, 0))],
        out_specs=pl.BlockSpec((TM_EPI, D_MODEL), lambda i: (i, 0)),
        compiler_params=_cparams(("arbitrary",)),
        name="norm0",
    )(x2d, x2d, tail, g)


RIDER_ROWS = 128


def _pack_bf16(w_f32):
    return pltpu.bitcast(w_f32.astype(BF16), jnp.uint32)


def _unpack_bf16(w_u32):
    return pltpu.bitcast(w_u32, BF16)


def _rider(wr, layer, n_outer, n_inner):
    n_chunks = wr.shape[1] // RIDER_ROWS
    assert n_chunks <= n_outer * n_inner
    chunk = lambda j, i: jnp.minimum(j * n_inner + i, n_chunks - 1)
    in_spec = pl.BlockSpec((None, RIDER_ROWS, D_MODEL), lambda j, i: (layer, chunk(j, i), 0))
    out_spec = pl.BlockSpec((RIDER_ROWS // 2, D_MODEL), lambda j, i: (chunk(j, i), 0))
    return in_spec, out_spec, jax.ShapeDtypeStruct((wr.shape[1] // 2, D_MODEL), jnp.uint32)


def _matmul_kernel(x_ref, w_ref, wr_ref, o_ref, wrb_ref, wbf_ref):
    @pl.when(pl.program_id(1) == 0)
    def _():
        wbf_ref[...] = w_ref[...].astype(BF16)

    o_ref[...] = jnp.dot(x_ref[...], wbf_ref[...], preferred_element_type=F32)
    wrb_ref[...] = _pack_bf16(wr_ref[...])


def _matmul(x, w, wr, layer, n_cols):
    k = x.shape[1]
    n_outer, n_inner = n_cols // TN_IN, M_ALL // TM_IN
    r_in, r_out, r_shape = _rider(wr, layer, n_outer, n_inner)
    return pl.pallas_call(
        _matmul_kernel,
        out_shape=(jax.ShapeDtypeStruct((M_ALL, n_cols), F32), r_shape),
        grid=(n_outer, n_inner),
        in_specs=[pl.BlockSpec((TM_IN, k), lambda j, i: (i, 0)),
                  pl.BlockSpec((None, k, TN_IN), lambda j, i: (layer, 0, j)),
                  r_in],
        out_specs=(pl.BlockSpec((TM_IN, TN_IN), lambda j, i: (i, j)), r_out),
        scratch_shapes=[pltpu.VMEM((k, TN_IN), BF16)],
        compiler_params=_cparams(("arbitrary", "arbitrary")),
        name="in_proj",
    )(x, w, wr)


def _gated_kernel(*refs, silu, has_rider):
    if has_rider:
        x_ref, wa_ref, wb_ref, wr_ref, o_ref, wrb_ref, wbf_ref = refs
    else:
        x_ref, wa_ref, wb_ref, o_ref, wbf_ref = refs

    @pl.when(pl.program_id(1) == 0)
    def _():
        wbf_ref[:, 0:TN] = wa_ref[...].astype(BF16)
        wbf_ref[:, TN:2 * TN] = wb_ref[...].astype(BF16)

    ab = jnp.dot(x_ref[...], wbf_ref[...], preferred_element_type=F32)
    a = ab[:, 0:TN]
    b = ab[:, TN:2 * TN]
    if silu:
        a = a * (1.0 / (1.0 + jnp.exp(-a)))
    o_ref[...] = (a * b).astype(o_ref.dtype)
    if has_rider:
        wrb_ref[...] = _pack_bf16(wr_ref[...])


def _gated(x, wa, wb, wr, layer, a_col0, b_col0, n_cols, silu, out_dtype, name):
    k = x.shape[1]
    a0 = a_col0 // TN
    b0 = b_col0 // TN
    n_outer, n_inner = n_cols // TN, M_ALL // TM_BIG
    in_specs = [pl.BlockSpec((TM_BIG, k), lambda j, i: (i, 0)),
                pl.BlockSpec((None, k, TN), lambda j, i: (layer, 0, a0 + j)),
                pl.BlockSpec((None, k, TN), lambda j, i: (layer, 0, b0 + j))]
    out_specs = [pl.BlockSpec((TM_BIG, TN), lambda j, i: (i, j))]
    out_shape = [jax.ShapeDtypeStruct((M_ALL, n_cols), out_dtype)]
    args = [x, wa, wb]
    if wr is not None:
        r_in, r_out, r_shape = _rider(wr, layer, n_outer, n_inner)
        in_specs.append(r_in)
        out_specs.append(r_out)
        out_shape.append(r_shape)
        args.append(wr)
    return pl.pallas_call(
        functools.partial(_gated_kernel, silu=silu, has_rider=wr is not None),
        out_shape=tuple(out_shape),
        grid=(n_outer, n_inner),
        in_specs=in_specs,
        out_specs=tuple(out_specs),
        scratch_shapes=[pltpu.VMEM((k, 2 * TN), BF16)],
        compiler_params=_cparams(("arbitrary", "arbitrary")),
        name=name,
    )(*args)


def _proj_norm_kernel(a_ref, w_ref, hin_ref, gpost_ref, gnext_ref, h_ref, xn_ref, *, l_post, l_next):
    g_post = gpost_ref[l_post:l_post + 1, :]
    g_next = gnext_ref[l_next:l_next + 1, :]
    for c in range(TM_PROJ // RC_PROJ):
        rows = slice(c * RC_PROJ, (c + 1) * RC_PROJ)
        y = jnp.dot(a_ref[rows, :], _unpack_bf16(w_ref[...]), preferred_element_type=F32)
        hn = hin_ref[rows, :] + _rms(y, g_post)
        h_ref[rows, :] = hn
        xn_ref[rows, :] = _rms(hn, g_next).astype(BF16)


def _proj_norm(a, w, layer, hin, g_post, g_next, l_next, name):
    k = a.shape[1]
    return pl.pallas_call(
        functools.partial(_proj_norm_kernel, l_post=layer, l_next=l_next),
        out_shape=(jax.ShapeDtypeStruct((M_ALL, D_MODEL), F32),
                   jax.ShapeDtypeStruct((M_ALL, D_MODEL), BF16)),
        grid=(M_ALL // TM_PROJ,),
        in_specs=[pl.BlockSpec((TM_PROJ, k), lambda i: (i, 0)),
                  pl.BlockSpec((k // 2, D_MODEL), lambda i: (0, 0), pipeline_mode=pl.Buffered(1)),
                  pl.BlockSpec((TM_PROJ, D_MODEL), lambda i: (i, 0)),
                  pl.BlockSpec((DEPTH, D_MODEL), lambda i: (0, 0)),
                  pl.BlockSpec((DEPTH, D_MODEL), lambda i: (0, 0))],
        out_specs=(pl.BlockSpec((TM_PROJ, D_MODEL), lambda i: (i, 0)),
                   pl.BlockSpec((TM_PROJ, D_MODEL), lambda i: (i, 0))),
        compiler_params=_cparams(("parallel",)),
        name=name,
    )(a, w, hin, g_post, g_next)


def _proj_last_kernel(a_ref, w_ref, hin_ref, gpost_ref, y_ref, tail_ref, *, l_post):
    g_post = gpost_ref[l_post:l_post + 1, :]
    for c in range(TM_PROJ // RC_PROJ):
        rows = slice(c * RC_PROJ, (c + 1) * RC_PROJ)
        y = jnp.dot(a_ref[rows, :], _unpack_bf16(w_ref[...]), preferred_element_type=F32)
        y_ref[rows, :] = hin_ref[rows, :] + _rms(y, g_post)

    @pl.when(pl.program_id(0) == N_PROJ - 1)
    def _():
        tail_ref[...] = y_ref[PROJ_LAST_X:TM_PROJ, :]


def _proj_last(a, w, layer, hin, g_post):
    k = a.shape[1]
    return pl.pallas_call(
        functools.partial(_proj_last_kernel, l_post=layer),
        out_shape=(jax.ShapeDtypeStruct((M_X, D_MODEL), F32),
                   jax.ShapeDtypeStruct((BLOCK, D_MODEL), F32)),
        grid=(N_PROJ,),
        in_specs=[pl.BlockSpec((TM_PROJ, k), lambda i: (i, 0)),
                  pl.BlockSpec((k // 2, D_MODEL), lambda i: (0, 0), pipeline_mode=pl.Buffered(1)),
                  pl.BlockSpec((TM_PROJ, D_MODEL), lambda i: (i, 0)),
                  pl.BlockSpec((DEPTH, D_MODEL), lambda i: (0, 0))],
        out_specs=(pl.BlockSpec((TM_PROJ, D_MODEL), lambda i: (i, 0)),
                   pl.BlockSpec((BLOCK, D_MODEL), lambda i: (0, 0))),
        compiler_params=_cparams(("arbitrary",)),
        name="ffn_down_last",
    )(a, w, hin, g_post)


def _band_bias_kernel(tab_ref, bucket_ref, valid_ref, o_ref):
    h = pl.program_id(0)
    bucket = bucket_ref[...]
    acc = jnp.zeros((BLOCK, 2 * BLOCK), F32)
    for b in range(NUM_BUCKETS):
        acc = jnp.where(bucket == b, tab_ref[b, h], acc)
    acc = acc * LOG2E
    for v in range(3):
        o_ref[v, 0] = jnp.where(valid_ref[v] != 0, acc, NEG)


def _band_bias(rel_bias, bucket, valid):
    return pl.pallas_call(
        _band_bias_kernel,
        out_shape=jax.ShapeDtypeStruct((3, N_HEADS, BLOCK, 2 * BLOCK), F32),
        grid=(N_HEADS,),
        in_specs=[pl.BlockSpec(memory_space=pltpu.SMEM),
                  pl.BlockSpec((BLOCK, 2 * BLOCK), lambda h: (0, 0)),
                  pl.BlockSpec((3, BLOCK, 2 * BLOCK), lambda h: (0, 0, 0))],
        out_specs=pl.BlockSpec((3, 1, BLOCK, 2 * BLOCK), lambda h: (0, h, 0, 0)),
        compiler_params=_cparams(("arbitrary",)),
        name="band_bias",
    )(rel_bias, bucket, valid)


NB_MIX = 2
TM_MIX = NB_MIX * BLOCK
N_GRP = -(-(N_XBLK + 1) // NB_MIX)


def _mix_block(b, kv_prev, u_prev8, is_tail, bias_ref, sink_ref, z_ref, u_ref, cw_ref, ext_ref,
               mix_ref):
    rows = slice(b * BLOCK, (b + 1) * BLOCK)
    q = (z_ref[rows, 0:Q_DIM] * (SCALE * LOG2E)).astype(BF16)
    kk = jnp.concatenate([kv_prev[:, 0:KV_DIM], z_ref[rows, Q_DIM:Q_DIM + KV_DIM]], axis=0)
    vv = jnp.concatenate([kv_prev[:, KV_DIM:2 * KV_DIM],
                          z_ref[rows, Q_DIM + KV_DIM:Q_DIM + 2 * KV_DIM]], axis=0)

    lo_kv = lax.broadcasted_iota(jnp.int32, (2 * BLOCK, BLOCK), 1) < HEAD_DIM
    one_lo = jnp.where(lo_kv, 1.0, 0.0).astype(BF16)
    one_hi = jnp.where(lo_kv, 0.0, 1.0).astype(BF16)
    k2, rv = {}, {}
    for col in range(KV_DIM // BLOCK):
        kc = kk[:, col * BLOCK:(col + 1) * BLOCK]
        vc = vv[:, col * BLOCK:(col + 1) * BLOCK]
        ks = pltpu.roll(kc, HEAD_DIM, 1)
        vs = pltpu.roll(vc, HEAD_DIM, 1)
        for in_hi in range(2):
            kh = 2 * col + in_hi
            own_k, oth_k = (ks, kc) if in_hi else (kc, ks)
            own_v, oth_v = (vs, vc) if in_hi else (vc, vs)
            k2[kh] = jnp.where(lo_kv, own_k, oth_k).astype(BF16)
            rv[kh] = jnp.concatenate(
                [jnp.concatenate([jnp.where(lo_kv, own_v, 0.0).astype(BF16), one_lo], axis=1),
                 jnp.concatenate([jnp.where(lo_kv, 0.0, oth_v).astype(BF16), one_hi], axis=1)],
                axis=0)

    lane = lax.broadcasted_iota(jnp.int32, (BLOCK, BLOCK), 1)
    lo_half = lane < HEAD_DIM
    zero = jnp.zeros((BLOCK, BLOCK), BF16)
    for p in range(N_HEADS // 2):
        kh = p // 2
        qp = q[:, p * BLOCK:(p + 1) * BLOCK]
        es, ms, sks = [], [], []
        for half in range(2):
            h = 2 * p + half
            qm = jnp.where(lo_half if half == 0 else jnp.logical_not(lo_half), qp, zero)
            s = lax.dot_general(qm, k2[kh], (((1,), (1,)), ((), ())),
                                preferred_element_type=F32) + bias_ref[0, h]
            sk = sink_ref[h] * LOG2E
            m = jnp.maximum(jnp.max(s, axis=-1, keepdims=True), sk)
            es.append(jnp.exp2(s - m).astype(BF16))
            ms.append(m)
            sks.append(sk)
        acc = jnp.dot(jnp.concatenate(es, axis=1), rv[kh], preferred_element_type=F32)
        sink_term = jnp.exp2(jnp.where(lo_half, sks[0], sks[1]) - jnp.where(lo_half, ms[0], ms[1]))
        den = acc[:, BLOCK:] + sink_term
        mix_ref[rows, p * BLOCK:(p + 1) * BLOCK] = (acc[:, :BLOCK] / den).astype(BF16)

    ext_ref[b, 0:8, :] = u_prev8
    ext_ref[b, 8:8 + BLOCK, :] = u_ref[rows, :]
    u1 = ext_ref[b, 7:7 + BLOCK, :]
    u2 = ext_ref[b, 6:6 + BLOCK, :]
    if is_tail is not None:
        row = lax.broadcasted_iota(jnp.int32, (BLOCK, CONV_DIM), 0)
        first = jnp.where(is_tail, META_ROW0, -8)
        u1 = jnp.where(row >= first + 1, u1, 0.0)
        u2 = jnp.where(row >= first + 2, u2, 0.0)
    gb = z_ref[rows, Q_DIM + 2 * KV_DIM:Z1_DIM]
    c = gb * (cw_ref[0:1, :] * u2 + cw_ref[1:2, :] * u1 + cw_ref[2:3, :] * u_ref[rows, :])
    mix_ref[rows, Q_DIM:] = c.astype(BF16)


def _mixer_kernel(sink_ref, z_ref, zp_ref, u_ref, up_ref, bias0_ref, bias_ref, cw_ref, mixs_ref,
                  w_ref, hin_ref, hint_ref, gpost_ref, gnext_ref,
                  h_ref, xn_ref, kp_ref, vp_ref, cp_ref, ext_ref, mix_ref, lhs_ref, *, layer):
    t = pl.program_id(0)
    i0 = jnp.minimum(t, N_GRP - 1) * NB_MIX
    is_tail = i0 == N_XBLK

    @pl.when(t == 0)
    def _():
        mix_ref[...] = jnp.zeros_like(mix_ref)

    lhs_ref[...] = mix_ref[...]
    y = jnp.dot(lhs_ref[...], _unpack_bf16(w_ref[...]), preferred_element_type=F32)

    for b in range(NB_MIX):
        if b == 0:
            _mix_block(0, zp_ref[...], up_ref[...], is_tail, bias0_ref, sink_ref, z_ref, u_ref,
                       cw_ref, ext_ref, mix_ref)
        else:
            prev = slice((b - 1) * BLOCK, b * BLOCK)
            _mix_block(b, z_ref[prev, Q_DIM:Q_DIM + 2 * KV_DIM], u_ref[b * BLOCK - 8:b * BLOCK, :],
                       None, bias_ref, sink_ref, z_ref, u_ref, cw_ref, ext_ref, mix_ref)

    hin = hin_ref[...]
    if layer == 0:
        hin = jnp.where(t - 1 == N_GRP - 1, jnp.concatenate([hint_ref[...]] * NB_MIX, axis=0), hin)
    hn = hin + _rms(y, gpost_ref[layer:layer + 1, :])
    h_ref[...] = hn
    xn_ref[...] = _rms(hn, gnext_ref[layer:layer + 1, :]).astype(BF16)

    @pl.when(is_tail)
    def _():
        mix_ref[0:DEC_BATCH, :] = mixs_ref[...].astype(BF16)

    last = i0 + NB_MIX - 1
    @pl.when(jnp.logical_and(last < N_XBLK, last % BLK_PER_SEQ == BLK_PER_SEQ - 1))
    def _():
        rows = slice((NB_MIX - 1) * BLOCK, NB_MIX * BLOCK)
        kp_ref[0] = z_ref[rows, Q_DIM:Q_DIM + KV_DIM].T
        vp_ref[0] = z_ref[rows, Q_DIM + KV_DIM:Q_DIM + 2 * KV_DIM].T
        cp_ref[0] = u_ref[TM_MIX - (CONV_WIDTH - 1):TM_MIX, :]


def _prev_blk(i):
    return jnp.where(i % BLK_PER_SEQ == 0, N_XBLK, i - 1)


def _bias_variant(i):
    return jnp.where(i == N_XBLK, 0, jnp.where(i % BLK_PER_SEQ == 0, 1, 2))


def _mixer(layer, sinks, z1, u, bias3, cw, mix_s, w_out_b, hin, hin_tail, g_post, g_next):
    assert BLK_PER_SEQ % NB_MIX == 0
    rows8 = BLOCK // 8
    grp = lambda t: jnp.minimum(t, N_GRP - 1)
    blk0 = lambda t: grp(t) * NB_MIX
    prj = lambda t: jnp.maximum(t - 1, 0)
    n_hin = -(-hin.shape[0] // TM_MIX)
    seq_of = lambda t: jnp.minimum(blk0(t) // BLK_PER_SEQ, BATCH - 1)
    bias_spec = lambda f, **kw: pl.BlockSpec((1, N_HEADS, BLOCK, 2 * BLOCK), f, **kw)
    return pl.pallas_call(
        functools.partial(_mixer_kernel, layer=layer),
        out_shape=(jax.ShapeDtypeStruct((M_ALL, D_MODEL), F32),
                   jax.ShapeDtypeStruct((M_ALL, D_MODEL), BF16),
                   jax.ShapeDtypeStruct((BATCH, KV_DIM, WINDOW), F32),
                   jax.ShapeDtypeStruct((BATCH, KV_DIM, WINDOW), F32),
                   jax.ShapeDtypeStruct((BATCH, CONV_WIDTH - 1, CONV_DIM), F32)),
        grid=(N_GRP + 1,),
        in_specs=[pl.BlockSpec(memory_space=pltpu.SMEM),
                  pl.BlockSpec((TM_MIX, Z1_DIM), lambda t: (grp(t), 0)),
                  pl.BlockSpec((BLOCK, 2 * KV_DIM),
                               lambda t: (_prev_blk(blk0(t)), Q_DIM // (2 * KV_DIM))),
                  pl.BlockSpec((TM_MIX, CONV_DIM), lambda t: (grp(t), 0)),
                  pl.BlockSpec((8, CONV_DIM), lambda t: (_prev_blk(blk0(t)) * rows8 + rows8 - 1, 0)),
                  bias_spec(lambda t: (_bias_variant(blk0(t)), 0, 0, 0)),
                  bias_spec(lambda t: (2, 0, 0, 0), pipeline_mode=pl.Buffered(1)),
                  pl.BlockSpec((CONV_WIDTH, CONV_DIM), lambda t: (0, 0)),
                  pl.BlockSpec((DEC_BATCH, D_MODEL), lambda t: (0, 0)),
                  pl.BlockSpec((D_MODEL // 2, D_MODEL), lambda t: (0, 0), pipeline_mode=pl.Buffered(1)),
                  pl.BlockSpec((TM_MIX, D_MODEL), lambda t: (jnp.minimum(prj(t), n_hin - 1), 0)),
                  pl.BlockSpec((BLOCK, D_MODEL), lambda t: (hin_tail.shape[0] // BLOCK - 1, 0)),
                  pl.BlockSpec((DEPTH, D_MODEL), lambda t: (0, 0)),
                  pl.BlockSpec((DEPTH, D_MODEL), lambda t: (0, 0))],
        out_specs=(pl.BlockSpec((TM_MIX, D_MODEL), lambda t: (prj(t), 0)),
                   pl.BlockSpec((TM_MIX, D_MODEL), lambda t: (prj(t), 0)),
                   pl.BlockSpec((1, KV_DIM, WINDOW), lambda t: (seq_of(t), 0, 0)),
                   pl.BlockSpec((1, KV_DIM, WINDOW), lambda t: (seq_of(t), 0, 0)),
                   pl.BlockSpec((1, CONV_WIDTH - 1, CONV_DIM), lambda t: (seq_of(t), 0, 0))),
        scratch_shapes=[pltpu.VMEM((NB_MIX, BLOCK + 8, CONV_DIM), F32),
                        pltpu.VMEM((TM_MIX, D_MODEL), BF16),
                        pltpu.VMEM((TM_MIX, D_MODEL), BF16)],
        compiler_params=_cparams(("arbitrary",)),
        name="mixer",
    )(sinks, z1, z1, u, u, bias3, bias3, cw, mix_s, w_out_b, hin, hin_tail, g_post, g_next)


S_CHUNK = 8
SH_ROWS = S_CHUNK * N_HEADS


def _sample_kernel(zs_ref, us_ref, kvt_ref, kc_ref, vc_ref, st_ref, cw_ref, sinkt_ref, tabt_ref,
                   brow_ref, fold_ref, unfold_ref, kall_ref, vall_ref,
                   mixs_ref, kout_ref, vout_ref, cs_ref, *, layer):
    del kall_ref, vall_ref
    z = zs_ref[...]
    q = z[:, 0:Q_DIM] * SCALE
    qrep = jnp.concatenate([jnp.broadcast_to(q[s:s + 1, :], (N_HEADS, Q_DIM))
                            for s in range(S_CHUNK)], axis=0)
    row_head = lax.broadcasted_iota(jnp.int32, (SH_ROWS, Q_DIM), 0) % N_HEADS
    own_q = lax.broadcasted_iota(jnp.int32, (SH_ROWS, Q_DIM), 1) // HEAD_DIM == row_head
    qblk = jnp.dot(jnp.where(own_q, qrep, 0.0).astype(BF16), fold_ref[...],
                   preferred_element_type=F32)
    own_kv = (lax.broadcasted_iota(jnp.int32, (N_HEADS, KV_DIM), 1) // HEAD_DIM
              == lax.broadcasted_iota(jnp.int32, (N_HEADS, KV_DIM), 0) // GROUP)

    brow = brow_ref[...]
    tabt = tabt_ref[...]
    bias_t = jnp.zeros((N_HEADS, W_BUF), F32)
    for b in range(NUM_BUCKETS):
        bias_t = jnp.where(brow == b, tabt[:, b:b + 1], bias_t)
    bias0 = tabt[:, 0:1]
    sink = sinkt_ref[:, layer:layer + 1]
    last_key = lax.broadcasted_iota(jnp.int32, (KV_DIM, W_BUF), 1) == W_BUF - 1

    o_rows = []
    for s in range(S_CHUNK):
        rows = slice(s * KV_DIM, (s + 1) * KV_DIM)
        kt = kc_ref[rows, :]
        vt = vc_ref[rows, :]
        qb = qblk[s * N_HEADS:(s + 1) * N_HEADS, :]
        k_s = z[s:s + 1, Q_DIM:Q_DIM + KV_DIM]
        v_s = z[s:s + 1, Q_DIM + KV_DIM:Q_DIM + 2 * KV_DIM]
        sc = jnp.dot(qb.astype(BF16), kt.astype(BF16), preferred_element_type=F32) + bias_t
        sn = jnp.sum(qb * k_s, axis=-1, keepdims=True) + bias0
        m = jnp.maximum(jnp.maximum(jnp.max(sc, axis=-1, keepdims=True), sn), sink)
        e = jnp.exp(sc - m)
        en = jnp.exp(sn - m)
        den = jnp.sum(e, axis=-1, keepdims=True) + en + jnp.exp(sink - m)
        o = lax.dot_general((e / den).astype(BF16), vt.astype(BF16), (((1,), (1,)), ((), ())),
                            preferred_element_type=F32)
        o_rows.append(jnp.where(own_kv, o + (en / den) * v_s, 0.0))
        kout_ref[rows, :] = jnp.where(last_key, kvt_ref[0, 0:KV_DIM, s:s + 1],
                                      pltpu.roll(kt, W_BUF - 1, 1))
        vout_ref[rows, :] = jnp.where(last_key, kvt_ref[0, KV_DIM:2 * KV_DIM, s:s + 1],
                                      pltpu.roll(vt, W_BUF - 1, 1))

    a_all = jnp.dot(jnp.concatenate(o_rows, axis=0).astype(BF16), unfold_ref[...],
                    preferred_element_type=F32)
    a_all = jnp.where(own_q, a_all, 0.0)
    mixs_ref[:, 0:Q_DIM] = jnp.sum(a_all.reshape(S_CHUNK, N_HEADS, Q_DIM), axis=1)

    gb = z[:, Q_DIM + 2 * KV_DIM:Z1_DIM]
    u = us_ref[...]
    s0 = st_ref[:, 0:CONV_DIM]
    s1 = st_ref[:, CONV_DIM:]
    mixs_ref[:, Q_DIM:] = gb * (cw_ref[0:1, :] * s0 + cw_ref[1:2, :] * s1 + cw_ref[2:3, :] * u)
    cs_ref[:, 0:CONV_DIM] = s1
    cs_ref[:, CONV_DIM:] = u


def _sample_mixer(layer, z1, u, kvt, cache_kt, cache_vt, state2d, cw, sinkt, tabt, brow,
                  fold, unfold, k_all, v_all):
    n_chunks = DEC_BATCH // S_CHUNK
    row_blk0 = M_X // S_CHUNK
    full = lambda shape: pl.BlockSpec(shape, lambda c: (0,) * len(shape))
    n_in = 14
    return pl.pallas_call(
        functools.partial(_sample_kernel, layer=layer),
        out_shape=(jax.ShapeDtypeStruct((DEC_BATCH, D_MODEL), F32),
                   jax.ShapeDtypeStruct(k_all.shape, F32),
                   jax.ShapeDtypeStruct(v_all.shape, F32),
                   jax.ShapeDtypeStruct((DEC_BATCH, 2 * CONV_DIM), F32)),
        input_output_aliases={n_in - 2: 1, n_in - 1: 2},
        grid=(n_chunks,),
        in_specs=[pl.BlockSpec((S_CHUNK, Z1_DIM), lambda c: (row_blk0 + c, 0)),
                  pl.BlockSpec((S_CHUNK, CONV_DIM), lambda c: (row_blk0 + c, 0)),
                  pl.BlockSpec((1, 2 * KV_DIM, S_CHUNK), lambda c: (c, 0, 0)),
                  pl.BlockSpec((S_CHUNK * KV_DIM, W_BUF), lambda c: (layer * n_chunks + c, 0)),
                  pl.BlockSpec((S_CHUNK * KV_DIM, W_BUF), lambda c: (layer * n_chunks + c, 0)),
                  pl.BlockSpec((S_CHUNK, 2 * CONV_DIM), lambda c: (layer * n_chunks + c, 0)),
                  full((CONV_WIDTH, CONV_DIM)),
                  full((N_HEADS, DEPTH)),
                  full((N_HEADS, NUM_BUCKETS)),
                  full((1, W_BUF)),
                  full((Q_DIM, KV_DIM)),
                  full((KV_DIM, Q_DIM)),
                  pl.BlockSpec(memory_space=pl.ANY),
                  pl.BlockSpec(memory_space=pl.ANY)],
        out_specs=(pl.BlockSpec((S_CHUNK, D_MODEL), lambda c: (c, 0)),
                   pl.BlockSpec((S_CHUNK * KV_DIM, W_BUF), lambda c: (layer * n_chunks + c, 0)),
                   pl.BlockSpec((S_CHUNK * KV_DIM, W_BUF), lambda c: (layer * n_chunks + c, 0)),
                   pl.BlockSpec((S_CHUNK, 2 * CONV_DIM), lambda c: (c, 0))),
        compiler_params=_cparams(("arbitrary",)),
        name="sample_mixer",
    )(z1, u, kvt, cache_kt, cache_vt, state2d, cw, sinkt, tabt, brow, fold, unfold, k_all, v_all)


def _t5_bucket(d):
    max_exact = NUM_BUCKETS // 2
    df = jnp.maximum(d, 1).astype(F32)
    large = max_exact + (jnp.log(df / max_exact) / math.log(MAX_DISTANCE / max_exact)
                         * (NUM_BUCKETS - max_exact)).astype(jnp.int32)
    large = jnp.minimum(large, NUM_BUCKETS - 1)
    return jnp.where(d < max_exact, d, large)


def _band_tables():
    i = np.arange(BLOCK)[:, None]
    j = np.arange(2 * BLOCK)[None, :]
    d = BLOCK + i - j
    band = (d >= 0) & (d <= WINDOW)
    valid = np.stack([band & (j >= BLOCK + META_ROW0), band & (j >= META_ROW0), band])
    bucket = _t5_bucket(jnp.asarray(np.maximum(d, 0), jnp.int32))
    return bucket, jnp.asarray(valid.astype(np.int32))


def _fold_tables():
    d = np.arange(HEAD_DIM)
    fold = np.zeros((Q_DIM, KV_DIM), np.float32)
    for h in range(N_HEADS):
        fold[h * HEAD_DIM + d, (h // GROUP) * HEAD_DIM + d] = 1.0
    return jnp.asarray(fold, BF16), jnp.asarray(fold.T, BF16)


def kernel(x_prompt, x_sample, cache_k, cache_v, state_conv, meta_tokens, rel_bias, w_in, conv_w,
           attn_sinks, w_out, norm_pre_mix, norm_post_mix, norm_pre_ffn, norm_post_ffn,
           w_gate, w_up, w_down):
    tail = jnp.concatenate([x_sample.reshape(DEC_BATCH, D_MODEL),
                            jnp.zeros((META_ROW0 - DEC_BATCH, D_MODEL), F32),
                            meta_tokens.astype(F32)], axis=0)

    bucket, valid = _band_tables()
    fold, unfold = _fold_tables()
    bias3 = _band_bias(rel_bias, bucket, valid)
    brow = _t5_bucket(jnp.asarray(W_BUF - np.arange(W_BUF), jnp.int32)).reshape(1, W_BUF)

    kv_lanes = lambda c: jnp.transpose(c, (0, 1, 3, 4, 2)).reshape(DEPTH * DEC_BATCH * KV_DIM, W_BUF)
    cache_kt = kv_lanes(cache_k)
    cache_vt = kv_lanes(cache_v)
    state2d = state_conv.reshape(DEPTH * DEC_BATCH, 2 * CONV_DIM)

    x2d = x_prompt.reshape(M_X, D_MODEL)
    xn = _norm(x2d, tail, norm_pre_mix)
    h, h_tail = x2d, tail

    def kv_rows(t, n):
        return jnp.transpose(t.reshape(n, N_KV_HEADS, HEAD_DIM, WINDOW), (0, 3, 1, 2))

    k_all = jnp.zeros(cache_kt.shape, F32)
    v_all = jnp.zeros(cache_vt.shape, F32)
    kp, vp, cp, cs = [], [], [], []
    for l in range(DEPTH):
        z1, w_out_b = _matmul(xn, w_in, w_out, l, Z1_DIM)
        (u,) = _gated(xn, w_in, w_in, None, l, Z1_DIM, Z1_DIM + CONV_DIM, CONV_DIM,
                      False, F32, "conv_in")
        kvt = jnp.transpose(z1[M_X:M_X + DEC_BATCH, Q_DIM:Q_DIM + 2 * KV_DIM]
                            .reshape(DEC_BATCH // S_CHUNK, S_CHUNK, 2 * KV_DIM), (0, 2, 1))
        mix_s, k_all, v_all, c_new = _sample_mixer(
            l, z1, u, kvt, cache_kt, cache_vt, state2d, conv_w[l], attn_sinks.T, rel_bias.T,
            brow, fold, unfold, k_all, v_all)
        h, xn, kp_l, vp_l, cp_l = _mixer(l, attn_sinks[l], z1, u, bias3, conv_w[l], mix_s,
                                         w_out_b, h, h_tail, norm_post_mix, norm_pre_ffn)
        act, w_down_b = _gated(xn, w_gate, w_up, w_down, l, 0, 0, D_FF, True, BF16, "ffn_up")
        if l + 1 < DEPTH:
            h, xn = _proj_norm(act, w_down_b, l, h, norm_post_ffn, norm_pre_mix, l + 1, "ffn_down")
            h_tail = h
        else:
            y2d, y_tail = _proj_last(act, w_down_b, l, h, norm_post_ffn)

        kp.append(kv_rows(kp_l, BATCH))
        vp.append(kv_rows(vp_l, BATCH))
        cp.append(cp_l)
        cs.append(c_new.reshape(DEC_BATCH, CONV_WIDTH - 1, CONV_DIM))

    y_prompt = y2d.reshape(BATCH, SEQ, D_MODEL)
    y_sample = y_tail[0:DEC_BATCH].reshape(DEC_BATCH, 1, D_MODEL)
    k_sample = kv_rows(k_all, DEPTH * DEC_BATCH).reshape(DEPTH, DEC_BATCH, W_BUF, N_KV_HEADS, HEAD_DIM)
    v_sample = kv_rows(v_all, DEPTH * DEC_BATCH).reshape(DEPTH, DEC_BATCH, W_BUF, N_KV_HEADS, HEAD_DIM)
    return (y_prompt, y_sample, jnp.stack(kp), jnp.stack(vp), jnp.stack(cp),
            k_sample, v_sample, jnp.stack(cs))
```

```python
import functools
import math

import numpy as np
import jax
import jax.numpy as jnp
from jax import lax
from jax.experimental import pallas as pl
from jax.experimental.pallas import tpu as pltpu

D_MODEL = 2048
BATCH = 4
SEQ = 2048
DEPTH = 4
DEC_BATCH = 32
HEAD_DIM = 64
N_HEADS = 16
N_KV_HEADS = 4
GROUP = 4
Q_DIM = 1024
KV_DIM = 256
CONV_DIM = 1024
CONV_WIDTH = 3
IN_DIM = 4608
WINDOW = 128
BLOCK = 128
NUM_BUCKETS = 32
MAX_DISTANCE = 128
N_META = 16
D_FF = 5632
RMS_EPS = 1e-6
SCALE = HEAD_DIM ** -0.5
W_BUF = 128

M_X = BATCH * SEQ
N_XBLK = M_X // BLOCK
BLK_PER_SEQ = SEQ // BLOCK
M_ALL = M_X + BLOCK
META_ROW0 = BLOCK - N_META
Z1_DIM = Q_DIM + 2 * KV_DIM + CONV_DIM
NEG = -1e30
LOG2E = math.log2(math.e)

TM_BIG = 1664
TM_EPI = 640
N_EPI = M_ALL // TM_EPI
EPI_LAST_X = M_X - (N_EPI - 1) * TM_EPI
assert (M_X - EPI_LAST_X) % EPI_LAST_X == 0 and EPI_LAST_X + BLOCK == TM_EPI
TM_PROJ = 416
RC_PROJ = 208
N_PROJ = M_ALL // TM_PROJ
PROJ_LAST_X = M_X - (N_PROJ - 1) * TM_PROJ
assert PROJ_LAST_X + BLOCK == TM_PROJ and PROJ_LAST_X % 8 == 0
TN = 512
TM_IN = 832
TN_IN = 1280
VMEM_LIMIT = 56 * 1024 * 1024

F32 = jnp.float32
BF16 = jnp.bfloat16


def _cparams(sem):
    return pltpu.CompilerParams(dimension_semantics=sem, vmem_limit_bytes=VMEM_LIMIT)


def _rms(x, g):
    return x * lax.rsqrt(jnp.mean(x * x, axis=-1, keepdims=True) + RMS_EPS) * g


def _norm_kernel(xa_ref, xb_ref, tail_ref, g_ref, xn_ref):
    last = pl.program_id(0) == N_EPI - 1
    h_last = jnp.concatenate([xb_ref[...], tail_ref[...]], axis=0)
    h = jnp.where(last, h_last, xa_ref[...])
    xn_ref[...] = _rms(h, g_ref[0:1, :]).astype(BF16)


def _norm(x2d, tail, g):
    return pl.pallas_call(
        _norm_kernel,
        out_shape=jax.ShapeDtypeStruct((M_ALL, D_MODEL), BF16),
        grid=(N_EPI,),
        in_specs=[pl.BlockSpec((TM_EPI, D_MODEL), lambda i: (jnp.minimum(i, N_EPI - 2), 0)),
                  pl.BlockSpec((EPI_LAST_X, D_MODEL), lambda i: ((M_X - EPI_LAST_X) // EPI_LAST_X, 0)),
                  pl.BlockSpec((BLOCK, D_MODEL), lambda i: (0, 0)),
                  pl.BlockSpec((DEPTH, D_MODEL), lambda i: (0, 0))],
        out_specs=pl.BlockSpec((TM_EPI, D_MODEL), lambda i: (i, 0)),
        compiler_params=_cparams(("arbitrary",)),
        name="norm0",
    )(x2d, x2d, tail, g)


RIDER_ROWS = 128


def _pack_bf16(w_f32):
    return pltpu.bitcast(w_f32.astype(BF16), jnp.uint32)


def _unpack_bf16(w_u32):
    return pltpu.bitcast(w_u32, BF16)


def _rider(wr, layer, n_outer, n_inner):
    n_chunks = wr.shape[1] // RIDER_ROWS
    assert n_chunks <= n_outer * n_inner
    chunk = lambda j, i: jnp.minimum(j * n_inner + i, n_chunks - 1)
    in_spec = pl.BlockSpec((None, RIDER_ROWS, D_MODEL), lambda j, i: (layer, chunk(j, i), 0))
    out_spec = pl.BlockSpec((RIDER_ROWS // 2, D_MODEL), lambda j, i: (chunk(j, i), 0))
    return in_spec, out_spec, jax.ShapeDtypeStruct((wr.shape[1] // 2, D_MODEL), jnp.uint32)


def _matmul_kernel(x_ref, w_ref, wr_ref, o_ref, wrb_ref, wbf_ref):
    @pl.when(pl.program_id(1) == 0)
    def _():
        wbf_ref[...] = w_ref[...].astype(BF16)

    o_ref[...] = jnp.dot(x_ref[...], wbf_ref[...], preferred_element_type=F32)
    wrb_ref[...] = _pack_bf16(wr_ref[...])


def _matmul(x, w, wr, layer, n_cols):
    k = x.shape[1]
    n_outer, n_inner = n_cols // TN_IN, M_ALL // TM_IN
    r_in, r_out, r_shape = _rider(wr, layer, n_outer, n_inner)
    return pl.pallas_call(
        _matmul_kernel,
        out_shape=(jax.ShapeDtypeStruct((M_ALL, n_cols), F32), r_shape),
        grid=(n_outer, n_inner),
        in_specs=[pl.BlockSpec((TM_IN, k), lambda j, i: (i, 0)),
                  pl.BlockSpec((None, k, TN_IN), lambda j, i: (layer, 0, j)),
                  r_in],
        out_specs=(pl.BlockSpec((TM_IN, TN_IN), lambda j, i: (i, j)), r_out),
        scratch_shapes=[pltpu.VMEM((k, TN_IN), BF16)],
        compiler_params=_cparams(("arbitrary", "arbitrary")),
        name="in_proj",
    )(x, w, wr)


def _gated_kernel(*refs, silu, has_rider):
    if has_rider:
        x_ref, wa_ref, wb_ref, wr_ref, o_ref, wrb_ref, wbf_ref = refs
    else:
        x_ref, wa_ref, wb_ref, o_ref, wbf_ref = refs

    @pl.when(pl.program_id(1) == 0)
    def _():
        wbf_ref[:, 0:TN] = wa_ref[...].astype(BF16)
        wbf_ref[:, TN:2 * TN] = wb_ref[...].astype(BF16)

    ab = jnp.dot(x_ref[...], wbf_ref[...], preferred_element_type=F32)
    a = ab[:, 0:TN]
    b = ab[:, TN:2 * TN]
    if silu:
        a = a * (1.0 / (1.0 + jnp.exp(-a)))
    o_ref[...] = (a * b).astype(o_ref.dtype)
    if has_rider:
        wrb_ref[...] = _pack_bf16(wr_ref[...])


def _gated(x, wa, wb, wr, layer, a_col0, b_col0, n_cols, silu, out_dtype, name):
    k = x.shape[1]
    a0 = a_col0 // TN
    b0 = b_col0 // TN
    n_outer, n_inner = n_cols // TN, M_ALL // TM_BIG
    in_specs = [pl.BlockSpec((TM_BIG, k), lambda j, i: (i, 0)),
                pl.BlockSpec((None, k, TN), lambda j, i: (layer, 0, a0 + j)),
                pl.BlockSpec((None, k, TN), lambda j, i: (layer, 0, b0 + j))]
    out_specs = [pl.BlockSpec((TM_BIG, TN), lambda j, i: (i, j))]
    out_shape = [jax.ShapeDtypeStruct((M_ALL, n_cols), out_dtype)]
    args = [x, wa, wb]
    if wr is not None:
        r_in, r_out, r_shape = _rider(wr, layer, n_outer, n_inner)
        in_specs.append(r_in)
        out_specs.append(r_out)
        out_shape.append(r_shape)
        args.append(wr)
    return pl.pallas_call(
        functools.partial(_gated_kernel, silu=silu, has_rider=wr is not None),
        out_shape=tuple(out_shape),
        grid=(n_outer, n_inner),
        in_specs=in_specs,
        out_specs=tuple(out_specs),
        scratch_shapes=[pltpu.VMEM((k, 2 * TN), BF16)],
        compiler_params=_cparams(("arbitrary", "arbitrary")),
        name=name,
    )(*args)


def _proj_norm_kernel(a_ref, w_ref, hin_ref, gpost_ref, gnext_ref, h_ref, xn_ref, *, l_post, l_next):
    g_post = gpost_ref[l_post:l_post + 1, :]
    g_next = gnext_ref[l_next:l_next + 1, :]
    for c in range(TM_PROJ // RC_PROJ):
        rows = slice(c * RC_PROJ, (c + 1) * RC_PROJ)
        y = jnp.dot(a_ref[rows, :], _unpack_bf16(w_ref[...]), preferred_element_type=F32)
        hn = hin_ref[rows, :] + _rms(y, g_post)
        h_ref[rows, :] = hn
        xn_ref[rows, :] = _rms(hn, g_next).astype(BF16)


def _proj_norm(a, w, layer, hin, g_post, g_next, l_next, name):
    k = a.shape[1]
    return pl.pallas_call(
        functools.partial(_proj_norm_kernel, l_post=layer, l_next=l_next),
        out_shape=(jax.ShapeDtypeStruct((M_ALL, D_MODEL), F32),
                   jax.ShapeDtypeStruct((M_ALL, D_MODEL), BF16)),
        grid=(M_ALL // TM_PROJ,),
        in_specs=[pl.BlockSpec((TM_PROJ, k), lambda i: (i, 0)),
                  pl.BlockSpec((k // 2, D_MODEL), lambda i: (0, 0), pipeline_mode=pl.Buffered(1)),
                  pl.BlockSpec((TM_PROJ, D_MODEL), lambda i: (i, 0)),
                  pl.BlockSpec((DEPTH, D_MODEL), lambda i: (0, 0)),
                  pl.BlockSpec((DEPTH, D_MODEL), lambda i: (0, 0))],
        out_specs=(pl.BlockSpec((TM_PROJ, D_MODEL), lambda i: (i, 0)),
                   pl.BlockSpec((TM_PROJ, D_MODEL), lambda i: (i, 0))),
        compiler_params=_cparams(("parallel",)),
        name=name,
    )(a, w, hin, g_post, g_next)


def _proj_last_kernel(a_ref, w_ref, hin_ref, gpost_ref, y_ref, tail_ref, *, l_post):
    g_post = gpost_ref[l_post:l_post + 1, :]
    for c in range(TM_PROJ // RC_PROJ):
        rows = slice(c * RC_PROJ, (c + 1) * RC_PROJ)
        y = jnp.dot(a_ref[rows, :], _unpack_bf16(w_ref[...]), preferred_element_type=F32)
        y_ref[rows, :] = hin_ref[rows, :] + _rms(y, g_post)

    @pl.when(pl.program_id(0) == N_PROJ - 1)
    def _():
        tail_ref[...] = y_ref[PROJ_LAST_X:TM_PROJ, :]


def _proj_last(a, w, layer, hin, g_post):
    k = a.shape[1]
    return pl.pallas_call(
        functools.partial(_proj_last_kernel, l_post=layer),
        out_shape=(jax.ShapeDtypeStruct((M_X, D_MODEL), F32),
                   jax.ShapeDtypeStruct((BLOCK, D_MODEL), F32)),
        grid=(N_PROJ,),
        in_specs=[pl.BlockSpec((TM_PROJ, k), lambda i: (i, 0)),
                  pl.BlockSpec((k // 2, D_MODEL), lambda i: (0, 0), pipeline_mode=pl.Buffered(1)),
                  pl.BlockSpec((TM_PROJ, D_MODEL), lambda i: (i, 0)),
                  pl.BlockSpec((DEPTH, D_MODEL), lambda i: (0, 0))],
        out_specs=(pl.BlockSpec((TM_PROJ, D_MODEL), lambda i: (i, 0)),
                   pl.BlockSpec((BLOCK, D_MODEL), lambda i: (0, 0))),
        compiler_params=_cparams(("arbitrary",)),
        name="ffn_down_last",
    )(a, w, hin, g_post)


def _band_bias_kernel(tab_ref, bucket_ref, valid_ref, o_ref):
    h = pl.program_id(0)
    bucket = bucket_ref[...]
    acc = jnp.zeros((BLOCK, 2 * BLOCK), F32)
    for b in range(NUM_BUCKETS):
        acc = jnp.where(bucket == b, tab_ref[b, h], acc)
    acc = acc * LOG2E
    for v in range(3):
        o_ref[v, 0] = jnp.where(valid_ref[v] != 0, acc, NEG)


def _band_bias(rel_bias, bucket, valid):
    return pl.pallas_call(
        _band_bias_kernel,
        out_shape=jax.ShapeDtypeStruct((3, N_HEADS, BLOCK, 2 * BLOCK), F32),
        grid=(N_HEADS,),
        in_specs=[pl.BlockSpec(memory_space=pltpu.SMEM),
                  pl.BlockSpec((BLOCK, 2 * BLOCK), lambda h: (0, 0)),
                  pl.BlockSpec((3, BLOCK, 2 * BLOCK), lambda h: (0, 0, 0))],
        out_specs=pl.BlockSpec((3, 1, BLOCK, 2 * BLOCK), lambda h: (0, h, 0, 0)),
        compiler_params=_cparams(("arbitrary",)),
        name="band_bias",
    )(rel_bias, bucket, valid)


NB_MIX = 2
TM_MIX = NB_MIX * BLOCK
N_GRP = -(-(N_XBLK + 1) // NB_MIX)


def _mix_block(b, kv_prev, u_prev8, is_tail, bias_ref, sink_ref, z_ref, u_ref, cw_ref, ext_ref,
               mix_ref):
    rows = slice(b * BLOCK, (b + 1) * BLOCK)
    q = (z_ref[rows, 0:Q_DIM] * (SCALE * LOG2E)).astype(BF16)
    kk = jnp.concatenate([kv_prev[:, 0:KV_DIM], z_ref[rows, Q_DIM:Q_DIM + KV_DIM]], axis=0)
    vv = jnp.concatenate([kv_prev[:, KV_DIM:2 * KV_DIM],
                          z_ref[rows, Q_DIM + KV_DIM:Q_DIM + 2 * KV_DIM]], axis=0)

    lo_kv = lax.broadcasted_iota(jnp.int32, (2 * BLOCK, BLOCK), 1) < HEAD_DIM
    one_lo = jnp.where(lo_kv, 1.0, 0.0).astype(BF16)
    one_hi = jnp.where(lo_kv, 0.0, 1.0).astype(BF16)
    k2, rv = {}, {}
    for col in range(KV_DIM // BLOCK):
        kc = kk[:, col * BLOCK:(col + 1) * BLOCK]
        vc = vv[:, col * BLOCK:(col + 1) * BLOCK]
        ks = pltpu.roll(kc, HEAD_DIM, 1)
        vs = pltpu.roll(vc, HEAD_DIM, 1)
        for in_hi in range(2):
            kh = 2 * col + in_hi
            own_k, oth_k = (ks, kc) if in_hi else (kc, ks)
            own_v, oth_v = (vs, vc) if in_hi else (vc, vs)
            k2[kh] = jnp.where(lo_kv, own_k, oth_k).astype(BF16)
            rv[kh] = jnp.concatenate(
                [jnp.concatenate([jnp.where(lo_kv, own_v, 0.0).astype(BF16), one_lo], axis=1),
                 jnp.concatenate([jnp.where(lo_kv, 0.0, oth_v).astype(BF16), one_hi], axis=1)],
                axis=0)

    lane = lax.broadcasted_iota(jnp.int32, (BLOCK, BLOCK), 1)
    lo_half = lane < HEAD_DIM
    zero = jnp.zeros((BLOCK, BLOCK), BF16)
    for p in range(N_HEADS // 2):
        kh = p // 2
        qp = q[:, p * BLOCK:(p + 1) * BLOCK]
        es, ms, sks = [], [], []
        for half in range(2):
            h = 2 * p + half
            qm = jnp.where(lo_half if half == 0 else jnp.logical_not(lo_half), qp, zero)
            s = lax.dot_general(qm, k2[kh], (((1,), (1,)), ((), ())),
                                preferred_element_type=F32) + bias_ref[0, h]
            sk = sink_ref[h] * LOG2E
            m = jnp.maximum(jnp.max(s, axis=-1, keepdims=True), sk)
            es.append(jnp.exp2(s - m).astype(BF16))
            ms.append(m)
            sks.append(sk)
        acc = jnp.dot(jnp.concatenate(es, axis=1), rv[kh], preferred_element_type=F32)
        sink_term = jnp.exp2(jnp.where(lo_half, sks[0], sks[1]) - jnp.where(lo_half, ms[0], ms[1]))
        den = acc[:, BLOCK:] + sink_term
        mix_ref[rows, p * BLOCK:(p + 1) * BLOCK] = (acc[:, :BLOCK] / den).astype(BF16)

    ext_ref[b, 0:8, :] = u_prev8
    ext_ref[b, 8:8 + BLOCK, :] = u_ref[rows, :]
    u1 = ext_ref[b, 7:7 + BLOCK, :]
    u2 = ext_ref[b, 6:6 + BLOCK, :]
    if is_tail is not None:
        row = lax.broadcasted_iota(jnp.int32, (BLOCK, CONV_DIM), 0)
        first = jnp.where(is_tail, META_ROW0, -8)
        u1 = jnp.where(row >= first + 1, u1, 0.0)
        u2 = jnp.where(row >= first + 2, u2, 0.0)
    gb = z_ref[rows, Q_DIM + 2 * KV_DIM:Z1_DIM]
    c = gb * (cw_ref[0:1, :] * u2 + cw_ref[1:2, :] * u1 + cw_ref[2:3, :] * u_ref[rows, :])
    mix_ref[rows, Q_DIM:] = c.astype(BF16)


def _mixer_kernel(sink_ref, z_ref, zp_ref, u_ref, up_ref, bias0_ref, bias_ref, cw_ref, mixs_ref,
                  w_ref, hin_ref, hint_ref, gpost_ref, gnext_ref,
                  h_ref, xn_ref, kp_ref, vp_ref, cp_ref, ext_ref, mix_ref, lhs_ref, *, layer):
    t = pl.program_id(0)
    i0 = jnp.minimum(t, N_GRP - 1) * NB_MIX
    is_tail = i0 == N_XBLK

    @pl.when(t == 0)
    def _():
        mix_ref[...] = jnp.zeros_like(mix_ref)

    lhs_ref[...] = mix_ref[...]
    y = jnp.dot(lhs_ref[...], _unpack_bf16(w_ref[...]), preferred_element_type=F32)

    for b in range(NB_MIX):
        if b == 0:
            _mix_block(0, zp_ref[...], up_ref[...], is_tail, bias0_ref, sink_ref, z_ref, u_ref,
                       cw_ref, ext_ref, mix_ref)
        else:
            prev = slice((b - 1) * BLOCK, b * BLOCK)
            _mix_block(b, z_ref[prev, Q_DIM:Q_DIM + 2 * KV_DIM], u_ref[b * BLOCK - 8:b * BLOCK, :],
                       None, bias_ref, sink_ref, z_ref, u_ref, cw_ref, ext_ref, mix_ref)

    hin = hin_ref[...]
    if layer == 0:
        hin = jnp.where(t - 1 == N_GRP - 1, jnp.concatenate([hint_ref[...]] * NB_MIX, axis=0), hin)
    hn = hin + _rms(y, gpost_ref[layer:layer + 1, :])
    h_ref[...] = hn
    xn_ref[...] = _rms(hn, gnext_ref[layer:layer + 1, :]).astype(BF16)

    @pl.when(is_tail)
    def _():
        mix_ref[0:DEC_BATCH, :] = mixs_ref[...].astype(BF16)

    last = i0 + NB_MIX - 1
    @pl.when(jnp.logical_and(last < N_XBLK, last % BLK_PER_SEQ == BLK_PER_SEQ - 1))
    def _():
        rows = slice((NB_MIX - 1) * BLOCK, NB_MIX * BLOCK)
        kp_ref[0] = z_ref[rows, Q_DIM:Q_DIM + KV_DIM].T
        vp_ref[0] = z_ref[rows, Q_DIM + KV_DIM:Q_DIM + 2 * KV_DIM].T
        cp_ref[0] = u_ref[TM_MIX - (CONV_WIDTH - 1):TM_MIX, :]


def _prev_blk(i):
    return jnp.where(i % BLK_PER_SEQ == 0, N_XBLK, i - 1)


def _bias_variant(i):
    return jnp.where(i == N_XBLK, 0, jnp.where(i % BLK_PER_SEQ == 0, 1, 2))


def _mixer(layer, sinks, z1, u, bias3, cw, mix_s, w_out_b, hin, hin_tail, g_post, g_next):
    assert BLK_PER_SEQ % NB_MIX == 0
    rows8 = BLOCK // 8
    grp = lambda t: jnp.minimum(t, N_GRP - 1)
    blk0 = lambda t: grp(t) * NB_MIX
    prj = lambda t: jnp.maximum(t - 1, 0)
    n_hin = -(-hin.shape[0] // TM_MIX)
    seq_of = lambda t: jnp.minimum(blk0(t) // BLK_PER_SEQ, BATCH - 1)
    bias_spec = lambda f, **kw: pl.BlockSpec((1, N_HEADS, BLOCK, 2 * BLOCK), f, **kw)
    return pl.pallas_call(
        functools.partial(_mixer_kernel, layer=layer),
        out_shape=(jax.ShapeDtypeStruct((M_ALL, D_MODEL), F32),
                   jax.ShapeDtypeStruct((M_ALL, D_MODEL), BF16),
                   jax.ShapeDtypeStruct((BATCH, KV_DIM, WINDOW), F32),
                   jax.ShapeDtypeStruct((BATCH, KV_DIM, WINDOW), F32),
                   jax.ShapeDtypeStruct((BATCH, CONV_WIDTH - 1, CONV_DIM), F32)),
        grid=(N_GRP + 1,),
        in_specs=[pl.BlockSpec(memory_space=pltpu.SMEM),
                  pl.BlockSpec((TM_MIX, Z1_DIM), lambda t: (grp(t), 0)),
                  pl.BlockSpec((BLOCK, 2 * KV_DIM),
                               lambda t: (_prev_blk(blk0(t)), Q_DIM // (2 * KV_DIM))),
                  pl.BlockSpec((TM_MIX, CONV_DIM), lambda t: (grp(t), 0)),
                  pl.BlockSpec((8, CONV_DIM), lambda t: (_prev_blk(blk0(t)) * rows8 + rows8 - 1, 0)),
                  bias_spec(lambda t: (_bias_variant(blk0(t)), 0, 0, 0)),
                  bias_spec(lambda t: (2, 0, 0, 0), pipeline_mode=pl.Buffered(1)),
                  pl.BlockSpec((CONV_WIDTH, CONV_DIM), lambda t: (0, 0)),
                  pl.BlockSpec((DEC_BATCH, D_MODEL), lambda t: (0, 0)),
                  pl.BlockSpec((D_MODEL // 2, D_MODEL), lambda t: (0, 0), pipeline_mode=pl.Buffered(1)),
                  pl.BlockSpec((TM_MIX, D_MODEL), lambda t: (jnp.minimum(prj(t), n_hin - 1), 0)),
                  pl.BlockSpec((BLOCK, D_MODEL), lambda t: (hin_tail.shape[0] // BLOCK - 1, 0)),
                  pl.BlockSpec((DEPTH, D_MODEL), lambda t: (0, 0)),
                  pl.BlockSpec((DEPTH, D_MODEL), lambda t: (0, 0))],
        out_specs=(pl.BlockSpec((TM_MIX, D_MODEL), lambda t: (prj(t), 0)),
                   pl.BlockSpec((TM_MIX, D_MODEL), lambda t: (prj(t), 0)),
                   pl.BlockSpec((1, KV_DIM, WINDOW), lambda t: (seq_of(t), 0, 0)),
                   pl.BlockSpec((1, KV_DIM, WINDOW), lambda t: (seq_of(t), 0, 0)),
                   pl.BlockSpec((1, CONV_WIDTH - 1, CONV_DIM), lambda t: (seq_of(t), 0, 0))),
        scratch_shapes=[pltpu.VMEM((NB_MIX, BLOCK + 8, CONV_DIM), F32),
                        pltpu.VMEM((TM_MIX, D_MODEL), BF16),
                        pltpu.VMEM((TM_MIX, D_MODEL), BF16)],
        compiler_params=_cparams(("arbitrary",)),
        name="mixer",
    )(sinks, z1, z1, u, u, bias3, bias3, cw, mix_s, w_out_b, hin, hin_tail, g_post, g_next)


S_CHUNK = 16
SH_ROWS = S_CHUNK * N_HEADS


def _sample_kernel(zs_ref, us_ref, kvt_ref, kc_ref, vc_ref, st_ref, cw_ref, sinkt_ref, tabt_ref,
                   brow_ref, fold_ref, unfold_ref, kall_ref, vall_ref,
                   mixs_ref, kout_ref, vout_ref, cs_ref, *, layer):
    del kall_ref, vall_ref
    z = zs_ref[...]
    q = z[:, 0:Q_DIM] * SCALE
    qrep = jnp.concatenate([jnp.broadcast_to(q[s:s + 1, :], (N_HEADS, Q_DIM))
                            for s in range(S_CHUNK)], axis=0)
    row_head = lax.broadcasted_iota(jnp.int32, (SH_ROWS, Q_DIM), 0) % N_HEADS
    own_q = lax.broadcasted_iota(jnp.int32, (SH_ROWS, Q_DIM), 1) // HEAD_DIM == row_head
    qblk = jnp.dot(jnp.where(own_q, qrep, 0.0).astype(BF16), fold_ref[...],
                   preferred_element_type=F32)
    own_kv = (lax.broadcasted_iota(jnp.int32, (N_HEADS, KV_DIM), 1) // HEAD_DIM
              == lax.broadcasted_iota(jnp.int32, (N_HEADS, KV_DIM), 0) // GROUP)

    brow = brow_ref[...]
    tabt = tabt_ref[...]
    bias_t = jnp.zeros((N_HEADS, W_BUF), F32)
    for b in range(NUM_BUCKETS):
        bias_t = jnp.where(brow == b, tabt[:, b:b + 1], bias_t)
    bias0 = tabt[:, 0:1]
    sink = sinkt_ref[:, layer:layer + 1]
    last_key = lax.broadcasted_iota(jnp.int32, (KV_DIM, W_BUF), 1) == W_BUF - 1

    tile_s = lambda x: jnp.concatenate([x] * S_CHUNK, axis=0)
    rep_s = lambda c0, c1: jnp.concatenate(
        [jnp.broadcast_to(z[s:s + 1, c0:c1], (N_HEADS, c1 - c0)) for s in range(S_CHUNK)], axis=0)
    sample_rows = lambda s: slice(s * KV_DIM, (s + 1) * KV_DIM)
    head_rows = lambda s: slice(s * N_HEADS, (s + 1) * N_HEADS)

    qblk_b = qblk.astype(BF16)
    sc = jnp.concatenate(
        [jnp.dot(qblk_b[head_rows(s), :], kc_ref[sample_rows(s), :].astype(BF16),
                 preferred_element_type=F32) for s in range(S_CHUNK)], axis=0) + tile_s(bias_t)
    sn = (jnp.sum(qblk * rep_s(Q_DIM, Q_DIM + KV_DIM), axis=-1, keepdims=True)
          + tile_s(bias0))
    sink_all = tile_s(sink)
    m = jnp.maximum(jnp.maximum(jnp.max(sc, axis=-1, keepdims=True), sn), sink_all)
    e = jnp.exp(sc - m)
    en = jnp.exp(sn - m)
    den = jnp.sum(e, axis=-1, keepdims=True) + en + jnp.exp(sink_all - m)
    p = (e / den).astype(BF16)
    o_all = jnp.concatenate(
        [lax.dot_general(p[head_rows(s), :], vc_ref[sample_rows(s), :].astype(BF16),
                         (((1,), (1,)), ((), ())), preferred_element_type=F32)
         for s in range(S_CHUNK)], axis=0)
    o_all = jnp.where(tile_s(own_kv), o_all + (en / den) * rep_s(Q_DIM + KV_DIM, Q_DIM + 2 * KV_DIM),
                      0.0)

    for s in range(S_CHUNK):
        rows = sample_rows(s)
        kout_ref[rows, :] = jnp.where(last_key, kvt_ref[0, 0:KV_DIM, s:s + 1],
                                      pltpu.roll(kc_ref[rows, :], W_BUF - 1, 1))
        vout_ref[rows, :] = jnp.where(last_key, kvt_ref[0, KV_DIM:2 * KV_DIM, s:s + 1],
                                      pltpu.roll(vc_ref[rows, :], W_BUF - 1, 1))

    a_all = jnp.dot(o_all.astype(BF16), unfold_ref[...],
                    preferred_element_type=F32)
    a_all = jnp.where(own_q, a_all, 0.0)
    mixs_ref[:, 0:Q_DIM] = jnp.sum(a_all.reshape(S_CHUNK, N_HEADS, Q_DIM), axis=1)

    gb = z[:, Q_DIM + 2 * KV_DIM:Z1_DIM]
    u = us_ref[...]
    s0 = st_ref[:, 0:CONV_DIM]
    s1 = st_ref[:, CONV_DIM:]
    mixs_ref[:, Q_DIM:] = gb * (cw_ref[0:1, :] * s0 + cw_ref[1:2, :] * s1 + cw_ref[2:3, :] * u)
    cs_ref[:, 0:CONV_DIM] = s1
    cs_ref[:, CONV_DIM:] = u


def _sample_mixer(layer, z1, u, kvt, cache_kt, cache_vt, state2d, cw, sinkt, tabt, brow,
                  fold, unfold, k_all, v_all):
    n_chunks = DEC_BATCH // S_CHUNK
    row_blk0 = M_X // S_CHUNK
    full = lambda shape: pl.BlockSpec(shape, lambda c: (0,) * len(shape))
    n_in = 14
    return pl.pallas_call(
        functools.partial(_sample_kernel, layer=layer),
        out_shape=(jax.ShapeDtypeStruct((DEC_BATCH, D_MODEL), F32),
                   jax.ShapeDtypeStruct(k_all.shape, F32),
                   jax.ShapeDtypeStruct(v_all.shape, F32),
                   jax.ShapeDtypeStruct((DEC_BATCH, 2 * CONV_DIM), F32)),
        input_output_aliases={n_in - 2: 1, n_in - 1: 2},
        grid=(n_chunks,),
        in_specs=[pl.BlockSpec((S_CHUNK, Z1_DIM), lambda c: (row_blk0 + c, 0)),
                  pl.BlockSpec((S_CHUNK, CONV_DIM), lambda c: (row_blk0 + c, 0)),
                  pl.BlockSpec((1, 2 * KV_DIM, S_CHUNK), lambda c: (c, 0, 0)),
                  pl.BlockSpec((S_CHUNK * KV_DIM, W_BUF), lambda c: (layer * n_chunks + c, 0)),
                  pl.BlockSpec((S_CHUNK * KV_DIM, W_BUF), lambda c: (layer * n_chunks + c, 0)),
                  pl.BlockSpec((S_CHUNK, 2 * CONV_DIM), lambda c: (layer * n_chunks + c, 0)),
                  full((CONV_WIDTH, CONV_DIM)),
                  full((N_HEADS, DEPTH)),
                  full((N_HEADS, NUM_BUCKETS)),
                  full((1, W_BUF)),
                  full((Q_DIM, KV_DIM)),
                  full((KV_DIM, Q_DIM)),
                  pl.BlockSpec(memory_space=pl.ANY),
                  pl.BlockSpec(memory_space=pl.ANY)],
        out_specs=(pl.BlockSpec((S_CHUNK, D_MODEL), lambda c: (c, 0)),
                   pl.BlockSpec((S_CHUNK * KV_DIM, W_BUF), lambda c: (layer * n_chunks + c, 0)),
                   pl.BlockSpec((S_CHUNK * KV_DIM, W_BUF), lambda c: (layer * n_chunks + c, 0)),
                   pl.BlockSpec((S_CHUNK, 2 * CONV_DIM), lambda c: (c, 0))),
        compiler_params=_cparams(("arbitrary",)),
        name="sample_mixer",
    )(z1, u, kvt, cache_kt, cache_vt, state2d, cw, sinkt, tabt, brow, fold, unfold, k_all, v_all)


def _t5_bucket(d):
    max_exact = NUM_BUCKETS // 2
    df = jnp.maximum(d, 1).astype(F32)
    large = max_exact + (jnp.log(df / max_exact) / math.log(MAX_DISTANCE / max_exact)
                         * (NUM_BUCKETS - max_exact)).astype(jnp.int32)
    large = jnp.minimum(large, NUM_BUCKETS - 1)
    return jnp.where(d < max_exact, d, large)


def _band_tables():
    i = np.arange(BLOCK)[:, None]
    j = np.arange(2 * BLOCK)[None, :]
    d = BLOCK + i - j
    band = (d >= 0) & (d <= WINDOW)
    valid = np.stack([band & (j >= BLOCK + META_ROW0), band & (j >= META_ROW0), band])
    bucket = _t5_bucket(jnp.asarray(np.maximum(d, 0), jnp.int32))
    return bucket, jnp.asarray(valid.astype(np.int32))


def _fold_tables():
    d = np.arange(HEAD_DIM)
    fold = np.zeros((Q_DIM, KV_DIM), np.float32)
    for h in range(N_HEADS):
        fold[h * HEAD_DIM + d, (h // GROUP) * HEAD_DIM + d] = 1.0
    return jnp.asarray(fold, BF16), jnp.asarray(fold.T, BF16)


def kernel(x_prompt, x_sample, cache_k, cache_v, state_conv, meta_tokens, rel_bias, w_in, conv_w,
           attn_sinks, w_out, norm_pre_mix, norm_post_mix, norm_pre_ffn, norm_post_ffn,
           w_gate, w_up, w_down):
    tail = jnp.concatenate([x_sample.reshape(DEC_BATCH, D_MODEL),
                            jnp.zeros((META_ROW0 - DEC_BATCH, D_MODEL), F32),
                            meta_tokens.astype(F32)], axis=0)

    bucket, valid = _band_tables()
    fold, unfold = _fold_tables()
    bias3 = _band_bias(rel_bias, bucket, valid)
    brow = _t5_bucket(jnp.asarray(W_BUF - np.arange(W_BUF), jnp.int32)).reshape(1, W_BUF)

    kv_lanes = lambda c: jnp.transpose(c, (0, 1, 3, 4, 2)).reshape(DEPTH * DEC_BATCH * KV_DIM, W_BUF)
    cache_kt = kv_lanes(cache_k)
    cache_vt = kv_lanes(cache_v)
    state2d = state_conv.reshape(DEPTH * DEC_BATCH, 2 * CONV_DIM)

    x2d = x_prompt.reshape(M_X, D_MODEL)
    xn = _norm(x2d, tail, norm_pre_mix)
    h, h_tail = x2d, tail

    def kv_rows(t, n):
        return jnp.transpose(t.reshape(n, N_KV_HEADS, HEAD_DIM, WINDOW), (0, 3, 1, 2))

    k_all = jnp.zeros(cache_kt.shape, F32)
    v_all = jnp.zeros(cache_vt.shape, F32)
    kp, vp, cp, cs = [], [], [], []
    for l in range(DEPTH):
        z1, w_out_b = _matmul(xn, w_in, w_out, l, Z1_DIM)
        (u,) = _gated(xn, w_in, w_in, None, l, Z1_DIM, Z1_DIM + CONV_DIM, CONV_DIM,
                      False, F32, "conv_in")
        kvt = jnp.transpose(z1[M_X:M_X + DEC_BATCH, Q_DIM:Q_DIM + 2 * KV_DIM]
                            .reshape(DEC_BATCH // S_CHUNK, S_CHUNK, 2 * KV_DIM), (0, 2, 1))
        mix_s, k_all, v_all, c_new = _sample_mixer(
            l, z1, u, kvt, cache_kt, cache_vt, state2d, conv_w[l], attn_sinks.T, rel_bias.T,
            brow, fold, unfold, k_all, v_all)
        h, xn, kp_l, vp_l, cp_l = _mixer(l, attn_sinks[l], z1, u, bias3, conv_w[l], mix_s,
                                         w_out_b, h, h_tail, norm_post_mix, norm_pre_ffn)
        act, w_down_b = _gated(xn, w_gate, w_up, w_down, l, 0, 0, D_FF, True, BF16, "ffn_up")
        if l + 1 < DEPTH:
            h, xn = _proj_norm(act, w_down_b, l, h, norm_post_ffn, norm_pre_mix, l + 1, "ffn_down")
            h_tail = h
        else:
            y2d, y_tail = _proj_last(act, w_down_b, l, h, norm_post_ffn)

        kp.append(kv_rows(kp_l, BATCH))
        vp.append(kv_rows(vp_l, BATCH))
        cp.append(cp_l)
        cs.append(c_new.reshape(DEC_BATCH, CONV_WIDTH - 1, CONV_DIM))

    y_prompt = y2d.reshape(BATCH, SEQ, D_MODEL)
    y_sample = y_tail[0:DEC_BATCH].reshape(DEC_BATCH, 1, D_MODEL)
    k_sample = kv_rows(k_all, DEPTH * DEC_BATCH).reshape(DEPTH, DEC_BATCH, W_BUF, N_KV_HEADS, HEAD_DIM)
    v_sample = kv_rows(v_all, DEPTH * DEC_BATCH).reshape(DEPTH, DEC_BATCH, W_BUF, N_KV_HEADS, HEAD_DIM)
    return (y_prompt, y_sample, jnp.stack(kp), jnp.stack(vp), jnp.stack(cp),
            k_sample, v_sample, jnp.stack(cs))
```

```python
import functools
import math

import numpy as np
import jax
import jax.numpy as jnp
from jax import lax
from jax.experimental import pallas as pl
from jax.experimental.pallas import tpu as pltpu

D_MODEL = 2048
BATCH = 4
SEQ = 2048
DEPTH = 4
DEC_BATCH = 32
HEAD_DIM = 64
N_HEADS = 16
N_KV_HEADS = 4
GROUP = 4
Q_DIM = 1024
KV_DIM = 256
CONV_DIM = 1024
CONV_WIDTH = 3
IN_DIM = 4608
WINDOW = 128
BLOCK = 128
NUM_BUCKETS = 32
MAX_DISTANCE = 128
N_META = 16
D_FF = 5632
RMS_EPS = 1e-6
SCALE = HEAD_DIM ** -0.5
W_BUF = 128

M_X = BATCH * SEQ
N_XBLK = M_X // BLOCK
BLK_PER_SEQ = SEQ // BLOCK
M_ALL = M_X + BLOCK
META_ROW0 = BLOCK - N_META
Z1_DIM = Q_DIM + 2 * KV_DIM + CONV_DIM
NEG = -1e30
LOG2E = math.log2(math.e)

TM_BIG = 1664
TM_EPI = 640
N_EPI = M_ALL // TM_EPI
EPI_LAST_X = M_X - (N_EPI - 1) * TM_EPI
assert (M_X - EPI_LAST_X) % EPI_LAST_X == 0 and EPI_LAST_X + BLOCK == TM_EPI
TM_PROJ = 416
RC_PROJ = 208
N_PROJ = M_ALL // TM_PROJ
PROJ_LAST_X = M_X - (N_PROJ - 1) * TM_PROJ
assert PROJ_LAST_X + BLOCK == TM_PROJ and PROJ_LAST_X % 8 == 0
TN = 512
TM_IN = 832
TN_IN = 1280
VMEM_LIMIT = 56 * 1024 * 1024

F32 = jnp.float32
BF16 = jnp.bfloat16


def _cparams(sem):
    return pltpu.CompilerParams(dimension_semantics=sem, vmem_limit_bytes=VMEM_LIMIT)


def _rms(x, g):
    return x * lax.rsqrt(jnp.mean(x * x, axis=-1, keepdims=True) + RMS_EPS) * g


def _norm_kernel(xa_ref, xb_ref, tail_ref, g_ref, xn_ref):
    last = pl.program_id(0) == N_EPI - 1
    h_last = jnp.concatenate([xb_ref[...], tail_ref[...]], axis=0)
    h = jnp.where(last, h_last, xa_ref[...])
    xn_ref[...] = _rms(h, g_ref[0:1, :]).astype(BF16)


def _norm(x2d, tail, g):
    return pl.pallas_call(
        _norm_kernel,
        out_shape=jax.ShapeDtypeStruct((M_ALL, D_MODEL), BF16),
        grid=(N_EPI,),
        in_specs=[pl.BlockSpec((TM_EPI, D_MODEL), lambda i: (jnp.minimum(i, N_EPI - 2), 0)),
                  pl.BlockSpec((EPI_LAST_X, D_MODEL), lambda i: ((M_X - EPI_LAST_X) // EPI_LAST_X, 0)),
                  pl.BlockSpec((BLOCK, D_MODEL), lambda i: (0, 0)),
                  pl.BlockSpec((DEPTH, D_MODEL), lambda i: (0, 0))],
        out_specs=pl.BlockSpec((TM_EPI, D_MODEL), lambda i: (i, 0)),
        compiler_params=_cparams(("arbitrary",)),
        name="norm0",
    )(x2d, x2d, tail, g)


RIDER_ROWS = 128


def _pack_bf16(w_f32):
    return pltpu.bitcast(w_f32.astype(BF16), jnp.uint32)


def _unpack_bf16(w_u32):
    return pltpu.bitcast(w_u32, BF16)


def _rider(wr, layer, n_outer, n_inner):
    n_chunks = wr.shape[1] // RIDER_ROWS
    assert n_chunks <= n_outer * n_inner
    chunk = lambda j, i: jnp.minimum(j * n_inner + i, n_chunks - 1)
    in_spec = pl.BlockSpec((None, RIDER_ROWS, D_MODEL), lambda j, i: (layer, chunk(j, i), 0))
    out_spec = pl.BlockSpec((RIDER_ROWS // 2, D_MODEL), lambda j, i: (chunk(j, i), 0))
    return in_spec, out_spec, jax.ShapeDtypeStruct((wr.shape[1] // 2, D_MODEL), jnp.uint32)


def _matmul_kernel(x_ref, w_ref, wr_ref, o_ref, wrb_ref, wbf_ref):
    @pl.when(pl.program_id(1) == 0)
    def _():
        wbf_ref[...] = w_ref[...].astype(BF16)

    o_ref[...] = jnp.dot(x_ref[...], wbf_ref[...], preferred_element_type=F32)
    wrb_ref[...] = _pack_bf16(wr_ref[...])


def _matmul(x, w, wr, layer, n_cols):
    k = x.shape[1]
    n_outer, n_inner = n_cols // TN_IN, M_ALL // TM_IN
    r_in, r_out, r_shape = _rider(wr, layer, n_outer, n_inner)
    return pl.pallas_call(
        _matmul_kernel,
        out_shape=(jax.ShapeDtypeStruct((M_ALL, n_cols), F32), r_shape),
        grid=(n_outer, n_inner),
        in_specs=[pl.BlockSpec((TM_IN, k), lambda j, i: (i, 0)),
                  pl.BlockSpec((None, k, TN_IN), lambda j, i: (layer, 0, j)),
                  r_in],
        out_specs=(pl.BlockSpec((TM_IN, TN_IN), lambda j, i: (i, j)), r_out),
        scratch_shapes=[pltpu.VMEM((k, TN_IN), BF16)],
        compiler_params=_cparams(("arbitrary", "arbitrary")),
        name="in_proj",
    )(x, w, wr)


def _gated_kernel(*refs, silu, has_rider):
    if has_rider:
        x_ref, wa_ref, wb_ref, wr_ref, o_ref, wrb_ref, wbf_ref = refs
    else:
        x_ref, wa_ref, wb_ref, o_ref, wbf_ref = refs

    @pl.when(pl.program_id(1) == 0)
    def _():
        wbf_ref[:, 0:TN] = wa_ref[...].astype(BF16)
        wbf_ref[:, TN:2 * TN] = wb_ref[...].astype(BF16)

    ab = jnp.dot(x_ref[...], wbf_ref[...], preferred_element_type=F32)
    a = ab[:, 0:TN]
    b = ab[:, TN:2 * TN]
    if silu:
        a = a * (1.0 / (1.0 + jnp.exp(-a)))
    o_ref[...] = (a * b).astype(o_ref.dtype)
    if has_rider:
        wrb_ref[...] = _pack_bf16(wr_ref[...])


def _gated(x, wa, wb, wr, layer, a_col0, b_col0, n_cols, silu, out_dtype, name):
    k = x.shape[1]
    a0 = a_col0 // TN
    b0 = b_col0 // TN
    n_outer, n_inner = n_cols // TN, M_ALL // TM_BIG
    in_specs = [pl.BlockSpec((TM_BIG, k), lambda j, i: (i, 0)),
                pl.BlockSpec((None, k, TN), lambda j, i: (layer, 0, a0 + j)),
                pl.BlockSpec((None, k, TN), lambda j, i: (layer, 0, b0 + j))]
    out_specs = [pl.BlockSpec((TM_BIG, TN), lambda j, i: (i, j))]
    out_shape = [jax.ShapeDtypeStruct((M_ALL, n_cols), out_dtype)]
    args = [x, wa, wb]
    if wr is not None:
        r_in, r_out, r_shape = _rider(wr, layer, n_outer, n_inner)
        in_specs.append(r_in)
        out_specs.append(r_out)
        out_shape.append(r_shape)
        args.append(wr)
    return pl.pallas_call(
        functools.partial(_gated_kernel, silu=silu, has_rider=wr is not None),
        out_shape=tuple(out_shape),
        grid=(n_outer, n_inner),
        in_specs=in_specs,
        out_specs=tuple(out_specs),
        scratch_shapes=[pltpu.VMEM((k, 2 * TN), BF16)],
        compiler_params=_cparams(("arbitrary", "arbitrary")),
        name=name,
    )(*args)


def _proj_norm_kernel(a_ref, w_ref, hin_ref, gpost_ref, gnext_ref, h_ref, xn_ref, *, l_post, l_next):
    g_post = gpost_ref[l_post:l_post + 1, :]
    g_next = gnext_ref[l_next:l_next + 1, :]
    for c in range(TM_PROJ // RC_PROJ):
        rows = slice(c * RC_PROJ, (c + 1) * RC_PROJ)
        y = jnp.dot(a_ref[rows, :], _unpack_bf16(w_ref[...]), preferred_element_type=F32)
        hn = hin_ref[rows, :] + _rms(y, g_post)
        h_ref[rows, :] = hn
        xn_ref[rows, :] = _rms(hn, g_next).astype(BF16)


def _proj_norm(a, w, layer, hin, g_post, g_next, l_next, name):
    k = a.shape[1]
    return pl.pallas_call(
        functools.partial(_proj_norm_kernel, l_post=layer, l_next=l_next),
        out_shape=(jax.ShapeDtypeStruct((M_ALL, D_MODEL), F32),
                   jax.ShapeDtypeStruct((M_ALL, D_MODEL), BF16)),
        grid=(M_ALL // TM_PROJ,),
        in_specs=[pl.BlockSpec((TM_PROJ, k), lambda i: (i, 0)),
                  pl.BlockSpec((k // 2, D_MODEL), lambda i: (0, 0), pipeline_mode=pl.Buffered(1)),
                  pl.BlockSpec((TM_PROJ, D_MODEL), lambda i: (i, 0)),
                  pl.BlockSpec((DEPTH, D_MODEL), lambda i: (0, 0)),
                  pl.BlockSpec((DEPTH, D_MODEL), lambda i: (0, 0))],
        out_specs=(pl.BlockSpec((TM_PROJ, D_MODEL), lambda i: (i, 0)),
                   pl.BlockSpec((TM_PROJ, D_MODEL), lambda i: (i, 0))),
        compiler_params=_cparams(("parallel",)),
        name=name,
    )(a, w, hin, g_post, g_next)


def _proj_last_kernel(a_ref, w_ref, hin_ref, gpost_ref, y_ref, tail_ref, *, l_post):
    g_post = gpost_ref[l_post:l_post + 1, :]
    for c in range(TM_PROJ // RC_PROJ):
        rows = slice(c * RC_PROJ, (c + 1) * RC_PROJ)
        y = jnp.dot(a_ref[rows, :], _unpack_bf16(w_ref[...]), preferred_element_type=F32)
        y_ref[rows, :] = hin_ref[rows, :] + _rms(y, g_post)

    @pl.when(pl.program_id(0) == N_PROJ - 1)
    def _():
        tail_ref[...] = y_ref[PROJ_LAST_X:TM_PROJ, :]


def _proj_last(a, w, layer, hin, g_post):
    k = a.shape[1]
    return pl.pallas_call(
        functools.partial(_proj_last_kernel, l_post=layer),
        out_shape=(jax.ShapeDtypeStruct((M_X, D_MODEL), F32),
                   jax.ShapeDtypeStruct((BLOCK, D_MODEL), F32)),
        grid=(N_PROJ,),
        in_specs=[pl.BlockSpec((TM_PROJ, k), lambda i: (i, 0)),
                  pl.BlockSpec((k // 2, D_MODEL), lambda i: (0, 0), pipeline_mode=pl.Buffered(1)),
                  pl.BlockSpec((TM_PROJ, D_MODEL), lambda i: (i, 0)),
                  pl.BlockSpec((DEPTH, D_MODEL), lambda i: (0, 0))],
        out_specs=(pl.BlockSpec((TM_PROJ, D_MODEL), lambda i: (i, 0)),
                   pl.BlockSpec((BLOCK, D_MODEL), lambda i: (0, 0))),
        compiler_params=_cparams(("arbitrary",)),
        name="ffn_down_last",
    )(a, w, hin, g_post)


def _band_bias_kernel(tab_ref, bucket_ref, valid_ref, o_ref):
    h = pl.program_id(0)
    bucket = bucket_ref[...]
    acc = jnp.zeros((BLOCK, 2 * BLOCK), F32)
    for b in range(NUM_BUCKETS):
        acc = jnp.where(bucket == b, tab_ref[b, h], acc)
    acc = acc * LOG2E
    for v in range(3):
        o_ref[v, 0] = jnp.where(valid_ref[v] != 0, acc, NEG)


def _band_bias(rel_bias, bucket, valid):
    return pl.pallas_call(
        _band_bias_kernel,
        out_shape=jax.ShapeDtypeStruct((3, N_HEADS, BLOCK, 2 * BLOCK), F32),
        grid=(N_HEADS,),
        in_specs=[pl.BlockSpec(memory_space=pltpu.SMEM),
                  pl.BlockSpec((BLOCK, 2 * BLOCK), lambda h: (0, 0)),
                  pl.BlockSpec((3, BLOCK, 2 * BLOCK), lambda h: (0, 0, 0))],
        out_specs=pl.BlockSpec((3, 1, BLOCK, 2 * BLOCK), lambda h: (0, h, 0, 0)),
        compiler_params=_cparams(("arbitrary",)),
        name="band_bias",
    )(rel_bias, bucket, valid)


NB_MIX = 2
TM_MIX = NB_MIX * BLOCK
N_GRP = -(-(N_XBLK + 1) // NB_MIX)


def _mix_block(b, kv_prev, u_prev8, is_tail, bias_ref, sink_ref, z_ref, u_ref, cw_ref, ext_ref,
               mix_ref):
    rows = slice(b * BLOCK, (b + 1) * BLOCK)

    lo_kv = lax.broadcasted_iota(jnp.int32, (2 * BLOCK, BLOCK), 1) < HEAD_DIM
    one_lo = jnp.where(lo_kv, 1.0, 0.0).astype(BF16)
    one_hi = jnp.where(lo_kv, 0.0, 1.0).astype(BF16)
    col_cache = {}

    def kv_tables(kh):
        col, in_hi = kh // 2, kh % 2
        if col not in col_cache:
            c0 = col * BLOCK
            kc = jnp.concatenate([kv_prev[:, c0:c0 + BLOCK],
                                  z_ref[rows, Q_DIM + c0:Q_DIM + c0 + BLOCK]], axis=0)
            vc = jnp.concatenate([kv_prev[:, KV_DIM + c0:KV_DIM + c0 + BLOCK],
                                  z_ref[rows, Q_DIM + KV_DIM + c0:Q_DIM + KV_DIM + c0 + BLOCK]], axis=0)
            col_cache[col] = (kc, vc, pltpu.roll(kc, HEAD_DIM, 1), pltpu.roll(vc, HEAD_DIM, 1))
        kc, vc, ks, vs = col_cache[col]
        own_k, oth_k = (ks, kc) if in_hi else (kc, ks)
        own_v, oth_v = (vs, vc) if in_hi else (vc, vs)
        k2 = jnp.where(lo_kv, own_k, oth_k).astype(BF16)
        rv = jnp.concatenate(
            [jnp.concatenate([jnp.where(lo_kv, own_v, 0.0).astype(BF16), one_lo], axis=1),
             jnp.concatenate([jnp.where(lo_kv, 0.0, oth_v).astype(BF16), one_hi], axis=1)], axis=0)
        return k2, rv

    lane = lax.broadcasted_iota(jnp.int32, (BLOCK, BLOCK), 1)
    lo_half = lane < HEAD_DIM
    zero = jnp.zeros((BLOCK, BLOCK), BF16)
    k2, rv = {}, {}
    for p in range(N_HEADS // 2):
        kh = p // 2
        if kh not in k2:
            k2[kh], rv[kh] = kv_tables(kh)
        qp = (z_ref[rows, p * BLOCK:(p + 1) * BLOCK] * (SCALE * LOG2E)).astype(BF16)
        es, ms, sks = [], [], []
        for half in range(2):
            h = 2 * p + half
            qm = jnp.where(lo_half if half == 0 else jnp.logical_not(lo_half), qp, zero)
            s = lax.dot_general(qm, k2[kh], (((1,), (1,)), ((), ())),
                                preferred_element_type=F32) + bias_ref[0, h]
            sk = sink_ref[h] * LOG2E
            m = jnp.maximum(jnp.max(s, axis=-1, keepdims=True), sk)
            es.append(jnp.exp2(s - m).astype(BF16))
            ms.append(m)
            sks.append(sk)
        acc = jnp.dot(jnp.concatenate(es, axis=1), rv[kh], preferred_element_type=F32)
        sink_term = jnp.exp2(jnp.where(lo_half, sks[0], sks[1]) - jnp.where(lo_half, ms[0], ms[1]))
        den = acc[:, BLOCK:] + sink_term
        mix_ref[rows, p * BLOCK:(p + 1) * BLOCK] = (acc[:, :BLOCK] / den).astype(BF16)

    ext_ref[b, 0:8, :] = u_prev8
    ext_ref[b, 8:8 + BLOCK, :] = u_ref[rows, :]
    u1 = ext_ref[b, 7:7 + BLOCK, :]
    u2 = ext_ref[b, 6:6 + BLOCK, :]
    if is_tail is not None:
        row = lax.broadcasted_iota(jnp.int32, (BLOCK, CONV_DIM), 0)
        first = jnp.where(is_tail, META_ROW0, -8)
        u1 = jnp.where(row >= first + 1, u1, 0.0)
        u2 = jnp.where(row >= first + 2, u2, 0.0)
    gb = z_ref[rows, Q_DIM + 2 * KV_DIM:Z1_DIM]
    c = gb * (cw_ref[0:1, :] * u2 + cw_ref[1:2, :] * u1 + cw_ref[2:3, :] * u_ref[rows, :])
    mix_ref[rows, Q_DIM:] = c.astype(BF16)


def _mixer_kernel(sink_ref, z_ref, zp_ref, u_ref, up_ref, bias0_ref, bias_ref, cw_ref, mixs_ref,
                  w_ref, hin_ref, hint_ref, gpost_ref, gnext_ref,
                  h_ref, xn_ref, kp_ref, vp_ref, cp_ref, ext_ref, mix_ref, lhs_ref, *, layer):
    t = pl.program_id(0)
    i0 = jnp.minimum(t, N_GRP - 1) * NB_MIX
    is_tail = i0 == N_XBLK

    @pl.when(t == 0)
    def _():
        mix_ref[...] = jnp.zeros_like(mix_ref)

    lhs_ref[...] = mix_ref[...]
    half_n = D_MODEL // 2
    y_parts = []

    for b in range(NB_MIX):
        y_parts.append(jnp.dot(lhs_ref[...], _unpack_bf16(w_ref[:, b * half_n:(b + 1) * half_n]),
                               preferred_element_type=F32))
        if b == 0:
            _mix_block(0, zp_ref[...], up_ref[...], is_tail, bias0_ref, sink_ref, z_ref, u_ref,
                       cw_ref, ext_ref, mix_ref)
        else:
            prev = slice((b - 1) * BLOCK, b * BLOCK)
            _mix_block(b, z_ref[prev, Q_DIM:Q_DIM + 2 * KV_DIM], u_ref[b * BLOCK - 8:b * BLOCK, :],
                       None, bias_ref, sink_ref, z_ref, u_ref, cw_ref, ext_ref, mix_ref)

    y = jnp.concatenate(y_parts, axis=1)
    hin = hin_ref[...]
    if layer == 0:
        hin = jnp.where(t - 1 == N_GRP - 1, jnp.concatenate([hint_ref[...]] * NB_MIX, axis=0), hin)
    hn = hin + _rms(y, gpost_ref[layer:layer + 1, :])
    h_ref[...] = hn
    xn_ref[...] = _rms(hn, gnext_ref[layer:layer + 1, :]).astype(BF16)

    @pl.when(is_tail)
    def _():
        mix_ref[0:DEC_BATCH, :] = mixs_ref[...].astype(BF16)

    last = i0 + NB_MIX - 1
    @pl.when(jnp.logical_and(last < N_XBLK, last % BLK_PER_SEQ == BLK_PER_SEQ - 1))
    def _():
        rows = slice((NB_MIX - 1) * BLOCK, NB_MIX * BLOCK)
        kp_ref[0] = z_ref[rows, Q_DIM:Q_DIM + KV_DIM].T
        vp_ref[0] = z_ref[rows, Q_DIM + KV_DIM:Q_DIM + 2 * KV_DIM].T
        cp_ref[0] = u_ref[TM_MIX - (CONV_WIDTH - 1):TM_MIX, :]


def _prev_blk(i):
    return jnp.where(i % BLK_PER_SEQ == 0, N_XBLK, i - 1)


def _bias_variant(i):
    return jnp.where(i == N_XBLK, 0, jnp.where(i % BLK_PER_SEQ == 0, 1, 2))


def _mixer(layer, sinks, z1, u, bias3, cw, mix_s, w_out_b, hin, hin_tail, g_post, g_next):
    assert BLK_PER_SEQ % NB_MIX == 0
    rows8 = BLOCK // 8
    grp = lambda t: jnp.minimum(t, N_GRP - 1)
    blk0 = lambda t: grp(t) * NB_MIX
    prj = lambda t: jnp.maximum(t - 1, 0)
    n_hin = -(-hin.shape[0] // TM_MIX)
    seq_of = lambda t: jnp.minimum(blk0(t) // BLK_PER_SEQ, BATCH - 1)
    bias_spec = lambda f, **kw: pl.BlockSpec((1, N_HEADS, BLOCK, 2 * BLOCK), f, **kw)
    return pl.pallas_call(
        functools.partial(_mixer_kernel, layer=layer),
        out_shape=(jax.ShapeDtypeStruct((M_ALL, D_MODEL), F32),
                   jax.ShapeDtypeStruct((M_ALL, D_MODEL), BF16),
                   jax.ShapeDtypeStruct((BATCH, KV_DIM, WINDOW), F32),
                   jax.ShapeDtypeStruct((BATCH, KV_DIM, WINDOW), F32),
                   jax.ShapeDtypeStruct((BATCH, CONV_WIDTH - 1, CONV_DIM), F32)),
        grid=(N_GRP + 1,),
        in_specs=[pl.BlockSpec(memory_space=pltpu.SMEM),
                  pl.BlockSpec((TM_MIX, Z1_DIM), lambda t: (grp(t), 0)),
                  pl.BlockSpec((BLOCK, 2 * KV_DIM),
                               lambda t: (_prev_blk(blk0(t)), Q_DIM // (2 * KV_DIM))),
                  pl.BlockSpec((TM_MIX, CONV_DIM), lambda t: (grp(t), 0)),
                  pl.BlockSpec((8, CONV_DIM), lambda t: (_prev_blk(blk0(t)) * rows8 + rows8 - 1, 0)),
                  bias_spec(lambda t: (_bias_variant(blk0(t)), 0, 0, 0)),
                  bias_spec(lambda t: (2, 0, 0, 0), pipeline_mode=pl.Buffered(1)),
                  pl.BlockSpec((CONV_WIDTH, CONV_DIM), lambda t: (0, 0)),
                  pl.BlockSpec((DEC_BATCH, D_MODEL), lambda t: (0, 0)),
                  pl.BlockSpec((D_MODEL // 2, D_MODEL), lambda t: (0, 0), pipeline_mode=pl.Buffered(1)),
                  pl.BlockSpec((TM_MIX, D_MODEL), lambda t: (jnp.minimum(prj(t), n_hin - 1), 0)),
                  pl.BlockSpec((BLOCK, D_MODEL), lambda t: (hin_tail.shape[0] // BLOCK - 1, 0)),
                  pl.BlockSpec((DEPTH, D_MODEL), lambda t: (0, 0)),
                  pl.BlockSpec((DEPTH, D_MODEL), lambda t: (0, 0))],
        out_specs=(pl.BlockSpec((TM_MIX, D_MODEL), lambda t: (prj(t), 0)),
                   pl.BlockSpec((TM_MIX, D_MODEL), lambda t: (prj(t), 0)),
                   pl.BlockSpec((1, KV_DIM, WINDOW), lambda t: (seq_of(t), 0, 0)),
                   pl.BlockSpec((1, KV_DIM, WINDOW), lambda t: (seq_of(t), 0, 0)),
                   pl.BlockSpec((1, CONV_WIDTH - 1, CONV_DIM), lambda t: (seq_of(t), 0, 0))),
        scratch_shapes=[pltpu.VMEM((NB_MIX, BLOCK + 8, CONV_DIM), F32),
                        pltpu.VMEM((TM_MIX, D_MODEL), BF16),
                        pltpu.VMEM((TM_MIX, D_MODEL), BF16)],
        compiler_params=_cparams(("arbitrary",)),
        name="mixer",
    )(sinks, z1, z1, u, u, bias3, bias3, cw, mix_s, w_out_b, hin, hin_tail, g_post, g_next)


S_CHUNK = 16
SH_ROWS = S_CHUNK * N_HEADS


def _sample_kernel(zs_ref, us_ref, kvt_ref, kc_ref, vc_ref, st_ref, cw_ref, sinkt_ref, tabt_ref,
                   brow_ref, fold_ref, unfold_ref, kall_ref, vall_ref,
                   mixs_ref, kout_ref, vout_ref, cs_ref, *, layer):
    del kall_ref, vall_ref
    z = zs_ref[...]
    q = z[:, 0:Q_DIM] * SCALE
    qrep = jnp.concatenate([jnp.broadcast_to(q[s:s + 1, :], (N_HEADS, Q_DIM))
                            for s in range(S_CHUNK)], axis=0)
    row_head = lax.broadcasted_iota(jnp.int32, (SH_ROWS, Q_DIM), 0) % N_HEADS
    own_q = lax.broadcasted_iota(jnp.int32, (SH_ROWS, Q_DIM), 1) // HEAD_DIM == row_head
    qblk = jnp.dot(jnp.where(own_q, qrep, 0.0).astype(BF16), fold_ref[...],
                   preferred_element_type=F32)
    own_kv = (lax.broadcasted_iota(jnp.int32, (N_HEADS, KV_DIM), 1) // HEAD_DIM
              == lax.broadcasted_iota(jnp.int32, (N_HEADS, KV_DIM), 0) // GROUP)

    brow = brow_ref[...]
    tabt = tabt_ref[...]
    bias_t = jnp.zeros((N_HEADS, W_BUF), F32)
    for b in range(NUM_BUCKETS):
        bias_t = jnp.where(brow == b, tabt[:, b:b + 1], bias_t)
    bias0 = tabt[:, 0:1]
    sink = sinkt_ref[:, layer:layer + 1]
    last_key = lax.broadcasted_iota(jnp.int32, (KV_DIM, W_BUF), 1) == W_BUF - 1

    tile_s = lambda x: jnp.concatenate([x] * S_CHUNK, axis=0)
    rep_s = lambda c0, c1: jnp.concatenate(
        [jnp.broadcast_to(z[s:s + 1, c0:c1], (N_HEADS, c1 - c0)) for s in range(S_CHUNK)], axis=0)
    sample_rows = lambda s: slice(s * KV_DIM, (s + 1) * KV_DIM)
    head_rows = lambda s: slice(s * N_HEADS, (s + 1) * N_HEADS)

    qblk_b = qblk.astype(BF16)
    sc = jnp.concatenate(
        [jnp.dot(qblk_b[head_rows(s), :], kc_ref[sample_rows(s), :].astype(BF16),
                 preferred_element_type=F32) for s in range(S_CHUNK)], axis=0) + tile_s(bias_t)
    sn = (jnp.sum(qblk * rep_s(Q_DIM, Q_DIM + KV_DIM), axis=-1, keepdims=True)
          + tile_s(bias0))
    sink_all = tile_s(sink)
    m = jnp.maximum(jnp.maximum(jnp.max(sc, axis=-1, keepdims=True), sn), sink_all)
    e = jnp.exp(sc - m)
    en = jnp.exp(sn - m)
    den = jnp.sum(e, axis=-1, keepdims=True) + en + jnp.exp(sink_all - m)
    p = (e / den).astype(BF16)
    o_all = jnp.concatenate(
        [lax.dot_general(p[head_rows(s), :], vc_ref[sample_rows(s), :].astype(BF16),
                         (((1,), (1,)), ((), ())), preferred_element_type=F32)
         for s in range(S_CHUNK)], axis=0)
    o_all = jnp.where(tile_s(own_kv), o_all + (en / den) * rep_s(Q_DIM + KV_DIM, Q_DIM + 2 * KV_DIM),
                      0.0)

    for s in range(S_CHUNK):
        rows = sample_rows(s)
        kout_ref[rows, :] = jnp.where(last_key, kvt_ref[0, 0:KV_DIM, s:s + 1],
                                      pltpu.roll(kc_ref[rows, :], W_BUF - 1, 1))
        vout_ref[rows, :] = jnp.where(last_key, kvt_ref[0, KV_DIM:2 * KV_DIM, s:s + 1],
                                      pltpu.roll(vc_ref[rows, :], W_BUF - 1, 1))

    a_all = jnp.dot(o_all.astype(BF16), unfold_ref[...],
                    preferred_element_type=F32)
    a_all = jnp.where(own_q, a_all, 0.0)
    mixs_ref[:, 0:Q_DIM] = jnp.sum(a_all.reshape(S_CHUNK, N_HEADS, Q_DIM), axis=1)

    gb = z[:, Q_DIM + 2 * KV_DIM:Z1_DIM]
    u = us_ref[...]
    s0 = st_ref[:, 0:CONV_DIM]
    s1 = st_ref[:, CONV_DIM:]
    mixs_ref[:, Q_DIM:] = gb * (cw_ref[0:1, :] * s0 + cw_ref[1:2, :] * s1 + cw_ref[2:3, :] * u)
    cs_ref[:, 0:CONV_DIM] = s1
    cs_ref[:, CONV_DIM:] = u


def _sample_mixer(layer, z1, u, kvt, cache_kt, cache_vt, state2d, cw, sinkt, tabt, brow,
                  fold, unfold, k_all, v_all):
    n_chunks = DEC_BATCH // S_CHUNK
    row_blk0 = M_X // S_CHUNK
    full = lambda shape: pl.BlockSpec(shape, lambda c: (0,) * len(shape))
    n_in = 14
    return pl.pallas_call(
        functools.partial(_sample_kernel, layer=layer),
        out_shape=(jax.ShapeDtypeStruct((DEC_BATCH, D_MODEL), F32),
                   jax.ShapeDtypeStruct(k_all.shape, F32),
                   jax.ShapeDtypeStruct(v_all.shape, F32),
                   jax.ShapeDtypeStruct((DEC_BATCH, 2 * CONV_DIM), F32)),
        input_output_aliases={n_in - 2: 1, n_in - 1: 2},
        grid=(n_chunks,),
        in_specs=[pl.BlockSpec((S_CHUNK, Z1_DIM), lambda c: (row_blk0 + c, 0)),
                  pl.BlockSpec((S_CHUNK, CONV_DIM), lambda c: (row_blk0 + c, 0)),
                  pl.BlockSpec((1, 2 * KV_DIM, S_CHUNK), lambda c: (c, 0, 0)),
                  pl.BlockSpec((S_CHUNK * KV_DIM, W_BUF), lambda c: (layer * n_chunks + c, 0)),
                  pl.BlockSpec((S_CHUNK * KV_DIM, W_BUF), lambda c: (layer * n_chunks + c, 0)),
                  pl.BlockSpec((S_CHUNK, 2 * CONV_DIM), lambda c: (layer * n_chunks + c, 0)),
                  full((CONV_WIDTH, CONV_DIM)),
                  full((N_HEADS, DEPTH)),
                  full((N_HEADS, NUM_BUCKETS)),
                  full((1, W_BUF)),
                  full((Q_DIM, KV_DIM)),
                  full((KV_DIM, Q_DIM)),
                  pl.BlockSpec(memory_space=pl.ANY),
                  pl.BlockSpec(memory_space=pl.ANY)],
        out_specs=(pl.BlockSpec((S_CHUNK, D_MODEL), lambda c: (c, 0)),
                   pl.BlockSpec((S_CHUNK * KV_DIM, W_BUF), lambda c: (layer * n_chunks + c, 0)),
                   pl.BlockSpec((S_CHUNK * KV_DIM, W_BUF), lambda c: (layer * n_chunks + c, 0)),
                   pl.BlockSpec((S_CHUNK, 2 * CONV_DIM), lambda c: (c, 0))),
        compiler_params=_cparams(("arbitrary",)),
        name="sample_mixer",
    )(z1, u, kvt, cache_kt, cache_vt, state2d, cw, sinkt, tabt, brow, fold, unfold, k_all, v_all)


def _t5_bucket(d):
    max_exact = NUM_BUCKETS // 2
    df = jnp.maximum(d, 1).astype(F32)
    large = max_exact + (jnp.log(df / max_exact) / math.log(MAX_DISTANCE / max_exact)
                         * (NUM_BUCKETS - max_exact)).astype(jnp.int32)
    large = jnp.minimum(large, NUM_BUCKETS - 1)
    return jnp.where(d < max_exact, d, large)


def _band_tables():
    i = np.arange(BLOCK)[:, None]
    j = np.arange(2 * BLOCK)[None, :]
    d = BLOCK + i - j
    band = (d >= 0) & (d <= WINDOW)
    valid = np.stack([band & (j >= BLOCK + META_ROW0), band & (j >= META_ROW0), band])
    bucket = _t5_bucket(jnp.asarray(np.maximum(d, 0), jnp.int32))
    return bucket, jnp.asarray(valid.astype(np.int32))


def _fold_tables():
    d = np.arange(HEAD_DIM)
    fold = np.zeros((Q_DIM, KV_DIM), np.float32)
    for h in range(N_HEADS):
        fold[h * HEAD_DIM + d, (h // GROUP) * HEAD_DIM + d] = 1.0
    return jnp.asarray(fold, BF16), jnp.asarray(fold.T, BF16)


def kernel(x_prompt, x_sample, cache_k, cache_v, state_conv, meta_tokens, rel_bias, w_in, conv_w,
           attn_sinks, w_out, norm_pre_mix, norm_post_mix, norm_pre_ffn, norm_post_ffn,
           w_gate, w_up, w_down):
    tail = jnp.concatenate([x_sample.reshape(DEC_BATCH, D_MODEL),
                            jnp.zeros((META_ROW0 - DEC_BATCH, D_MODEL), F32),
                            meta_tokens.astype(F32)], axis=0)

    bucket, valid = _band_tables()
    fold, unfold = _fold_tables()
    bias3 = _band_bias(rel_bias, bucket, valid)
    brow = _t5_bucket(jnp.asarray(W_BUF - np.arange(W_BUF), jnp.int32)).reshape(1, W_BUF)

    kv_lanes = lambda c: jnp.transpose(c, (0, 1, 3, 4, 2)).reshape(DEPTH * DEC_BATCH * KV_DIM, W_BUF)
    cache_kt = kv_lanes(cache_k)
    cache_vt = kv_lanes(cache_v)
    state2d = state_conv.reshape(DEPTH * DEC_BATCH, 2 * CONV_DIM)

    x2d = x_prompt.reshape(M_X, D_MODEL)
    xn = _norm(x2d, tail, norm_pre_mix)
    h, h_tail = x2d, tail

    def kv_rows(t, n):
        return jnp.transpose(t.reshape(n, N_KV_HEADS, HEAD_DIM, WINDOW), (0, 3, 1, 2))

    k_all = jnp.zeros(cache_kt.shape, F32)
    v_all = jnp.zeros(cache_vt.shape, F32)
    kp, vp, cp, cs = [], [], [], []
    for l in range(DEPTH):
        z1, w_out_b = _matmul(xn, w_in, w_out, l, Z1_DIM)
        (u,) = _gated(xn, w_in, w_in, None, l, Z1_DIM, Z1_DIM + CONV_DIM, CONV_DIM,
                      False, F32, "conv_in")
        kvt = jnp.transpose(z1[M_X:M_X + DEC_BATCH, Q_DIM:Q_DIM + 2 * KV_DIM]
                            .reshape(DEC_BATCH // S_CHUNK, S_CHUNK, 2 * KV_DIM), (0, 2, 1))
        mix_s, k_all, v_all, c_new = _sample_mixer(
            l, z1, u, kvt, cache_kt, cache_vt, state2d, conv_w[l], attn_sinks.T, rel_bias.T,
            brow, fold, unfold, k_all, v_all)
        h, xn, kp_l, vp_l, cp_l = _mixer(l, attn_sinks[l], z1, u, bias3, conv_w[l], mix_s,
                                         w_out_b, h, h_tail, norm_post_mix, norm_pre_ffn)
        act, w_down_b = _gated(xn, w_gate, w_up, w_down, l, 0, 0, D_FF, True, BF16, "ffn_up")
        if l + 1 < DEPTH:
            h, xn = _proj_norm(act, w_down_b, l, h, norm_post_ffn, norm_pre_mix, l + 1, "ffn_down")
            h_tail = h
        else:
            y2d, y_tail = _proj_last(act, w_down_b, l, h, norm_post_ffn)

        kp.append(kv_rows(kp_l, BATCH))
        vp.append(kv_rows(vp_l, BATCH))
        cp.append(cp_l)
        cs.append(c_new.reshape(DEC_BATCH, CONV_WIDTH - 1, CONV_DIM))

    y_prompt = y2d.reshape(BATCH, SEQ, D_MODEL)
    y_sample = y_tail[0:DEC_BATCH].reshape(DEC_BATCH, 1, D_MODEL)
    k_sample = kv_rows(k_all, DEPTH * DEC_BATCH).reshape(DEPTH, DEC_BATCH, W_BUF, N_KV_HEADS, HEAD_DIM)
    v_sample = kv_rows(v_all, DEPTH * DEC_BATCH).reshape(DEPTH, DEC_BATCH, W_BUF, N_KV_HEADS, HEAD_DIM)
    return (y_prompt, y_sample, jnp.stack(kp), jnp.stack(vp), jnp.stack(cp),
            k_sample, v_sample, jnp.stack(cs))
```

```python
import functools
import math

import numpy as np
import jax
import jax.numpy as jnp
from jax import lax
from jax.experimental import pallas as pl
from jax.experimental.pallas import tpu as pltpu

D_MODEL = 2048
BATCH = 4
SEQ = 2048
DEPTH = 4
DEC_BATCH = 32
HEAD_DIM = 64
N_HEADS = 16
N_KV_HEADS = 4
GROUP = 4
Q_DIM = 1024
KV_DIM = 256
CONV_DIM = 1024
CONV_WIDTH = 3
IN_DIM = 4608
WINDOW = 128
BLOCK = 128
NUM_BUCKETS = 32
MAX_DISTANCE = 128
N_META = 16
D_FF = 5632
RMS_EPS = 1e-6
SCALE = HEAD_DIM ** -0.5
W_BUF = 128

M_X = BATCH * SEQ
N_XBLK = M_X // BLOCK
BLK_PER_SEQ = SEQ // BLOCK
M_ALL = M_X + BLOCK
META_ROW0 = BLOCK - N_META
Z1_DIM = Q_DIM + 2 * KV_DIM + CONV_DIM
NEG = -1e30
LOG2E = math.log2(math.e)

TM_BIG = 1664
TM_EPI = 640
N_EPI = M_ALL // TM_EPI
EPI_LAST_X = M_X - (N_EPI - 1) * TM_EPI
assert (M_X - EPI_LAST_X) % EPI_LAST_X == 0 and EPI_LAST_X + BLOCK == TM_EPI
TM_PROJ = 416
RC_PROJ = 208
N_PROJ = M_ALL // TM_PROJ
PROJ_LAST_X = M_X - (N_PROJ - 1) * TM_PROJ
assert PROJ_LAST_X + BLOCK == TM_PROJ and PROJ_LAST_X % 8 == 0
TN = 512
GATED_CHUNKS = 13
TM_IN = 832
TN_IN = 1280
VMEM_LIMIT = 56 * 1024 * 1024

F32 = jnp.float32
BF16 = jnp.bfloat16


def _cparams(sem):
    return pltpu.CompilerParams(dimension_semantics=sem, vmem_limit_bytes=VMEM_LIMIT)


def _rms(x, g):
    return x * lax.rsqrt(jnp.mean(x * x, axis=-1, keepdims=True) + RMS_EPS) * g


def _norm_kernel(xa_ref, xb_ref, tail_ref, g_ref, xn_ref):
    last = pl.program_id(0) == N_EPI - 1
    h_last = jnp.concatenate([xb_ref[...], tail_ref[...]], axis=0)
    h = jnp.where(last, h_last, xa_ref[...])
    xn_ref[...] = _rms(h, g_ref[0:1, :]).astype(BF16)


def _norm(x2d, tail, g):
    return pl.pallas_call(
        _norm_kernel,
        out_shape=jax.ShapeDtypeStruct((M_ALL, D_MODEL), BF16),
        grid=(N_EPI,),
        in_specs=[pl.BlockSpec((TM_EPI, D_MODEL), lambda i: (jnp.minimum(i, N_EPI - 2), 0)),
                  pl.BlockSpec((EPI_LAST_X, D_MODEL), lambda i: ((M_X - EPI_LAST_X) // EPI_LAST_X, 0)),
                  pl.BlockSpec((BLOCK, D_MODEL), lambda i: (0, 0)),
                  pl.BlockSpec((DEPTH, D_MODEL), lambda i: (0, 0))],
        out_specs=pl.BlockSpec((TM_EPI, D_MODEL), lambda i: (i, 0)),
        compiler_params=_cparams(("arbitrary",)),
        name="norm0",
    )(x2d, x2d, tail, g)


RIDER_ROWS = 128


def _pack_bf16(w_f32):
    return pltpu.bitcast(w_f32.astype(BF16), jnp.uint32)


def _unpack_bf16(w_u32):
    return pltpu.bitcast(w_u32, BF16)


def _rider(wr, layer, n_outer, n_inner):
    n_chunks = wr.shape[1] // RIDER_ROWS
    assert n_chunks <= n_outer * n_inner
    chunk = lambda j, i: jnp.minimum(j * n_inner + i, n_chunks - 1)
    in_spec = pl.BlockSpec((None, RIDER_ROWS, D_MODEL), lambda j, i: (layer, chunk(j, i), 0))
    out_spec = pl.BlockSpec((RIDER_ROWS // 2, D_MODEL), lambda j, i: (chunk(j, i), 0))
    return in_spec, out_spec, jax.ShapeDtypeStruct((wr.shape[1] // 2, D_MODEL), jnp.uint32)


def _matmul_kernel(x_ref, w_ref, wr_ref, o_ref, wrb_ref, wbf_ref):
    @pl.when(pl.program_id(1) == 0)
    def _():
        wbf_ref[...] = w_ref[...].astype(BF16)

    o_ref[...] = jnp.dot(x_ref[...], wbf_ref[...], preferred_element_type=F32)
    wrb_ref[...] = _pack_bf16(wr_ref[...])


def _matmul(x, w, wr, layer, n_cols):
    k = x.shape[1]
    n_outer, n_inner = n_cols // TN_IN, M_ALL // TM_IN
    r_in, r_out, r_shape = _rider(wr, layer, n_outer, n_inner)
    return pl.pallas_call(
        _matmul_kernel,
        out_shape=(jax.ShapeDtypeStruct((M_ALL, n_cols), F32), r_shape),
        grid=(n_outer, n_inner),
        in_specs=[pl.BlockSpec((TM_IN, k), lambda j, i: (i, 0)),
                  pl.BlockSpec((None, k, TN_IN), lambda j, i: (layer, 0, j)),
                  r_in],
        out_specs=(pl.BlockSpec((TM_IN, TN_IN), lambda j, i: (i, j)), r_out),
        scratch_shapes=[pltpu.VMEM((k, TN_IN), BF16)],
        compiler_params=_cparams(("arbitrary", "arbitrary")),
        name="in_proj",
    )(x, w, wr)


def _gated_kernel(*refs, silu, has_rider):
    if has_rider:
        x_ref, wa_ref, wb_ref, wr_ref, o_ref, wrb_ref, wbf_ref = refs
    else:
        x_ref, wa_ref, wb_ref, o_ref, wbf_ref = refs

    @pl.when(pl.program_id(1) == 0)
    def _():
        wbf_ref[:, 0:TN] = wa_ref[...].astype(BF16)
        wbf_ref[:, TN:2 * TN] = wb_ref[...].astype(BF16)

    rc = TM_BIG // GATED_CHUNKS
    for c in range(GATED_CHUNKS):
        rows = slice(c * rc, (c + 1) * rc)
        ab = jnp.dot(x_ref[rows, :], wbf_ref[...], preferred_element_type=F32)
        a = ab[:, 0:TN]
        b = ab[:, TN:2 * TN]
        if silu:
            a = a * (1.0 / (1.0 + jnp.exp(-a)))
        o_ref[rows, :] = (a * b).astype(o_ref.dtype)
    if has_rider:
        wrb_ref[...] = _pack_bf16(wr_ref[...])


def _gated(x, wa, wb, wr, layer, a_col0, b_col0, n_cols, silu, out_dtype, name):
    k = x.shape[1]
    a0 = a_col0 // TN
    b0 = b_col0 // TN
    n_outer, n_inner = n_cols // TN, M_ALL // TM_BIG
    in_specs = [pl.BlockSpec((TM_BIG, k), lambda j, i: (i, 0)),
                pl.BlockSpec((None, k, TN), lambda j, i: (layer, 0, a0 + j)),
                pl.BlockSpec((None, k, TN), lambda j, i: (layer, 0, b0 + j))]
    out_specs = [pl.BlockSpec((TM_BIG, TN), lambda j, i: (i, j))]
    out_shape = [jax.ShapeDtypeStruct((M_ALL, n_cols), out_dtype)]
    args = [x, wa, wb]
    if wr is not None:
        r_in, r_out, r_shape = _rider(wr, layer, n_outer, n_inner)
        in_specs.append(r_in)
        out_specs.append(r_out)
        out_shape.append(r_shape)
        args.append(wr)
    return pl.pallas_call(
        functools.partial(_gated_kernel, silu=silu, has_rider=wr is not None),
        out_shape=tuple(out_shape),
        grid=(n_outer, n_inner),
        in_specs=in_specs,
        out_specs=tuple(out_specs),
        scratch_shapes=[pltpu.VMEM((k, 2 * TN), BF16)],
        compiler_params=_cparams(("arbitrary", "arbitrary")),
        name=name,
    )(*args)


def _proj_norm_kernel(a_ref, w_ref, hin_ref, gpost_ref, gnext_ref, h_ref, xn_ref, *, l_post, l_next):
    g_post = gpost_ref[l_post:l_post + 1, :]
    g_next = gnext_ref[l_next:l_next + 1, :]
    for c in range(TM_PROJ // RC_PROJ):
        rows = slice(c * RC_PROJ, (c + 1) * RC_PROJ)
        y = jnp.dot(a_ref[rows, :], _unpack_bf16(w_ref[...]), preferred_element_type=F32)
        hn = hin_ref[rows, :] + _rms(y, g_post)
        h_ref[rows, :] = hn
        xn_ref[rows, :] = _rms(hn, g_next).astype(BF16)


def _proj_norm(a, w, layer, hin, g_post, g_next, l_next, name):
    k = a.shape[1]
    return pl.pallas_call(
        functools.partial(_proj_norm_kernel, l_post=layer, l_next=l_next),
        out_shape=(jax.ShapeDtypeStruct((M_ALL, D_MODEL), F32),
                   jax.ShapeDtypeStruct((M_ALL, D_MODEL), BF16)),
        grid=(M_ALL // TM_PROJ,),
        in_specs=[pl.BlockSpec((TM_PROJ, k), lambda i: (i, 0)),
                  pl.BlockSpec((k // 2, D_MODEL), lambda i: (0, 0), pipeline_mode=pl.Buffered(1)),
                  pl.BlockSpec((TM_PROJ, D_MODEL), lambda i: (i, 0)),
                  pl.BlockSpec((DEPTH, D_MODEL), lambda i: (0, 0)),
                  pl.BlockSpec((DEPTH, D_MODEL), lambda i: (0, 0))],
        out_specs=(pl.BlockSpec((TM_PROJ, D_MODEL), lambda i: (i, 0)),
                   pl.BlockSpec((TM_PROJ, D_MODEL), lambda i: (i, 0))),
        compiler_params=_cparams(("parallel",)),
        name=name,
    )(a, w, hin, g_post, g_next)


def _proj_last_kernel(a_ref, w_ref, hin_ref, gpost_ref, y_ref, tail_ref, *, l_post):
    g_post = gpost_ref[l_post:l_post + 1, :]
    for c in range(TM_PROJ // RC_PROJ):
        rows = slice(c * RC_PROJ, (c + 1) * RC_PROJ)
        y = jnp.dot(a_ref[rows, :], _unpack_bf16(w_ref[...]), preferred_element_type=F32)
        y_ref[rows, :] = hin_ref[rows, :] + _rms(y, g_post)

    @pl.when(pl.program_id(0) == N_PROJ - 1)
    def _():
        tail_ref[...] = y_ref[PROJ_LAST_X:TM_PROJ, :]


def _proj_last(a, w, layer, hin, g_post):
    k = a.shape[1]
    return pl.pallas_call(
        functools.partial(_proj_last_kernel, l_post=layer),
        out_shape=(jax.ShapeDtypeStruct((M_X, D_MODEL), F32),
                   jax.ShapeDtypeStruct((BLOCK, D_MODEL), F32)),
        grid=(N_PROJ,),
        in_specs=[pl.BlockSpec((TM_PROJ, k), lambda i: (i, 0)),
                  pl.BlockSpec((k // 2, D_MODEL), lambda i: (0, 0), pipeline_mode=pl.Buffered(1)),
                  pl.BlockSpec((TM_PROJ, D_MODEL), lambda i: (i, 0)),
                  pl.BlockSpec((DEPTH, D_MODEL), lambda i: (0, 0))],
        out_specs=(pl.BlockSpec((TM_PROJ, D_MODEL), lambda i: (i, 0)),
                   pl.BlockSpec((BLOCK, D_MODEL), lambda i: (0, 0))),
        compiler_params=_cparams(("arbitrary",)),
        name="ffn_down_last",
    )(a, w, hin, g_post)


def _band_bias_kernel(tab_ref, bucket_ref, valid_ref, o_ref):
    h = pl.program_id(0)
    bucket = bucket_ref[...]
    acc = jnp.zeros((BLOCK, 2 * BLOCK), F32)
    for b in range(NUM_BUCKETS):
        acc = jnp.where(bucket == b, tab_ref[b, h], acc)
    acc = acc * LOG2E
    for v in range(3):
        o_ref[v, 0] = jnp.where(valid_ref[v] != 0, acc, NEG)


def _band_bias(rel_bias, bucket, valid):
    return pl.pallas_call(
        _band_bias_kernel,
        out_shape=jax.ShapeDtypeStruct((3, N_HEADS, BLOCK, 2 * BLOCK), F32),
        grid=(N_HEADS,),
        in_specs=[pl.BlockSpec(memory_space=pltpu.SMEM),
                  pl.BlockSpec((BLOCK, 2 * BLOCK), lambda h: (0, 0)),
                  pl.BlockSpec((3, BLOCK, 2 * BLOCK), lambda h: (0, 0, 0))],
        out_specs=pl.BlockSpec((3, 1, BLOCK, 2 * BLOCK), lambda h: (0, h, 0, 0)),
        compiler_params=_cparams(("arbitrary",)),
        name="band_bias",
    )(rel_bias, bucket, valid)


NB_MIX = 2
TM_MIX = NB_MIX * BLOCK
N_GRP = -(-(N_XBLK + 1) // NB_MIX)


def _mix_block(b, kv_prev, u_prev8, is_tail, bias_ref, sink_ref, z_ref, u_ref, cw_ref, ext_ref,
               mix_ref):
    rows = slice(b * BLOCK, (b + 1) * BLOCK)

    lo_kv = lax.broadcasted_iota(jnp.int32, (2 * BLOCK, BLOCK), 1) < HEAD_DIM
    one_lo = jnp.where(lo_kv, 1.0, 0.0).astype(BF16)
    one_hi = jnp.where(lo_kv, 0.0, 1.0).astype(BF16)
    col_cache = {}

    def kv_tables(kh):
        col, in_hi = kh // 2, kh % 2
        if col not in col_cache:
            c0 = col * BLOCK
            kc = jnp.concatenate([kv_prev[:, c0:c0 + BLOCK],
                                  z_ref[rows, Q_DIM + c0:Q_DIM + c0 + BLOCK]], axis=0)
            vc = jnp.concatenate([kv_prev[:, KV_DIM + c0:KV_DIM + c0 + BLOCK],
                                  z_ref[rows, Q_DIM + KV_DIM + c0:Q_DIM + KV_DIM + c0 + BLOCK]], axis=0)
            col_cache[col] = (kc, vc, pltpu.roll(kc, HEAD_DIM, 1), pltpu.roll(vc, HEAD_DIM, 1))
        kc, vc, ks, vs = col_cache[col]
        own_k, oth_k = (ks, kc) if in_hi else (kc, ks)
        own_v, oth_v = (vs, vc) if in_hi else (vc, vs)
        k2 = jnp.where(lo_kv, own_k, oth_k).astype(BF16)
        rv = jnp.concatenate(
            [jnp.concatenate([jnp.where(lo_kv, own_v, 0.0).astype(BF16), one_lo], axis=1),
             jnp.concatenate([jnp.where(lo_kv, 0.0, oth_v).astype(BF16), one_hi], axis=1)], axis=0)
        return k2, rv

    lane = lax.broadcasted_iota(jnp.int32, (BLOCK, BLOCK), 1)
    lo_half = lane < HEAD_DIM
    zero = jnp.zeros((BLOCK, BLOCK), BF16)
    k2, rv = {}, {}
    for p in range(N_HEADS // 2):
        kh = p // 2
        if kh not in k2:
            k2[kh], rv[kh] = kv_tables(kh)
        qp = (z_ref[rows, p * BLOCK:(p + 1) * BLOCK] * (SCALE * LOG2E)).astype(BF16)
        es, ms, sks = [], [], []
        for half in range(2):
            h = 2 * p + half
            qm = jnp.where(lo_half if half == 0 else jnp.logical_not(lo_half), qp, zero)
            s = lax.dot_general(qm, k2[kh], (((1,), (1,)), ((), ())),
                                preferred_element_type=F32) + bias_ref[0, h]
            sk = sink_ref[h] * LOG2E
            m = jnp.maximum(jnp.max(s, axis=-1, keepdims=True), sk)
            es.append(jnp.exp2(s - m).astype(BF16))
            ms.append(m)
            sks.append(sk)
        acc = jnp.dot(jnp.concatenate(es, axis=1), rv[kh], preferred_element_type=F32)
        sink_term = jnp.exp2(jnp.where(lo_half, sks[0], sks[1]) - jnp.where(lo_half, ms[0], ms[1]))
        den = acc[:, BLOCK:] + sink_term
        mix_ref[rows, p * BLOCK:(p + 1) * BLOCK] = (acc[:, :BLOCK] / den).astype(BF16)

    ext_ref[b, 0:8, :] = u_prev8
    ext_ref[b, 8:8 + BLOCK, :] = u_ref[rows, :]
    u1 = ext_ref[b, 7:7 + BLOCK, :]
    u2 = ext_ref[b, 6:6 + BLOCK, :]
    if is_tail is not None:
        row = lax.broadcasted_iota(jnp.int32, (BLOCK, CONV_DIM), 0)
        first = jnp.where(is_tail, META_ROW0, -8)
        u1 = jnp.where(row >= first + 1, u1, 0.0)
        u2 = jnp.where(row >= first + 2, u2, 0.0)
    gb = z_ref[rows, Q_DIM + 2 * KV_DIM:Z1_DIM]
    c = gb * (cw_ref[0:1, :] * u2 + cw_ref[1:2, :] * u1 + cw_ref[2:3, :] * u_ref[rows, :])
    mix_ref[rows, Q_DIM:] = c.astype(BF16)


def _mixer_kernel(sink_ref, z_ref, zp_ref, u_ref, up_ref, bias0_ref, bias_ref, cw_ref, mixs_ref,
                  w_ref, hin_ref, hint_ref, gpost_ref, gnext_ref,
                  h_ref, xn_ref, kp_ref, vp_ref, cp_ref, ext_ref, mix_ref, lhs_ref, *, layer):
    t = pl.program_id(0)
    i0 = jnp.minimum(t, N_GRP - 1) * NB_MIX
    is_tail = i0 == N_XBLK

    @pl.when(t == 0)
    def _():
        mix_ref[...] = jnp.zeros_like(mix_ref)

    lhs_ref[...] = mix_ref[...]
    half_n = D_MODEL // 2
    y_parts = []

    for b in range(NB_MIX):
        y_parts.append(jnp.dot(lhs_ref[...], _unpack_bf16(w_ref[:, b * half_n:(b + 1) * half_n]),
                               preferred_element_type=F32))
        if b == 0:
            _mix_block(0, zp_ref[...], up_ref[...], is_tail, bias0_ref, sink_ref, z_ref, u_ref,
                       cw_ref, ext_ref, mix_ref)
        else:
            prev = slice((b - 1) * BLOCK, b * BLOCK)
            _mix_block(b, z_ref[prev, Q_DIM:Q_DIM + 2 * KV_DIM], u_ref[b * BLOCK - 8:b * BLOCK, :],
                       None, bias_ref, sink_ref, z_ref, u_ref, cw_ref, ext_ref, mix_ref)

    y = jnp.concatenate(y_parts, axis=1)
    hin = hin_ref[...]
    if layer == 0:
        hin = jnp.where(t - 1 == N_GRP - 1, jnp.concatenate([hint_ref[...]] * NB_MIX, axis=0), hin)
    hn = hin + _rms(y, gpost_ref[layer:layer + 1, :])
    h_ref[...] = hn
    xn_ref[...] = _rms(hn, gnext_ref[layer:layer + 1, :]).astype(BF16)

    @pl.when(is_tail)
    def _():
        mix_ref[0:DEC_BATCH, :] = mixs_ref[...].astype(BF16)

    last = i0 + NB_MIX - 1
    @pl.when(jnp.logical_and(last < N_XBLK, last % BLK_PER_SEQ == BLK_PER_SEQ - 1))
    def _():
        rows = slice((NB_MIX - 1) * BLOCK, NB_MIX * BLOCK)
        kp_ref[0] = z_ref[rows, Q_DIM:Q_DIM + KV_DIM].T
        vp_ref[0] = z_ref[rows, Q_DIM + KV_DIM:Q_DIM + 2 * KV_DIM].T
        cp_ref[0] = u_ref[TM_MIX - (CONV_WIDTH - 1):TM_MIX, :]


def _prev_blk(i):
    return jnp.where(i % BLK_PER_SEQ == 0, N_XBLK, i - 1)


def _bias_variant(i):
    return jnp.where(i == N_XBLK, 0, jnp.where(i % BLK_PER_SEQ == 0, 1, 2))


def _mixer(layer, sinks, z1, u, bias3, cw, mix_s, w_out_b, hin, hin_tail, g_post, g_next):
    assert BLK_PER_SEQ % NB_MIX == 0
    rows8 = BLOCK // 8
    grp = lambda t: jnp.minimum(t, N_GRP - 1)
    blk0 = lambda t: grp(t) * NB_MIX
    prj = lambda t: jnp.maximum(t - 1, 0)
    n_hin = -(-hin.shape[0] // TM_MIX)
    seq_of = lambda t: jnp.minimum(blk0(t) // BLK_PER_SEQ, BATCH - 1)
    bias_spec = lambda f, **kw: pl.BlockSpec((1, N_HEADS, BLOCK, 2 * BLOCK), f, **kw)
    return pl.pallas_call(
        functools.partial(_mixer_kernel, layer=layer),
        out_shape=(jax.ShapeDtypeStruct((M_ALL, D_MODEL), F32),
                   jax.ShapeDtypeStruct((M_ALL, D_MODEL), BF16),
                   jax.ShapeDtypeStruct((BATCH, KV_DIM, WINDOW), F32),
                   jax.ShapeDtypeStruct((BATCH, KV_DIM, WINDOW), F32),
                   jax.ShapeDtypeStruct((BATCH, CONV_WIDTH - 1, CONV_DIM), F32)),
        grid=(N_GRP + 1,),
        in_specs=[pl.BlockSpec(memory_space=pltpu.SMEM),
                  pl.BlockSpec((TM_MIX, Z1_DIM), lambda t: (grp(t), 0)),
                  pl.BlockSpec((BLOCK, 2 * KV_DIM),
                               lambda t: (_prev_blk(blk0(t)), Q_DIM // (2 * KV_DIM))),
                  pl.BlockSpec((TM_MIX, CONV_DIM), lambda t: (grp(t), 0)),
                  pl.BlockSpec((8, CONV_DIM), lambda t: (_prev_blk(blk0(t)) * rows8 + rows8 - 1, 0)),
                  bias_spec(lambda t: (_bias_variant(blk0(t)), 0, 0, 0)),
                  bias_spec(lambda t: (2, 0, 0, 0), pipeline_mode=pl.Buffered(1)),
                  pl.BlockSpec((CONV_WIDTH, CONV_DIM), lambda t: (0, 0)),
                  pl.BlockSpec((DEC_BATCH, D_MODEL), lambda t: (0, 0)),
                  pl.BlockSpec((D_MODEL // 2, D_MODEL), lambda t: (0, 0), pipeline_mode=pl.Buffered(1)),
                  pl.BlockSpec((TM_MIX, D_MODEL), lambda t: (jnp.minimum(prj(t), n_hin - 1), 0)),
                  pl.BlockSpec((BLOCK, D_MODEL), lambda t: (hin_tail.shape[0] // BLOCK - 1, 0)),
                  pl.BlockSpec((DEPTH, D_MODEL), lambda t: (0, 0)),
                  pl.BlockSpec((DEPTH, D_MODEL), lambda t: (0, 0))],
        out_specs=(pl.BlockSpec((TM_MIX, D_MODEL), lambda t: (prj(t), 0)),
                   pl.BlockSpec((TM_MIX, D_MODEL), lambda t: (prj(t), 0)),
                   pl.BlockSpec((1, KV_DIM, WINDOW), lambda t: (seq_of(t), 0, 0)),
                   pl.BlockSpec((1, KV_DIM, WINDOW), lambda t: (seq_of(t), 0, 0)),
                   pl.BlockSpec((1, CONV_WIDTH - 1, CONV_DIM), lambda t: (seq_of(t), 0, 0))),
        scratch_shapes=[pltpu.VMEM((NB_MIX, BLOCK + 8, CONV_DIM), F32),
                        pltpu.VMEM((TM_MIX, D_MODEL), BF16),
                        pltpu.VMEM((TM_MIX, D_MODEL), BF16)],
        compiler_params=_cparams(("arbitrary",)),
        name="mixer",
    )(sinks, z1, z1, u, u, bias3, bias3, cw, mix_s, w_out_b, hin, hin_tail, g_post, g_next)


S_CHUNK = 16
SH_ROWS = S_CHUNK * N_HEADS


def _sample_kernel(zs_ref, us_ref, kvt_ref, kc_ref, vc_ref, st_ref, cw_ref, sinkt_ref, tabt_ref,
                   brow_ref, fold_ref, unfold_ref, kall_ref, vall_ref,
                   mixs_ref, kout_ref, vout_ref, cs_ref, *, layer):
    del kall_ref, vall_ref
    z = zs_ref[...]
    q = z[:, 0:Q_DIM] * SCALE
    qrep = jnp.concatenate([jnp.broadcast_to(q[s:s + 1, :], (N_HEADS, Q_DIM))
                            for s in range(S_CHUNK)], axis=0)
    row_head = lax.broadcasted_iota(jnp.int32, (SH_ROWS, Q_DIM), 0) % N_HEADS
    own_q = lax.broadcasted_iota(jnp.int32, (SH_ROWS, Q_DIM), 1) // HEAD_DIM == row_head
    qblk = jnp.dot(jnp.where(own_q, qrep, 0.0).astype(BF16), fold_ref[...],
                   preferred_element_type=F32)
    own_kv = (lax.broadcasted_iota(jnp.int32, (N_HEADS, KV_DIM), 1) // HEAD_DIM
              == lax.broadcasted_iota(jnp.int32, (N_HEADS, KV_DIM), 0) // GROUP)

    brow = brow_ref[...]
    tabt = tabt_ref[...]
    bias_t = jnp.zeros((N_HEADS, W_BUF), F32)
    for b in range(NUM_BUCKETS):
        bias_t = jnp.where(brow == b, tabt[:, b:b + 1], bias_t)
    bias0 = tabt[:, 0:1]
    sink = sinkt_ref[:, layer:layer + 1]
    last_key = lax.broadcasted_iota(jnp.int32, (KV_DIM, W_BUF), 1) == W_BUF - 1

    tile_s = lambda x: jnp.concatenate([x] * S_CHUNK, axis=0)
    rep_s = lambda c0, c1: jnp.concatenate(
        [jnp.broadcast_to(z[s:s + 1, c0:c1], (N_HEADS, c1 - c0)) for s in range(S_CHUNK)], axis=0)
    sample_rows = lambda s: slice(s * KV_DIM, (s + 1) * KV_DIM)
    head_rows = lambda s: slice(s * N_HEADS, (s + 1) * N_HEADS)

    qblk_b = qblk.astype(BF16)
    sc = jnp.concatenate(
        [jnp.dot(qblk_b[head_rows(s), :], kc_ref[sample_rows(s), :].astype(BF16),
                 preferred_element_type=F32) for s in range(S_CHUNK)], axis=0) + tile_s(bias_t)
    sn = (jnp.sum(qblk * rep_s(Q_DIM, Q_DIM + KV_DIM), axis=-1, keepdims=True)
          + tile_s(bias0))
    sink_all = tile_s(sink)
    m = jnp.maximum(jnp.maximum(jnp.max(sc, axis=-1, keepdims=True), sn), sink_all)
    e = jnp.exp(sc - m)
    en = jnp.exp(sn - m)
    den = jnp.sum(e, axis=-1, keepdims=True) + en + jnp.exp(sink_all - m)
    p = (e / den).astype(BF16)
    o_all = jnp.concatenate(
        [lax.dot_general(p[head_rows(s), :], vc_ref[sample_rows(s), :].astype(BF16),
                         (((1,), (1,)), ((), ())), preferred_element_type=F32)
         for s in range(S_CHUNK)], axis=0)
    o_all = jnp.where(tile_s(own_kv), o_all + (en / den) * rep_s(Q_DIM + KV_DIM, Q_DIM + 2 * KV_DIM),
                      0.0)

    for s in range(S_CHUNK):
        rows = sample_rows(s)
        kout_ref[rows, :] = jnp.where(last_key, kvt_ref[0, 0:KV_DIM, s:s + 1],
                                      pltpu.roll(kc_ref[rows, :], W_BUF - 1, 1))
        vout_ref[rows, :] = jnp.where(last_key, kvt_ref[0, KV_DIM:2 * KV_DIM, s:s + 1],
                                      pltpu.roll(vc_ref[rows, :], W_BUF - 1, 1))

    a_all = jnp.dot(o_all.astype(BF16), unfold_ref[...],
                    preferred_element_type=F32)
    a_all = jnp.where(own_q, a_all, 0.0)
    mixs_ref[:, 0:Q_DIM] = jnp.sum(a_all.reshape(S_CHUNK, N_HEADS, Q_DIM), axis=1)

    gb = z[:, Q_DIM + 2 * KV_DIM:Z1_DIM]
    u = us_ref[...]
    s0 = st_ref[:, 0:CONV_DIM]
    s1 = st_ref[:, CONV_DIM:]
    mixs_ref[:, Q_DIM:] = gb * (cw_ref[0:1, :] * s0 + cw_ref[1:2, :] * s1 + cw_ref[2:3, :] * u)
    cs_ref[:, 0:CONV_DIM] = s1
    cs_ref[:, CONV_DIM:] = u


def _sample_mixer(layer, z1, u, kvt, cache_kt, cache_vt, state2d, cw, sinkt, tabt, brow,
                  fold, unfold, k_all, v_all):
    n_chunks = DEC_BATCH // S_CHUNK
    row_blk0 = M_X // S_CHUNK
    full = lambda shape: pl.BlockSpec(shape, lambda c: (0,) * len(shape))
    n_in = 14
    return pl.pallas_call(
        functools.partial(_sample_kernel, layer=layer),
        out_shape=(jax.ShapeDtypeStruct((DEC_BATCH, D_MODEL), F32),
                   jax.ShapeDtypeStruct(k_all.shape, F32),
                   jax.ShapeDtypeStruct(v_all.shape, F32),
                   jax.ShapeDtypeStruct((DEC_BATCH, 2 * CONV_DIM), F32)),
        input_output_aliases={n_in - 2: 1, n_in - 1: 2},
        grid=(n_chunks,),
        in_specs=[pl.BlockSpec((S_CHUNK, Z1_DIM), lambda c: (row_blk0 + c, 0)),
                  pl.BlockSpec((S_CHUNK, CONV_DIM), lambda c: (row_blk0 + c, 0)),
                  pl.BlockSpec((1, 2 * KV_DIM, S_CHUNK), lambda c: (c, 0, 0)),
                  pl.BlockSpec((S_CHUNK * KV_DIM, W_BUF), lambda c: (layer * n_chunks + c, 0)),
                  pl.BlockSpec((S_CHUNK * KV_DIM, W_BUF), lambda c: (layer * n_chunks + c, 0)),
                  pl.BlockSpec((S_CHUNK, 2 * CONV_DIM), lambda c: (layer * n_chunks + c, 0)),
                  full((CONV_WIDTH, CONV_DIM)),
                  full((N_HEADS, DEPTH)),
                  full((N_HEADS, NUM_BUCKETS)),
                  full((1, W_BUF)),
                  full((Q_DIM, KV_DIM)),
                  full((KV_DIM, Q_DIM)),
                  pl.BlockSpec(memory_space=pl.ANY),
                  pl.BlockSpec(memory_space=pl.ANY)],
        out_specs=(pl.BlockSpec((S_CHUNK, D_MODEL), lambda c: (c, 0)),
                   pl.BlockSpec((S_CHUNK * KV_DIM, W_BUF), lambda c: (layer * n_chunks + c, 0)),
                   pl.BlockSpec((S_CHUNK * KV_DIM, W_BUF), lambda c: (layer * n_chunks + c, 0)),
                   pl.BlockSpec((S_CHUNK, 2 * CONV_DIM), lambda c: (c, 0))),
        compiler_params=_cparams(("arbitrary",)),
        name="sample_mixer",
    )(z1, u, kvt, cache_kt, cache_vt, state2d, cw, sinkt, tabt, brow, fold, unfold, k_all, v_all)


def _t5_bucket(d):
    max_exact = NUM_BUCKETS // 2
    df = jnp.maximum(d, 1).astype(F32)
    large = max_exact + (jnp.log(df / max_exact) / math.log(MAX_DISTANCE / max_exact)
                         * (NUM_BUCKETS - max_exact)).astype(jnp.int32)
    large = jnp.minimum(large, NUM_BUCKETS - 1)
    return jnp.where(d < max_exact, d, large)


def _band_tables():
    i = np.arange(BLOCK)[:, None]
    j = np.arange(2 * BLOCK)[None, :]
    d = BLOCK + i - j
    band = (d >= 0) & (d <= WINDOW)
    valid = np.stack([band & (j >= BLOCK + META_ROW0), band & (j >= META_ROW0), band])
    bucket = _t5_bucket(jnp.asarray(np.maximum(d, 0), jnp.int32))
    return bucket, jnp.asarray(valid.astype(np.int32))


def _fold_tables():
    d = np.arange(HEAD_DIM)
    fold = np.zeros((Q_DIM, KV_DIM), np.float32)
    for h in range(N_HEADS):
        fold[h * HEAD_DIM + d, (h // GROUP) * HEAD_DIM + d] = 1.0
    return jnp.asarray(fold, BF16), jnp.asarray(fold.T, BF16)


def kernel(x_prompt, x_sample, cache_k, cache_v, state_conv, meta_tokens, rel_bias, w_in, conv_w,
           attn_sinks, w_out, norm_pre_mix, norm_post_mix, norm_pre_ffn, norm_post_ffn,
           w_gate, w_up, w_down):
    tail = jnp.concatenate([x_sample.reshape(DEC_BATCH, D_MODEL),
                            jnp.zeros((META_ROW0 - DEC_BATCH, D_MODEL), F32),
                            meta_tokens.astype(F32)], axis=0)

    bucket, valid = _band_tables()
    fold, unfold = _fold_tables()
    bias3 = _band_bias(rel_bias, bucket, valid)
    brow = _t5_bucket(jnp.asarray(W_BUF - np.arange(W_BUF), jnp.int32)).reshape(1, W_BUF)

    kv_lanes = lambda c: jnp.transpose(c, (0, 1, 3, 4, 2)).reshape(DEPTH * DEC_BATCH * KV_DIM, W_BUF)
    cache_kt = kv_lanes(cache_k)
    cache_vt = kv_lanes(cache_v)
    state2d = state_conv.reshape(DEPTH * DEC_BATCH, 2 * CONV_DIM)

    x2d = x_prompt.reshape(M_X, D_MODEL)
    xn = _norm(x2d, tail, norm_pre_mix)
    h, h_tail = x2d, tail

    def kv_rows(t, n):
        return jnp.transpose(t.reshape(n, N_KV_HEADS, HEAD_DIM, WINDOW), (0, 3, 1, 2))

    k_all = jnp.zeros(cache_kt.shape, F32)
    v_all = jnp.zeros(cache_vt.shape, F32)
    kp, vp, cp, cs = [], [], [], []
    for l in range(DEPTH):
        z1, w_out_b = _matmul(xn, w_in, w_out, l, Z1_DIM)
        (u,) = _gated(xn, w_in, w_in, None, l, Z1_DIM, Z1_DIM + CONV_DIM, CONV_DIM,
                      False, F32, "conv_in")
        kvt = jnp.transpose(z1[M_X:M_X + DEC_BATCH, Q_DIM:Q_DIM + 2 * KV_DIM]
                            .reshape(DEC_BATCH // S_CHUNK, S_CHUNK, 2 * KV_DIM), (0, 2, 1))
        mix_s, k_all, v_all, c_new = _sample_mixer(
            l, z1, u, kvt, cache_kt, cache_vt, state2d, conv_w[l], attn_sinks.T, rel_bias.T,
            brow, fold, unfold, k_all, v_all)
        h, xn, kp_l, vp_l, cp_l = _mixer(l, attn_sinks[l], z1, u, bias3, conv_w[l], mix_s,
                                         w_out_b, h, h_tail, norm_post_mix, norm_pre_ffn)
        act, w_down_b = _gated(xn, w_gate, w_up, w_down, l, 0, 0, D_FF, True, BF16, "ffn_up")
        if l + 1 < DEPTH:
            h, xn = _proj_norm(act, w_down_b, l, h, norm_post_ffn, norm_pre_mix, l + 1, "ffn_down")
            h_tail = h
        else:
            y2d, y_tail = _proj_last(act, w_down_b, l, h, norm_post_ffn)

        kp.append(kv_rows(kp_l, BATCH))
        vp.append(kv_rows(vp_l, BATCH))
        cp.append(cp_l)
        cs.append(c_new.reshape(DEC_BATCH, CONV_WIDTH - 1, CONV_DIM))

    y_prompt = y2d.reshape(BATCH, SEQ, D_MODEL)
    y_sample = y_tail[0:DEC_BATCH].reshape(DEC_BATCH, 1, D_MODEL)
    k_sample = kv_rows(k_all, DEPTH * DEC_BATCH).reshape(DEPTH, DEC_BATCH, W_BUF, N_KV_HEADS, HEAD_DIM)
    v_sample = kv_rows(v_all, DEPTH * DEC_BATCH).reshape(DEPTH, DEC_BATCH, W_BUF, N_KV_HEADS, HEAD_DIM)
    return (y_prompt, y_sample, jnp.stack(kp), jnp.stack(vp), jnp.stack(cp),
            k_sample, v_sample, jnp.stack(cs))
```

```python
import functools
import math

import numpy as np
import jax
import jax.numpy as jnp
from jax import lax
from jax.experimental import pallas as pl
from jax.experimental.pallas import tpu as pltpu

D_MODEL = 2048
BATCH = 4
SEQ = 2048
DEPTH = 4
DEC_BATCH = 32
HEAD_DIM = 64
N_HEADS = 16
N_KV_HEADS = 4
GROUP = 4
Q_DIM = 1024
KV_DIM = 256
CONV_DIM = 1024
CONV_WIDTH = 3
IN_DIM = 4608
WINDOW = 128
BLOCK = 128
NUM_BUCKETS = 32
MAX_DISTANCE = 128
N_META = 16
D_FF = 5632
RMS_EPS = 1e-6
SCALE = HEAD_DIM ** -0.5
W_BUF = 128

M_X = BATCH * SEQ
N_XBLK = M_X // BLOCK
BLK_PER_SEQ = SEQ // BLOCK
M_ALL = M_X + BLOCK
META_ROW0 = BLOCK - N_META
Z1_DIM = Q_DIM + 2 * KV_DIM + CONV_DIM
NEG = -1e30
LOG2E = math.log2(math.e)

TM_BIG = 1664
TM_EPI = 640
N_EPI = M_ALL // TM_EPI
EPI_LAST_X = M_X - (N_EPI - 1) * TM_EPI
assert (M_X - EPI_LAST_X) % EPI_LAST_X == 0 and EPI_LAST_X + BLOCK == TM_EPI
TM_PROJ = 416
PROJ_ROW_CHUNKS = ((0, 144), (144, 288), (288, 416))
N_PROJ = M_ALL // TM_PROJ
PROJ_LAST_X = M_X - (N_PROJ - 1) * TM_PROJ
assert PROJ_LAST_X + BLOCK == TM_PROJ and PROJ_LAST_X % 8 == 0
TN = 512
GATED_CHUNKS = 13
TM_IN = 832
TN_IN = 1280
VMEM_LIMIT = 56 * 1024 * 1024

F32 = jnp.float32
BF16 = jnp.bfloat16


def _cparams(sem):
    return pltpu.CompilerParams(dimension_semantics=sem, vmem_limit_bytes=VMEM_LIMIT)


def _rms(x, g):
    return x * lax.rsqrt(jnp.mean(x * x, axis=-1, keepdims=True) + RMS_EPS) * g


def _norm_kernel(xa_ref, xb_ref, tail_ref, g_ref, xn_ref):
    last = pl.program_id(0) == N_EPI - 1
    h_last = jnp.concatenate([xb_ref[...], tail_ref[...]], axis=0)
    h = jnp.where(last, h_last, xa_ref[...])
    xn_ref[...] = _rms(h, g_ref[0:1, :]).astype(BF16)


def _norm(x2d, tail, g):
    return pl.pallas_call(
        _norm_kernel,
        out_shape=jax.ShapeDtypeStruct((M_ALL, D_MODEL), BF16),
        grid=(N_EPI,),
        in_specs=[pl.BlockSpec((TM_EPI, D_MODEL), lambda i: (jnp.minimum(i, N_EPI - 2), 0)),
                  pl.BlockSpec((EPI_LAST_X, D_MODEL), lambda i: ((M_X - EPI_LAST_X) // EPI_LAST_X, 0)),
                  pl.BlockSpec((BLOCK, D_MODEL), lambda i: (0, 0)),
                  pl.BlockSpec((DEPTH, D_MODEL), lambda i: (0, 0))],
        out_specs=pl.BlockSpec((TM_EPI, D_MODEL), lambda i: (i, 0)),
        compiler_params=_cparams(("arbitrary",)),
        name="norm0",
    )(x2d, x2d, tail, g)


RIDER_ROWS = 128


def _pack_bf16(w_f32):
    return pltpu.bitcast(w_f32.astype(BF16), jnp.uint32)


def _unpack_bf16(w_u32):
    return pltpu.bitcast(w_u32, BF16)


def _rider(wr, layer, n_outer, n_inner):
    n_chunks = wr.shape[1] // RIDER_ROWS
    assert n_chunks <= n_outer * n_inner
    chunk = lambda j, i: jnp.minimum(j * n_inner + i, n_chunks - 1)
    in_spec = pl.BlockSpec((None, RIDER_ROWS, D_MODEL), lambda j, i: (layer, chunk(j, i), 0))
    out_spec = pl.BlockSpec((RIDER_ROWS // 2, D_MODEL), lambda j, i: (chunk(j, i), 0))
    return in_spec, out_spec, jax.ShapeDtypeStruct((wr.shape[1] // 2, D_MODEL), jnp.uint32)


def _matmul_kernel(x_ref, w_ref, wr_ref, o_ref, wrb_ref, wbf_ref):
    @pl.when(pl.program_id(1) == 0)
    def _():
        wbf_ref[...] = w_ref[...].astype(BF16)

    o_ref[...] = jnp.dot(x_ref[...], wbf_ref[...], preferred_element_type=F32)
    wrb_ref[...] = _pack_bf16(wr_ref[...])


def _matmul(x, w, wr, layer, n_cols):
    k = x.shape[1]
    n_outer, n_inner = n_cols // TN_IN, M_ALL // TM_IN
    r_in, r_out, r_shape = _rider(wr, layer, n_outer, n_inner)
    return pl.pallas_call(
        _matmul_kernel,
        out_shape=(jax.ShapeDtypeStruct((M_ALL, n_cols), F32), r_shape),
        grid=(n_outer, n_inner),
        in_specs=[pl.BlockSpec((TM_IN, k), lambda j, i: (i, 0)),
                  pl.BlockSpec((None, k, TN_IN), lambda j, i: (layer, 0, j)),
                  r_in],
        out_specs=(pl.BlockSpec((TM_IN, TN_IN), lambda j, i: (i, j)), r_out),
        scratch_shapes=[pltpu.VMEM((k, TN_IN), BF16)],
        compiler_params=_cparams(("arbitrary", "arbitrary")),
        name="in_proj",
    )(x, w, wr)


def _gated_kernel(*refs, silu, has_rider):
    if has_rider:
        x_ref, wa_ref, wb_ref, wr_ref, o_ref, wrb_ref, wbf_ref = refs
    else:
        x_ref, wa_ref, wb_ref, o_ref, wbf_ref = refs

    @pl.when(pl.program_id(1) == 0)
    def _():
        wbf_ref[:, 0:TN] = wa_ref[...].astype(BF16)
        wbf_ref[:, TN:2 * TN] = wb_ref[...].astype(BF16)

    rc = TM_BIG // GATED_CHUNKS
    for c in range(GATED_CHUNKS):
        rows = slice(c * rc, (c + 1) * rc)
        ab = jnp.dot(x_ref[rows, :], wbf_ref[...], preferred_element_type=F32)
        a = ab[:, 0:TN]
        b = ab[:, TN:2 * TN]
        if silu:
            a = a * (1.0 / (1.0 + jnp.exp(-a)))
        o_ref[rows, :] = (a * b).astype(o_ref.dtype)
    if has_rider:
        wrb_ref[...] = _pack_bf16(wr_ref[...])


def _gated(x, wa, wb, wr, layer, a_col0, b_col0, n_cols, silu, out_dtype, name):
    k = x.shape[1]
    a0 = a_col0 // TN
    b0 = b_col0 // TN
    n_outer, n_inner = n_cols // TN, M_ALL // TM_BIG
    in_specs = [pl.BlockSpec((TM_BIG, k), lambda j, i: (i, 0)),
                pl.BlockSpec((None, k, TN), lambda j, i: (layer, 0, a0 + j)),
                pl.BlockSpec((None, k, TN), lambda j, i: (layer, 0, b0 + j))]
    out_specs = [pl.BlockSpec((TM_BIG, TN), lambda j, i: (i, j))]
    out_shape = [jax.ShapeDtypeStruct((M_ALL, n_cols), out_dtype)]
    args = [x, wa, wb]
    if wr is not None:
        r_in, r_out, r_shape = _rider(wr, layer, n_outer, n_inner)
        in_specs.append(r_in)
        out_specs.append(r_out)
        out_shape.append(r_shape)
        args.append(wr)
    return pl.pallas_call(
        functools.partial(_gated_kernel, silu=silu, has_rider=wr is not None),
        out_shape=tuple(out_shape),
        grid=(n_outer, n_inner),
        in_specs=in_specs,
        out_specs=tuple(out_specs),
        scratch_shapes=[pltpu.VMEM((k, 2 * TN), BF16)],
        compiler_params=_cparams(("arbitrary", "arbitrary")),
        name=name,
    )(*args)


def _proj_norm_kernel(a_ref, w_ref, hin_ref, gpost_ref, gnext_ref, h_ref, xn_ref, *, l_post, l_next):
    g_post = gpost_ref[l_post:l_post + 1, :]
    g_next = gnext_ref[l_next:l_next + 1, :]
    for r0, r1 in PROJ_ROW_CHUNKS:
        rows = slice(r0, r1)
        y = jnp.dot(a_ref[rows, :], _unpack_bf16(w_ref[...]), preferred_element_type=F32)
        hn = hin_ref[rows, :] + _rms(y, g_post)
        h_ref[rows, :] = hn
        xn_ref[rows, :] = _rms(hn, g_next).astype(BF16)


def _proj_norm(a, w, layer, hin, g_post, g_next, l_next, name):
    k = a.shape[1]
    return pl.pallas_call(
        functools.partial(_proj_norm_kernel, l_post=layer, l_next=l_next),
        out_shape=(jax.ShapeDtypeStruct((M_ALL, D_MODEL), F32),
                   jax.ShapeDtypeStruct((M_ALL, D_MODEL), BF16)),
        grid=(M_ALL // TM_PROJ,),
        in_specs=[pl.BlockSpec((TM_PROJ, k), lambda i: (i, 0)),
                  pl.BlockSpec((k // 2, D_MODEL), lambda i: (0, 0), pipeline_mode=pl.Buffered(1)),
                  pl.BlockSpec((TM_PROJ, D_MODEL), lambda i: (i, 0)),
                  pl.BlockSpec((DEPTH, D_MODEL), lambda i: (0, 0)),
                  pl.BlockSpec((DEPTH, D_MODEL), lambda i: (0, 0))],
        out_specs=(pl.BlockSpec((TM_PROJ, D_MODEL), lambda i: (i, 0)),
                   pl.BlockSpec((TM_PROJ, D_MODEL), lambda i: (i, 0))),
        compiler_params=_cparams(("parallel",)),
        name=name,
    )(a, w, hin, g_post, g_next)


def _proj_last_kernel(a_ref, w_ref, hin_ref, gpost_ref, y_ref, tail_ref, *, l_post):
    g_post = gpost_ref[l_post:l_post + 1, :]
    for r0, r1 in PROJ_ROW_CHUNKS:
        rows = slice(r0, r1)
        y = jnp.dot(a_ref[rows, :], _unpack_bf16(w_ref[...]), preferred_element_type=F32)
        y_ref[rows, :] = hin_ref[rows, :] + _rms(y, g_post)

    @pl.when(pl.program_id(0) == N_PROJ - 1)
    def _():
        tail_ref[...] = y_ref[PROJ_LAST_X:TM_PROJ, :]


def _proj_last(a, w, layer, hin, g_post):
    k = a.shape[1]
    return pl.pallas_call(
        functools.partial(_proj_last_kernel, l_post=layer),
        out_shape=(jax.ShapeDtypeStruct((M_X, D_MODEL), F32),
                   jax.ShapeDtypeStruct((BLOCK, D_MODEL), F32)),
        grid=(N_PROJ,),
        in_specs=[pl.BlockSpec((TM_PROJ, k), lambda i: (i, 0)),
                  pl.BlockSpec((k // 2, D_MODEL), lambda i: (0, 0), pipeline_mode=pl.Buffered(1)),
                  pl.BlockSpec((TM_PROJ, D_MODEL), lambda i: (i, 0)),
                  pl.BlockSpec((DEPTH, D_MODEL), lambda i: (0, 0))],
        out_specs=(pl.BlockSpec((TM_PROJ, D_MODEL), lambda i: (i, 0)),
                   pl.BlockSpec((BLOCK, D_MODEL), lambda i: (0, 0))),
        compiler_params=_cparams(("arbitrary",)),
        name="ffn_down_last",
    )(a, w, hin, g_post)


def _band_bias_kernel(tab_ref, bucket_ref, valid_ref, o_ref):
    h = pl.program_id(0)
    bucket = bucket_ref[...]
    acc = jnp.zeros((BLOCK, 2 * BLOCK), F32)
    for b in range(NUM_BUCKETS):
        acc = jnp.where(bucket == b, tab_ref[b, h], acc)
    acc = acc * LOG2E
    for v in range(3):
        o_ref[v, 0] = jnp.where(valid_ref[v] != 0, acc, NEG)


def _band_bias(rel_bias, bucket, valid):
    return pl.pallas_call(
        _band_bias_kernel,
        out_shape=jax.ShapeDtypeStruct((3, N_HEADS, BLOCK, 2 * BLOCK), F32),
        grid=(N_HEADS,),
        in_specs=[pl.BlockSpec(memory_space=pltpu.SMEM),
                  pl.BlockSpec((BLOCK, 2 * BLOCK), lambda h: (0, 0)),
                  pl.BlockSpec((3, BLOCK, 2 * BLOCK), lambda h: (0, 0, 0))],
        out_specs=pl.BlockSpec((3, 1, BLOCK, 2 * BLOCK), lambda h: (0, h, 0, 0)),
        compiler_params=_cparams(("arbitrary",)),
        name="band_bias",
    )(rel_bias, bucket, valid)


NB_MIX = 2
TM_MIX = NB_MIX * BLOCK
N_GRP = -(-(N_XBLK + 1) // NB_MIX)


def _mix_block(b, kv_prev, u_prev8, is_tail, bias_ref, sink_ref, z_ref, u_ref, cw_ref, ext_ref,
               mix_ref):
    rows = slice(b * BLOCK, (b + 1) * BLOCK)

    lo_kv = lax.broadcasted_iota(jnp.int32, (2 * BLOCK, BLOCK), 1) < HEAD_DIM
    one_lo = jnp.where(lo_kv, 1.0, 0.0).astype(BF16)
    one_hi = jnp.where(lo_kv, 0.0, 1.0).astype(BF16)
    col_cache = {}

    def kv_tables(kh):
        col, in_hi = kh // 2, kh % 2
        if col not in col_cache:
            c0 = col * BLOCK
            kc = jnp.concatenate([kv_prev[:, c0:c0 + BLOCK],
                                  z_ref[rows, Q_DIM + c0:Q_DIM + c0 + BLOCK]], axis=0)
            vc = jnp.concatenate([kv_prev[:, KV_DIM + c0:KV_DIM + c0 + BLOCK],
                                  z_ref[rows, Q_DIM + KV_DIM + c0:Q_DIM + KV_DIM + c0 + BLOCK]], axis=0)
            col_cache[col] = (kc, vc, pltpu.roll(kc, HEAD_DIM, 1), pltpu.roll(vc, HEAD_DIM, 1))
        kc, vc, ks, vs = col_cache[col]
        own_k, oth_k = (ks, kc) if in_hi else (kc, ks)
        own_v, oth_v = (vs, vc) if in_hi else (vc, vs)
        k2 = jnp.where(lo_kv, own_k, oth_k).astype(BF16)
        rv = jnp.concatenate(
            [jnp.concatenate([jnp.where(lo_kv, own_v, 0.0).astype(BF16), one_lo], axis=1),
             jnp.concatenate([jnp.where(lo_kv, 0.0, oth_v).astype(BF16), one_hi], axis=1)], axis=0)
        return k2, rv

    lane = lax.broadcasted_iota(jnp.int32, (BLOCK, BLOCK), 1)
    lo_half = lane < HEAD_DIM
    zero = jnp.zeros((BLOCK, BLOCK), BF16)
    k2, rv = {}, {}
    for p in range(N_HEADS // 2):
        kh = p // 2
        if kh not in k2:
            k2[kh], rv[kh] = kv_tables(kh)
        qp = (z_ref[rows, p * BLOCK:(p + 1) * BLOCK] * (SCALE * LOG2E)).astype(BF16)
        es, ms, sks = [], [], []
        for half in range(2):
            h = 2 * p + half
            qm = jnp.where(lo_half if half == 0 else jnp.logical_not(lo_half), qp, zero)
            s = lax.dot_general(qm, k2[kh], (((1,), (1,)), ((), ())),
                                preferred_element_type=F32) + bias_ref[0, h]
            sk = sink_ref[h] * LOG2E
            m = jnp.maximum(jnp.max(s, axis=-1, keepdims=True), sk)
            es.append(jnp.exp2(s - m).astype(BF16))
            ms.append(m)
            sks.append(sk)
        acc = jnp.dot(jnp.concatenate(es, axis=1), rv[kh], preferred_element_type=F32)
        sink_term = jnp.exp2(jnp.where(lo_half, sks[0], sks[1]) - jnp.where(lo_half, ms[0], ms[1]))
        den = acc[:, BLOCK:] + sink_term
        mix_ref[rows, p * BLOCK:(p + 1) * BLOCK] = (acc[:, :BLOCK] / den).astype(BF16)

    ext_ref[b, 0:8, :] = u_prev8
    ext_ref[b, 8:8 + BLOCK, :] = u_ref[rows, :]
    u1 = ext_ref[b, 7:7 + BLOCK, :]
    u2 = ext_ref[b, 6:6 + BLOCK, :]
    if is_tail is not None:
        row = lax.broadcasted_iota(jnp.int32, (BLOCK, CONV_DIM), 0)
        first = jnp.where(is_tail, META_ROW0, -8)
        u1 = jnp.where(row >= first + 1, u1, 0.0)
        u2 = jnp.where(row >= first + 2, u2, 0.0)
    gb = z_ref[rows, Q_DIM + 2 * KV_DIM:Z1_DIM]
    c = gb * (cw_ref[0:1, :] * u2 + cw_ref[1:2, :] * u1 + cw_ref[2:3, :] * u_ref[rows, :])
    mix_ref[rows, Q_DIM:] = c.astype(BF16)


def _mixer_kernel(sink_ref, z_ref, zp_ref, u_ref, up_ref, bias0_ref, bias_ref, cw_ref, mixs_ref,
                  w_ref, hin_ref, hint_ref, gpost_ref, gnext_ref,
                  h_ref, xn_ref, kp_ref, vp_ref, cp_ref, ext_ref, mix_ref, lhs_ref, *, layer):
    t = pl.program_id(0)
    i0 = jnp.minimum(t, N_GRP - 1) * NB_MIX
    is_tail = i0 == N_XBLK

    @pl.when(t == 0)
    def _():
        mix_ref[...] = jnp.zeros_like(mix_ref)

    lhs_ref[...] = mix_ref[...]
    half_n = D_MODEL // 2
    y_parts = []

    for b in range(NB_MIX):
        y_parts.append(jnp.dot(lhs_ref[...], _unpack_bf16(w_ref[:, b * half_n:(b + 1) * half_n]),
                               preferred_element_type=F32))
        if b == 0:
            _mix_block(0, zp_ref[...], up_ref[...], is_tail, bias0_ref, sink_ref, z_ref, u_ref,
                       cw_ref, ext_ref, mix_ref)
        else:
            prev = slice((b - 1) * BLOCK, b * BLOCK)
            _mix_block(b, z_ref[prev, Q_DIM:Q_DIM + 2 * KV_DIM], u_ref[b * BLOCK - 8:b * BLOCK, :],
                       None, bias_ref, sink_ref, z_ref, u_ref, cw_ref, ext_ref, mix_ref)

    y = jnp.concatenate(y_parts, axis=1)
    hin = hin_ref[...]
    if layer == 0:
        hin = jnp.where(t - 1 == N_GRP - 1, jnp.concatenate([hint_ref[...]] * NB_MIX, axis=0), hin)
    hn = hin + _rms(y, gpost_ref[layer:layer + 1, :])
    h_ref[...] = hn
    xn_ref[...] = _rms(hn, gnext_ref[layer:layer + 1, :]).astype(BF16)

    @pl.when(is_tail)
    def _():
        mix_ref[0:DEC_BATCH, :] = mixs_ref[...].astype(BF16)

    last = i0 + NB_MIX - 1
    @pl.when(jnp.logical_and(last < N_XBLK, last % BLK_PER_SEQ == BLK_PER_SEQ - 1))
    def _():
        rows = slice((NB_MIX - 1) * BLOCK, NB_MIX * BLOCK)
        kp_ref[0] = z_ref[rows, Q_DIM:Q_DIM + KV_DIM].T
        vp_ref[0] = z_ref[rows, Q_DIM + KV_DIM:Q_DIM + 2 * KV_DIM].T
        cp_ref[0] = u_ref[TM_MIX - (CONV_WIDTH - 1):TM_MIX, :]


def _prev_blk(i):
    return jnp.where(i % BLK_PER_SEQ == 0, N_XBLK, i - 1)


def _bias_variant(i):
    return jnp.where(i == N_XBLK, 0, jnp.where(i % BLK_PER_SEQ == 0, 1, 2))


def _mixer(layer, sinks, z1, u, bias3, cw, mix_s, w_out_b, hin, hin_tail, g_post, g_next):
    assert BLK_PER_SEQ % NB_MIX == 0
    rows8 = BLOCK // 8
    grp = lambda t: jnp.minimum(t, N_GRP - 1)
    blk0 = lambda t: grp(t) * NB_MIX
    prj = lambda t: jnp.maximum(t - 1, 0)
    n_hin = -(-hin.shape[0] // TM_MIX)
    seq_of = lambda t: jnp.minimum(blk0(t) // BLK_PER_SEQ, BATCH - 1)
    bias_spec = lambda f, **kw: pl.BlockSpec((1, N_HEADS, BLOCK, 2 * BLOCK), f, **kw)
    return pl.pallas_call(
        functools.partial(_mixer_kernel, layer=layer),
        out_shape=(jax.ShapeDtypeStruct((M_ALL, D_MODEL), F32),
                   jax.ShapeDtypeStruct((M_ALL, D_MODEL), BF16),
                   jax.ShapeDtypeStruct((BATCH, KV_DIM, WINDOW), F32),
                   jax.ShapeDtypeStruct((BATCH, KV_DIM, WINDOW), F32),
                   jax.ShapeDtypeStruct((BATCH, CONV_WIDTH - 1, CONV_DIM), F32)),
        grid=(N_GRP + 1,),
        in_specs=[pl.BlockSpec(memory_space=pltpu.SMEM),
                  pl.BlockSpec((TM_MIX, Z1_DIM), lambda t: (grp(t), 0)),
                  pl.BlockSpec((BLOCK, 2 * KV_DIM),
                               lambda t: (_prev_blk(blk0(t)), Q_DIM // (2 * KV_DIM))),
                  pl.BlockSpec((TM_MIX, CONV_DIM), lambda t: (grp(t), 0)),
                  pl.BlockSpec((8, CONV_DIM), lambda t: (_prev_blk(blk0(t)) * rows8 + rows8 - 1, 0)),
                  bias_spec(lambda t: (_bias_variant(blk0(t)), 0, 0, 0)),
                  bias_spec(lambda t: (2, 0, 0, 0), pipeline_mode=pl.Buffered(1)),
                  pl.BlockSpec((CONV_WIDTH, CONV_DIM), lambda t: (0, 0)),
                  pl.BlockSpec((DEC_BATCH, D_MODEL), lambda t: (0, 0)),
                  pl.BlockSpec((D_MODEL // 2, D_MODEL), lambda t: (0, 0), pipeline_mode=pl.Buffered(1)),
                  pl.BlockSpec((TM_MIX, D_MODEL), lambda t: (jnp.minimum(prj(t), n_hin - 1), 0)),
                  pl.BlockSpec((BLOCK, D_MODEL), lambda t: (hin_tail.shape[0] // BLOCK - 1, 0)),
                  pl.BlockSpec((DEPTH, D_MODEL), lambda t: (0, 0)),
                  pl.BlockSpec((DEPTH, D_MODEL), lambda t: (0, 0))],
        out_specs=(pl.BlockSpec((TM_MIX, D_MODEL), lambda t: (prj(t), 0)),
                   pl.BlockSpec((TM_MIX, D_MODEL), lambda t: (prj(t), 0)),
                   pl.BlockSpec((1, KV_DIM, WINDOW), lambda t: (seq_of(t), 0, 0)),
                   pl.BlockSpec((1, KV_DIM, WINDOW), lambda t: (seq_of(t), 0, 0)),
                   pl.BlockSpec((1, CONV_WIDTH - 1, CONV_DIM), lambda t: (seq_of(t), 0, 0))),
        scratch_shapes=[pltpu.VMEM((NB_MIX, BLOCK + 8, CONV_DIM), F32),
                        pltpu.VMEM((TM_MIX, D_MODEL), BF16),
                        pltpu.VMEM((TM_MIX, D_MODEL), BF16)],
        compiler_params=_cparams(("arbitrary",)),
        name="mixer",
    )(sinks, z1, z1, u, u, bias3, bias3, cw, mix_s, w_out_b, hin, hin_tail, g_post, g_next)


S_CHUNK = 16
SH_ROWS = S_CHUNK * N_HEADS


def _sample_kernel(zs_ref, us_ref, kvt_ref, kc_ref, vc_ref, st_ref, cw_ref, sinkt_ref, tabt_ref,
                   brow_ref, fold_ref, unfold_ref, kall_ref, vall_ref,
                   mixs_ref, kout_ref, vout_ref, cs_ref, *, layer):
    del kall_ref, vall_ref
    z = zs_ref[...]
    q = z[:, 0:Q_DIM] * SCALE
    qrep = jnp.concatenate([jnp.broadcast_to(q[s:s + 1, :], (N_HEADS, Q_DIM))
                            for s in range(S_CHUNK)], axis=0)
    row_head = lax.broadcasted_iota(jnp.int32, (SH_ROWS, Q_DIM), 0) % N_HEADS
    own_q = lax.broadcasted_iota(jnp.int32, (SH_ROWS, Q_DIM), 1) // HEAD_DIM == row_head
    qblk = jnp.dot(jnp.where(own_q, qrep, 0.0).astype(BF16), fold_ref[...],
                   preferred_element_type=F32)
    own_kv = (lax.broadcasted_iota(jnp.int32, (N_HEADS, KV_DIM), 1) // HEAD_DIM
              == lax.broadcasted_iota(jnp.int32, (N_HEADS, KV_DIM), 0) // GROUP)

    brow = brow_ref[...]
    tabt = tabt_ref[...]
    bias_t = jnp.zeros((N_HEADS, W_BUF), F32)
    for b in range(NUM_BUCKETS):
        bias_t = jnp.where(brow == b, tabt[:, b:b + 1], bias_t)
    bias0 = tabt[:, 0:1]
    sink = sinkt_ref[:, layer:layer + 1]
    last_key = lax.broadcasted_iota(jnp.int32, (KV_DIM, W_BUF), 1) == W_BUF - 1

    tile_s = lambda x: jnp.concatenate([x] * S_CHUNK, axis=0)
    rep_s = lambda c0, c1: jnp.concatenate(
        [jnp.broadcast_to(z[s:s + 1, c0:c1], (N_HEADS, c1 - c0)) for s in range(S_CHUNK)], axis=0)
    sample_rows = lambda s: slice(s * KV_DIM, (s + 1) * KV_DIM)
    head_rows = lambda s: slice(s * N_HEADS, (s + 1) * N_HEADS)

    qblk_b = qblk.astype(BF16)
    sc = jnp.concatenate(
        [jnp.dot(qblk_b[head_rows(s), :], kc_ref[sample_rows(s), :].astype(BF16),
                 preferred_element_type=F32) for s in range(S_CHUNK)], axis=0) + tile_s(bias_t)
    sn = (jnp.sum(qblk * rep_s(Q_DIM, Q_DIM + KV_DIM), axis=-1, keepdims=True)
          + tile_s(bias0))
    sink_all = tile_s(sink)
    m = jnp.maximum(jnp.maximum(jnp.max(sc, axis=-1, keepdims=True), sn), sink_all)
    e = jnp.exp(sc - m)
    en = jnp.exp(sn - m)
    den = jnp.sum(e, axis=-1, keepdims=True) + en + jnp.exp(sink_all - m)
    p = (e / den).astype(BF16)
    o_all = jnp.concatenate(
        [lax.dot_general(p[head_rows(s), :], vc_ref[sample_rows(s), :].astype(BF16),
                         (((1,), (1,)), ((), ())), preferred_element_type=F32)
         for s in range(S_CHUNK)], axis=0)
    o_all = jnp.where(tile_s(own_kv), o_all + (en / den) * rep_s(Q_DIM + KV_DIM, Q_DIM + 2 * KV_DIM),
                      0.0)

    for s in range(S_CHUNK):
        rows = sample_rows(s)
        kout_ref[rows, :] = jnp.where(last_key, kvt_ref[0, 0:KV_DIM, s:s + 1],
                                      pltpu.roll(kc_ref[rows, :], W_BUF - 1, 1))
        vout_ref[rows, :] = jnp.where(last_key, kvt_ref[0, KV_DIM:2 * KV_DIM, s:s + 1],
                                      pltpu.roll(vc_ref[rows, :], W_BUF - 1, 1))

    a_all = jnp.dot(o_all.astype(BF16), unfold_ref[...],
                    preferred_element_type=F32)
    a_all = jnp.where(own_q, a_all, 0.0)
    mixs_ref[:, 0:Q_DIM] = jnp.sum(a_all.reshape(S_CHUNK, N_HEADS, Q_DIM), axis=1)

    gb = z[:, Q_DIM + 2 * KV_DIM:Z1_DIM]
    u = us_ref[...]
    s0 = st_ref[:, 0:CONV_DIM]
    s1 = st_ref[:, CONV_DIM:]
    mixs_ref[:, Q_DIM:] = gb * (cw_ref[0:1, :] * s0 + cw_ref[1:2, :] * s1 + cw_ref[2:3, :] * u)
    cs_ref[:, 0:CONV_DIM] = s1
    cs_ref[:, CONV_DIM:] = u


def _sample_mixer(layer, z1, u, kvt, cache_kt, cache_vt, state2d, cw, sinkt, tabt, brow,
                  fold, unfold, k_all, v_all):
    n_chunks = DEC_BATCH // S_CHUNK
    row_blk0 = M_X // S_CHUNK
    full = lambda shape: pl.BlockSpec(shape, lambda c: (0,) * len(shape))
    n_in = 14
    return pl.pallas_call(
        functools.partial(_sample_kernel, layer=layer),
        out_shape=(jax.ShapeDtypeStruct((DEC_BATCH, D_MODEL), F32),
                   jax.ShapeDtypeStruct(k_all.shape, F32),
                   jax.ShapeDtypeStruct(v_all.shape, F32),
                   jax.ShapeDtypeStruct((DEC_BATCH, 2 * CONV_DIM), F32)),
        input_output_aliases={n_in - 2: 1, n_in - 1: 2},
        grid=(n_chunks,),
        in_specs=[pl.BlockSpec((S_CHUNK, Z1_DIM), lambda c: (row_blk0 + c, 0)),
                  pl.BlockSpec((S_CHUNK, CONV_DIM), lambda c: (row_blk0 + c, 0)),
                  pl.BlockSpec((1, 2 * KV_DIM, S_CHUNK), lambda c: (c, 0, 0)),
                  pl.BlockSpec((S_CHUNK * KV_DIM, W_BUF), lambda c: (layer * n_chunks + c, 0)),
                  pl.BlockSpec((S_CHUNK * KV_DIM, W_BUF), lambda c: (layer * n_chunks + c, 0)),
                  pl.BlockSpec((S_CHUNK, 2 * CONV_DIM), lambda c: (layer * n_chunks + c, 0)),
                  full((CONV_WIDTH, CONV_DIM)),
                  full((N_HEADS, DEPTH)),
                  full((N_HEADS, NUM_BUCKETS)),
                  full((1, W_BUF)),
                  full((Q_DIM, KV_DIM)),
                  full((KV_DIM, Q_DIM)),
                  pl.BlockSpec(memory_space=pl.ANY),
                  pl.BlockSpec(memory_space=pl.ANY)],
        out_specs=(pl.BlockSpec((S_CHUNK, D_MODEL), lambda c: (c, 0)),
                   pl.BlockSpec((S_CHUNK * KV_DIM, W_BUF), lambda c: (layer * n_chunks + c, 0)),
                   pl.BlockSpec((S_CHUNK * KV_DIM, W_BUF), lambda c: (layer * n_chunks + c, 0)),
                   pl.BlockSpec((S_CHUNK, 2 * CONV_DIM), lambda c: (c, 0))),
        compiler_params=_cparams(("arbitrary",)),
        name="sample_mixer",
    )(z1, u, kvt, cache_kt, cache_vt, state2d, cw, sinkt, tabt, brow, fold, unfold, k_all, v_all)


def _t5_bucket(d):
    max_exact = NUM_BUCKETS // 2
    df = jnp.maximum(d, 1).astype(F32)
    large = max_exact + (jnp.log(df / max_exact) / math.log(MAX_DISTANCE / max_exact)
                         * (NUM_BUCKETS - max_exact)).astype(jnp.int32)
    large = jnp.minimum(large, NUM_BUCKETS - 1)
    return jnp.where(d < max_exact, d, large)


def _band_tables():
    i = np.arange(BLOCK)[:, None]
    j = np.arange(2 * BLOCK)[None, :]
    d = BLOCK + i - j
    band = (d >= 0) & (d <= WINDOW)
    valid = np.stack([band & (j >= BLOCK + META_ROW0), band & (j >= META_ROW0), band])
    bucket = _t5_bucket(jnp.asarray(np.maximum(d, 0), jnp.int32))
    return bucket, jnp.asarray(valid.astype(np.int32))


def _fold_tables():
    d = np.arange(HEAD_DIM)
    fold = np.zeros((Q_DIM, KV_DIM), np.float32)
    for h in range(N_HEADS):
        fold[h * HEAD_DIM + d, (h // GROUP) * HEAD_DIM + d] = 1.0
    return jnp.asarray(fold, BF16), jnp.asarray(fold.T, BF16)


def kernel(x_prompt, x_sample, cache_k, cache_v, state_conv, meta_tokens, rel_bias, w_in, conv_w,
           attn_sinks, w_out, norm_pre_mix, norm_post_mix, norm_pre_ffn, norm_post_ffn,
           w_gate, w_up, w_down):
    tail = jnp.concatenate([x_sample.reshape(DEC_BATCH, D_MODEL),
                            jnp.zeros((META_ROW0 - DEC_BATCH, D_MODEL), F32),
                            meta_tokens.astype(F32)], axis=0)

    bucket, valid = _band_tables()
    fold, unfold = _fold_tables()
    bias3 = _band_bias(rel_bias, bucket, valid)
    brow = _t5_bucket(jnp.asarray(W_BUF - np.arange(W_BUF), jnp.int32)).reshape(1, W_BUF)

    kv_lanes = lambda c: jnp.transpose(c, (0, 1, 3, 4, 2)).reshape(DEPTH * DEC_BATCH * KV_DIM, W_BUF)
    cache_kt = kv_lanes(cache_k)
    cache_vt = kv_lanes(cache_v)
    state2d = state_conv.reshape(DEPTH * DEC_BATCH, 2 * CONV_DIM)

    x2d = x_prompt.reshape(M_X, D_MODEL)
    xn = _norm(x2d, tail, norm_pre_mix)
    h, h_tail = x2d, tail

    def kv_rows(t, n):
        return jnp.transpose(t.reshape(n, N_KV_HEADS, HEAD_DIM, WINDOW), (0, 3, 1, 2))

    k_all = jnp.zeros(cache_kt.shape, F32)
    v_all = jnp.zeros(cache_vt.shape, F32)
    kp, vp, cp, cs = [], [], [], []
    for l in range(DEPTH):
        z1, w_out_b = _matmul(xn, w_in, w_out, l, Z1_DIM)
        (u,) = _gated(xn, w_in, w_in, None, l, Z1_DIM, Z1_DIM + CONV_DIM, CONV_DIM,
                      False, F32, "conv_in")
        kvt = jnp.transpose(z1[M_X:M_X + DEC_BATCH, Q_DIM:Q_DIM + 2 * KV_DIM]
                            .reshape(DEC_BATCH // S_CHUNK, S_CHUNK, 2 * KV_DIM), (0, 2, 1))
        mix_s, k_all, v_all, c_new = _sample_mixer(
            l, z1, u, kvt, cache_kt, cache_vt, state2d, conv_w[l], attn_sinks.T, rel_bias.T,
            brow, fold, unfold, k_all, v_all)
        h, xn, kp_l, vp_l, cp_l = _mixer(l, attn_sinks[l], z1, u, bias3, conv_w[l], mix_s,
                                         w_out_b, h, h_tail, norm_post_mix, norm_pre_ffn)
        act, w_down_b = _gated(xn, w_gate, w_up, w_down, l, 0, 0, D_FF, True, BF16, "ffn_up")
        if l + 1 < DEPTH:
            h, xn = _proj_norm(act, w_down_b, l, h, norm_post_ffn, norm_pre_mix, l + 1, "ffn_down")
            h_tail = h
        else:
            y2d, y_tail = _proj_last(act, w_down_b, l, h, norm_post_ffn)

        kp.append(kv_rows(kp_l, BATCH))
        vp.append(kv_rows(vp_l, BATCH))
        cp.append(cp_l)
        cs.append(c_new.reshape(DEC_BATCH, CONV_WIDTH - 1, CONV_DIM))

    y_prompt = y2d.reshape(BATCH, SEQ, D_MODEL)
    y_sample = y_tail[0:DEC_BATCH].reshape(DEC_BATCH, 1, D_MODEL)
    k_sample = kv_rows(k_all, DEPTH * DEC_BATCH).reshape(DEPTH, DEC_BATCH, W_BUF, N_KV_HEADS, HEAD_DIM)
    v_sample = kv_rows(v_all, DEPTH * DEC_BATCH).reshape(DEPTH, DEC_BATCH, W_BUF, N_KV_HEADS, HEAD_DIM)
    return (y_prompt, y_sample, jnp.stack(kp), jnp.stack(vp), jnp.stack(cp),
            k_sample, v_sample, jnp.stack(cs))
```

```python
import functools
import math

import numpy as np
import jax
import jax.numpy as jnp
from jax import lax
from jax.experimental import pallas as pl
from jax.experimental.pallas import tpu as pltpu

D_MODEL = 2048
BATCH = 4
SEQ = 2048
DEPTH = 4
DEC_BATCH = 32
HEAD_DIM = 64
N_HEADS = 16
N_KV_HEADS = 4
GROUP = 4
Q_DIM = 1024
KV_DIM = 256
CONV_DIM = 1024
CONV_WIDTH = 3
IN_DIM = 4608
WINDOW = 128
BLOCK = 128
NUM_BUCKETS = 32
MAX_DISTANCE = 128
N_META = 16
D_FF = 5632
RMS_EPS = 1e-6
SCALE = HEAD_DIM ** -0.5
W_BUF = 128

M_X = BATCH * SEQ
N_XBLK = M_X // BLOCK
BLK_PER_SEQ = SEQ // BLOCK
M_ALL = M_X + BLOCK
META_ROW0 = BLOCK - N_META
Z1_DIM = Q_DIM + 2 * KV_DIM + CONV_DIM
NEG = -1e30
LOG2E = math.log2(math.e)

TM_BIG = 1664
TM_EPI = 640
N_EPI = M_ALL // TM_EPI
EPI_LAST_X = M_X - (N_EPI - 1) * TM_EPI
assert (M_X - EPI_LAST_X) % EPI_LAST_X == 0 and EPI_LAST_X + BLOCK == TM_EPI
TM_PROJ = 416
PROJ_ROW_CHUNKS = ((0, 144), (144, 288), (288, 416))
N_PROJ = M_ALL // TM_PROJ
PROJ_LAST_X = M_X - (N_PROJ - 1) * TM_PROJ
assert PROJ_LAST_X + BLOCK == TM_PROJ and PROJ_LAST_X % 8 == 0
TN = 512
GATED_CHUNKS = 13
TM_IN = 832
TN_IN = 1280
VMEM_LIMIT = 56 * 1024 * 1024

F32 = jnp.float32
BF16 = jnp.bfloat16


def _cparams(sem):
    return pltpu.CompilerParams(dimension_semantics=sem, vmem_limit_bytes=VMEM_LIMIT)


def _rms(x, g):
    return x * lax.rsqrt(jnp.mean(x * x, axis=-1, keepdims=True) + RMS_EPS) * g


def _norm_kernel(xa_ref, xb_ref, tail_ref, g_ref, xn_ref):
    last = pl.program_id(0) == N_EPI - 1
    h_last = jnp.concatenate([xb_ref[...], tail_ref[...]], axis=0)
    h = jnp.where(last, h_last, xa_ref[...])
    xn_ref[...] = _rms(h, g_ref[0:1, :]).astype(BF16)


def _norm(x2d, tail, g):
    return pl.pallas_call(
        _norm_kernel,
        out_shape=jax.ShapeDtypeStruct((M_ALL, D_MODEL), BF16),
        grid=(N_EPI,),
        in_specs=[pl.BlockSpec((TM_EPI, D_MODEL), lambda i: (jnp.minimum(i, N_EPI - 2), 0)),
                  pl.BlockSpec((EPI_LAST_X, D_MODEL), lambda i: ((M_X - EPI_LAST_X) // EPI_LAST_X, 0)),
                  pl.BlockSpec((BLOCK, D_MODEL), lambda i: (0, 0)),
                  pl.BlockSpec((DEPTH, D_MODEL), lambda i: (0, 0))],
        out_specs=pl.BlockSpec((TM_EPI, D_MODEL), lambda i: (i, 0)),
        compiler_params=_cparams(("arbitrary",)),
        name="norm0",
    )(x2d, x2d, tail, g)


RIDER_ROWS = 128


def _pack_bf16(w_f32):
    return pltpu.bitcast(w_f32.astype(BF16), jnp.uint32)


def _unpack_bf16(w_u32):
    return pltpu.bitcast(w_u32, BF16)


def _rider(wr, layer, n_outer, n_inner):
    n_chunks = wr.shape[1] // RIDER_ROWS
    assert n_chunks <= n_outer * n_inner
    chunk = lambda j, i: jnp.minimum(j * n_inner + i, n_chunks - 1)
    in_spec = pl.BlockSpec((None, RIDER_ROWS, D_MODEL), lambda j, i: (layer, chunk(j, i), 0))
    out_spec = pl.BlockSpec((RIDER_ROWS // 2, D_MODEL), lambda j, i: (chunk(j, i), 0))
    return in_spec, out_spec, jax.ShapeDtypeStruct((wr.shape[1] // 2, D_MODEL), jnp.uint32)


def _matmul_kernel(x_ref, w_ref, wr_ref, o_ref, wrb_ref, wbf_ref):
    @pl.when(pl.program_id(1) == 0)
    def _():
        wbf_ref[...] = w_ref[...].astype(BF16)

    o_ref[...] = jnp.dot(x_ref[...], wbf_ref[...], preferred_element_type=F32)
    wrb_ref[...] = _pack_bf16(wr_ref[...])


def _matmul(x, w, wr, layer, n_cols):
    k = x.shape[1]
    n_outer, n_inner = n_cols // TN_IN, M_ALL // TM_IN
    r_in, r_out, r_shape = _rider(wr, layer, n_outer, n_inner)
    return pl.pallas_call(
        _matmul_kernel,
        out_shape=(jax.ShapeDtypeStruct((M_ALL, n_cols), F32), r_shape),
        grid=(n_outer, n_inner),
        in_specs=[pl.BlockSpec((TM_IN, k), lambda j, i: (i, 0)),
                  pl.BlockSpec((None, k, TN_IN), lambda j, i: (layer, 0, j)),
                  r_in],
        out_specs=(pl.BlockSpec((TM_IN, TN_IN), lambda j, i: (i, j)), r_out),
        scratch_shapes=[pltpu.VMEM((k, TN_IN), BF16)],
        compiler_params=_cparams(("arbitrary", "arbitrary")),
        name="in_proj",
    )(x, w, wr)


def _gated_kernel(*refs, silu, has_rider):
    if has_rider:
        x_ref, wa_ref, wb_ref, wr_ref, o_ref, wrb_ref, wbf_ref = refs
    else:
        x_ref, wa_ref, wb_ref, o_ref, wbf_ref = refs

    @pl.when(pl.program_id(1) == 0)
    def _():
        wbf_ref[:, 0:TN] = wa_ref[...].astype(BF16)
        wbf_ref[:, TN:2 * TN] = wb_ref[...].astype(BF16)

    rc = TM_BIG // GATED_CHUNKS
    for c in range(GATED_CHUNKS):
        rows = slice(c * rc, (c + 1) * rc)
        ab = jnp.dot(x_ref[rows, :], wbf_ref[...], preferred_element_type=F32)
        a = ab[:, 0:TN]
        b = ab[:, TN:2 * TN]
        if silu:
            a = a * (1.0 / (1.0 + jnp.exp(-a)))
        o_ref[rows, :] = (a * b).astype(o_ref.dtype)
    if has_rider:
        wrb_ref[...] = _pack_bf16(wr_ref[...])


def _gated(x, wa, wb, wr, layer, a_col0, b_col0, n_cols, silu, out_dtype, name):
    k = x.shape[1]
    a0 = a_col0 // TN
    b0 = b_col0 // TN
    n_outer, n_inner = n_cols // TN, M_ALL // TM_BIG
    in_specs = [pl.BlockSpec((TM_BIG, k), lambda j, i: (i, 0)),
                pl.BlockSpec((None, k, TN), lambda j, i: (layer, 0, a0 + j)),
                pl.BlockSpec((None, k, TN), lambda j, i: (layer, 0, b0 + j))]
    out_specs = [pl.BlockSpec((TM_BIG, TN), lambda j, i: (i, j))]
    out_shape = [jax.ShapeDtypeStruct((M_ALL, n_cols), out_dtype)]
    args = [x, wa, wb]
    if wr is not None:
        r_in, r_out, r_shape = _rider(wr, layer, n_outer, n_inner)
        in_specs.append(r_in)
        out_specs.append(r_out)
        out_shape.append(r_shape)
        args.append(wr)
    return pl.pallas_call(
        functools.partial(_gated_kernel, silu=silu, has_rider=wr is not None),
        out_shape=tuple(out_shape),
        grid=(n_outer, n_inner),
        in_specs=in_specs,
        out_specs=tuple(out_specs),
        scratch_shapes=[pltpu.VMEM((k, 2 * TN), BF16)],
        compiler_params=_cparams(("arbitrary", "arbitrary")),
        name=name,
    )(*args)


def _proj_norm_kernel(a_ref, w_ref, hin_ref, gpost_ref, gnext_ref, h_ref, xn_ref, *, l_post, l_next):
    g_post = gpost_ref[l_post:l_post + 1, :]
    g_next = gnext_ref[l_next:l_next + 1, :]
    for r0, r1 in PROJ_ROW_CHUNKS:
        rows = slice(r0, r1)
        y = jnp.dot(a_ref[rows, :], _unpack_bf16(w_ref[...]), preferred_element_type=F32)
        hn = hin_ref[rows, :] + _rms(y, g_post)
        h_ref[rows, :] = hn
        xn_ref[rows, :] = _rms(hn, g_next).astype(BF16)


def _proj_norm(a, w, layer, hin, g_post, g_next, l_next, name):
    k = a.shape[1]
    return pl.pallas_call(
        functools.partial(_proj_norm_kernel, l_post=layer, l_next=l_next),
        out_shape=(jax.ShapeDtypeStruct((M_ALL, D_MODEL), F32),
                   jax.ShapeDtypeStruct((M_ALL, D_MODEL), BF16)),
        grid=(M_ALL // TM_PROJ,),
        in_specs=[pl.BlockSpec((TM_PROJ, k), lambda i: (i, 0)),
                  pl.BlockSpec((k // 2, D_MODEL), lambda i: (0, 0), pipeline_mode=pl.Buffered(1)),
                  pl.BlockSpec((TM_PROJ, D_MODEL), lambda i: (i, 0)),
                  pl.BlockSpec((DEPTH, D_MODEL), lambda i: (0, 0)),
                  pl.BlockSpec((DEPTH, D_MODEL), lambda i: (0, 0))],
        out_specs=(pl.BlockSpec((TM_PROJ, D_MODEL), lambda i: (i, 0)),
                   pl.BlockSpec((TM_PROJ, D_MODEL), lambda i: (i, 0))),
        compiler_params=_cparams(("parallel",)),
        name=name,
    )(a, w, hin, g_post, g_next)


def _proj_last_kernel(a_ref, w_ref, hin_ref, gpost_ref, y_ref, tail_ref, *, l_post):
    g_post = gpost_ref[l_post:l_post + 1, :]
    for r0, r1 in PROJ_ROW_CHUNKS:
        rows = slice(r0, r1)
        y = jnp.dot(a_ref[rows, :], _unpack_bf16(w_ref[...]), preferred_element_type=F32)
        y_ref[rows, :] = hin_ref[rows, :] + _rms(y, g_post)

    @pl.when(pl.program_id(0) == N_PROJ - 1)
    def _():
        tail_ref[...] = y_ref[PROJ_LAST_X:TM_PROJ, :]


def _proj_last(a, w, layer, hin, g_post):
    k = a.shape[1]
    return pl.pallas_call(
        functools.partial(_proj_last_kernel, l_post=layer),
        out_shape=(jax.ShapeDtypeStruct((M_X, D_MODEL), F32),
                   jax.ShapeDtypeStruct((BLOCK, D_MODEL), F32)),
        grid=(N_PROJ,),
        in_specs=[pl.BlockSpec((TM_PROJ, k), lambda i: (i, 0)),
                  pl.BlockSpec((k // 2, D_MODEL), lambda i: (0, 0), pipeline_mode=pl.Buffered(1)),
                  pl.BlockSpec((TM_PROJ, D_MODEL), lambda i: (i, 0)),
                  pl.BlockSpec((DEPTH, D_MODEL), lambda i: (0, 0))],
        out_specs=(pl.BlockSpec((TM_PROJ, D_MODEL), lambda i: (i, 0)),
                   pl.BlockSpec((BLOCK, D_MODEL), lambda i: (0, 0))),
        compiler_params=_cparams(("arbitrary",)),
        name="ffn_down_last",
    )(a, w, hin, g_post)


def _band_bias_kernel(tab_ref, bucket_ref, valid_ref, o_ref):
    h = pl.program_id(0)
    bucket = bucket_ref[...]
    acc = jnp.zeros((BLOCK, 2 * BLOCK), F32)
    for b in range(NUM_BUCKETS):
        acc = jnp.where(bucket == b, tab_ref[b, h], acc)
    acc = acc * LOG2E
    for v in range(3):
        o_ref[v, 0] = jnp.where(valid_ref[v] != 0, acc, NEG)


def _band_bias(rel_bias, bucket, valid):
    return pl.pallas_call(
        _band_bias_kernel,
        out_shape=jax.ShapeDtypeStruct((3, N_HEADS, BLOCK, 2 * BLOCK), F32),
        grid=(N_HEADS,),
        in_specs=[pl.BlockSpec(memory_space=pltpu.SMEM),
                  pl.BlockSpec((BLOCK, 2 * BLOCK), lambda h: (0, 0)),
                  pl.BlockSpec((3, BLOCK, 2 * BLOCK), lambda h: (0, 0, 0))],
        out_specs=pl.BlockSpec((3, 1, BLOCK, 2 * BLOCK), lambda h: (0, h, 0, 0)),
        compiler_params=_cparams(("arbitrary",)),
        name="band_bias",
    )(rel_bias, bucket, valid)


NB_MIX = 2
TM_MIX = NB_MIX * BLOCK
N_GRP = -(-(N_XBLK + 1) // NB_MIX)


def _mix_block(b, kv_prev, u_prev8, is_tail, bias_ref, sink_ref, z_ref, u_ref, cw_ref, ext_ref,
               mix_ref):
    rows = slice(b * BLOCK, (b + 1) * BLOCK)

    lo_kv = lax.broadcasted_iota(jnp.int32, (2 * BLOCK, BLOCK), 1) < HEAD_DIM
    one_lo = jnp.where(lo_kv, 1.0, 0.0).astype(BF16)
    one_hi = jnp.where(lo_kv, 0.0, 1.0).astype(BF16)
    col_cache = {}

    def kv_tables(kh):
        col, in_hi = kh // 2, kh % 2
        if col not in col_cache:
            c0 = col * BLOCK
            kc = jnp.concatenate([kv_prev[:, c0:c0 + BLOCK],
                                  z_ref[rows, Q_DIM + c0:Q_DIM + c0 + BLOCK]], axis=0)
            vc = jnp.concatenate([kv_prev[:, KV_DIM + c0:KV_DIM + c0 + BLOCK],
                                  z_ref[rows, Q_DIM + KV_DIM + c0:Q_DIM + KV_DIM + c0 + BLOCK]], axis=0)
            col_cache[col] = (kc, vc, pltpu.roll(kc, HEAD_DIM, 1), pltpu.roll(vc, HEAD_DIM, 1))
        kc, vc, ks, vs = col_cache[col]
        own_k, oth_k = (ks, kc) if in_hi else (kc, ks)
        own_v, oth_v = (vs, vc) if in_hi else (vc, vs)
        k2 = jnp.where(lo_kv, own_k, oth_k).astype(BF16)
        rv = jnp.concatenate(
            [jnp.concatenate([jnp.where(lo_kv, own_v, 0.0).astype(BF16), one_lo], axis=1),
             jnp.concatenate([jnp.where(lo_kv, 0.0, oth_v).astype(BF16), one_hi], axis=1)], axis=0)
        return k2, rv

    lane = lax.broadcasted_iota(jnp.int32, (BLOCK, BLOCK), 1)
    lo_half = lane < HEAD_DIM
    zero = jnp.zeros((BLOCK, BLOCK), BF16)
    k2, rv = {}, {}
    for p in range(N_HEADS // 2):
        kh = p // 2
        if kh not in k2:
            k2[kh], rv[kh] = kv_tables(kh)
        qp = (z_ref[rows, p * BLOCK:(p + 1) * BLOCK] * (SCALE * LOG2E)).astype(BF16)
        es, ms, sks = [], [], []
        for half in range(2):
            h = 2 * p + half
            qm = jnp.where(lo_half if half == 0 else jnp.logical_not(lo_half), qp, zero)
            s = lax.dot_general(qm, k2[kh], (((1,), (1,)), ((), ())),
                                preferred_element_type=F32) + bias_ref[0, h]
            sk = sink_ref[h] * LOG2E
            m = jnp.maximum(jnp.max(s, axis=-1, keepdims=True), sk)
            es.append(jnp.exp2(s - m).astype(BF16))
            ms.append(m)
            sks.append(sk)
        acc = jnp.dot(jnp.concatenate(es, axis=1), rv[kh], preferred_element_type=F32)
        sink_term = jnp.exp2(jnp.where(lo_half, sks[0], sks[1]) - jnp.where(lo_half, ms[0], ms[1]))
        den = acc[:, BLOCK:] + sink_term
        mix_ref[rows, p * BLOCK:(p + 1) * BLOCK] = (acc[:, :BLOCK] / den).astype(BF16)

    ext_ref[b, 0:8, :] = u_prev8
    ext_ref[b, 8:8 + BLOCK, :] = u_ref[rows, :]
    u1 = ext_ref[b, 7:7 + BLOCK, :]
    u2 = ext_ref[b, 6:6 + BLOCK, :]
    if is_tail is not None:
        row = lax.broadcasted_iota(jnp.int32, (BLOCK, CONV_DIM), 0)
        first = jnp.where(is_tail, META_ROW0, -8)
        u1 = jnp.where(row >= first + 1, u1, 0.0)
        u2 = jnp.where(row >= first + 2, u2, 0.0)
    gb = z_ref[rows, Q_DIM + 2 * KV_DIM:Z1_DIM]
    c = gb * (cw_ref[0:1, :] * u2 + cw_ref[1:2, :] * u1 + cw_ref[2:3, :] * u_ref[rows, :])
    mix_ref[rows, Q_DIM:] = c.astype(BF16)


def _mixer_kernel(sink_ref, z_ref, zp_ref, u_ref, up_ref, bias0_ref, bias_ref, cw_ref, mixs_ref,
                  w_ref, hin_ref, hint_ref, gpost_ref, gnext_ref,
                  h_ref, xn_ref, kp_ref, vp_ref, cp_ref, ext_ref, mix_ref, lhs_ref, *, layer):
    t = pl.program_id(0)
    i0 = jnp.minimum(t, N_GRP - 1) * NB_MIX
    is_tail = i0 == N_XBLK

    @pl.when(t == 0)
    def _():
        mix_ref[...] = jnp.zeros_like(mix_ref)

    lhs_ref[...] = mix_ref[...]
    half_n = D_MODEL // 2
    y_parts = []

    for b in range(NB_MIX):
        y_parts.append(jnp.dot(lhs_ref[...], _unpack_bf16(w_ref[:, b * half_n:(b + 1) * half_n]),
                               preferred_element_type=F32))
        if b == 0:
            _mix_block(0, zp_ref[...], up_ref[...], is_tail, bias0_ref, sink_ref, z_ref, u_ref,
                       cw_ref, ext_ref, mix_ref)
        else:
            prev = slice((b - 1) * BLOCK, b * BLOCK)
            _mix_block(b, z_ref[prev, Q_DIM:Q_DIM + 2 * KV_DIM], u_ref[b * BLOCK - 8:b * BLOCK, :],
                       None, bias_ref, sink_ref, z_ref, u_ref, cw_ref, ext_ref, mix_ref)

    y = jnp.concatenate(y_parts, axis=1)
    hin = hin_ref[...]
    if layer == 0:
        hin = jnp.where(t - 1 == N_GRP - 1, jnp.concatenate([hint_ref[...]] * NB_MIX, axis=0), hin)
    hn = hin + _rms(y, gpost_ref[layer:layer + 1, :])
    h_ref[...] = hn
    xn_ref[...] = _rms(hn, gnext_ref[layer:layer + 1, :]).astype(BF16)

    @pl.when(is_tail)
    def _():
        mix_ref[0:DEC_BATCH, :] = mixs_ref[...].astype(BF16)

    last = i0 + NB_MIX - 1
    @pl.when(jnp.logical_and(last < N_XBLK, last % BLK_PER_SEQ == BLK_PER_SEQ - 1))
    def _():
        rows = slice((NB_MIX - 1) * BLOCK, NB_MIX * BLOCK)
        kp_ref[0] = z_ref[rows, Q_DIM:Q_DIM + KV_DIM].T
        vp_ref[0] = z_ref[rows, Q_DIM + KV_DIM:Q_DIM + 2 * KV_DIM].T
        cp_ref[0] = u_ref[TM_MIX - (CONV_WIDTH - 1):TM_MIX, :]


def _prev_blk(i):
    return jnp.where(i % BLK_PER_SEQ == 0, N_XBLK, i - 1)


def _bias_variant(i):
    return jnp.where(i == N_XBLK, 0, jnp.where(i % BLK_PER_SEQ == 0, 1, 2))


def _mixer(layer, sinks, z1, u, bias3, cw, mix_s, w_out_b, hin, hin_tail, g_post, g_next):
    assert BLK_PER_SEQ % NB_MIX == 0
    rows8 = BLOCK // 8
    grp = lambda t: jnp.minimum(t, N_GRP - 1)
    blk0 = lambda t: grp(t) * NB_MIX
    prj = lambda t: jnp.maximum(t - 1, 0)
    n_hin = -(-hin.shape[0] // TM_MIX)
    seq_of = lambda t: jnp.minimum(blk0(t) // BLK_PER_SEQ, BATCH - 1)
    bias_spec = lambda f, **kw: pl.BlockSpec((1, N_HEADS, BLOCK, 2 * BLOCK), f, **kw)
    return pl.pallas_call(
        functools.partial(_mixer_kernel, layer=layer),
        out_shape=(jax.ShapeDtypeStruct((M_ALL, D_MODEL), F32),
                   jax.ShapeDtypeStruct((M_ALL, D_MODEL), BF16),
                   jax.ShapeDtypeStruct((BATCH, KV_DIM, WINDOW), F32),
                   jax.ShapeDtypeStruct((BATCH, KV_DIM, WINDOW), F32),
                   jax.ShapeDtypeStruct((BATCH, CONV_WIDTH - 1, CONV_DIM), F32)),
        grid=(N_GRP + 1,),
        in_specs=[pl.BlockSpec(memory_space=pltpu.SMEM),
                  pl.BlockSpec((TM_MIX, Z1_DIM), lambda t: (grp(t), 0)),
                  pl.BlockSpec((BLOCK, 2 * KV_DIM),
                               lambda t: (_prev_blk(blk0(t)), Q_DIM // (2 * KV_DIM))),
                  pl.BlockSpec((TM_MIX, CONV_DIM), lambda t: (grp(t), 0)),
                  pl.BlockSpec((8, CONV_DIM), lambda t: (_prev_blk(blk0(t)) * rows8 + rows8 - 1, 0)),
                  bias_spec(lambda t: (_bias_variant(blk0(t)), 0, 0, 0)),
                  bias_spec(lambda t: (2, 0, 0, 0), pipeline_mode=pl.Buffered(1)),
                  pl.BlockSpec((CONV_WIDTH, CONV_DIM), lambda t: (0, 0)),
                  pl.BlockSpec((DEC_BATCH, D_MODEL), lambda t: (0, 0)),
                  pl.BlockSpec((D_MODEL // 2, D_MODEL), lambda t: (0, 0), pipeline_mode=pl.Buffered(1)),
                  pl.BlockSpec((TM_MIX, D_MODEL), lambda t: (jnp.minimum(prj(t), n_hin - 1), 0)),
                  pl.BlockSpec((BLOCK, D_MODEL), lambda t: (hin_tail.shape[0] // BLOCK - 1, 0)),
                  pl.BlockSpec((DEPTH, D_MODEL), lambda t: (0, 0)),
                  pl.BlockSpec((DEPTH, D_MODEL), lambda t: (0, 0))],
        out_specs=(pl.BlockSpec((TM_MIX, D_MODEL), lambda t: (prj(t), 0)),
                   pl.BlockSpec((TM_MIX, D_MODEL), lambda t: (prj(t), 0)),
                   pl.BlockSpec((1, KV_DIM, WINDOW), lambda t: (seq_of(t), 0, 0)),
                   pl.BlockSpec((1, KV_DIM, WINDOW), lambda t: (seq_of(t), 0, 0)),
                   pl.BlockSpec((1, CONV_WIDTH - 1, CONV_DIM), lambda t: (seq_of(t), 0, 0))),
        scratch_shapes=[pltpu.VMEM((NB_MIX, BLOCK + 8, CONV_DIM), F32),
                        pltpu.VMEM((TM_MIX, D_MODEL), BF16),
                        pltpu.VMEM((TM_MIX, D_MODEL), BF16)],
        compiler_params=_cparams(("arbitrary",)),
        name="mixer",
    )(sinks, z1, z1, u, u, bias3, bias3, cw, mix_s, w_out_b, hin, hin_tail, g_post, g_next)


S_CHUNK = 8
SH_ROWS = S_CHUNK * N_HEADS


def _sample_kernel(zs_ref, us_ref, kvt_ref, kc_ref, vc_ref, st_ref, cw_ref, sinkt_ref, tabt_ref,
                   brow_ref, fold_ref, unfold_ref, kall_ref, vall_ref,
                   mixs_ref, kout_ref, vout_ref, cs_ref, *, layer):
    del kall_ref, vall_ref
    z = zs_ref[...]
    q = z[:, 0:Q_DIM] * SCALE
    qrep = jnp.concatenate([jnp.broadcast_to(q[s:s + 1, :], (N_HEADS, Q_DIM))
                            for s in range(S_CHUNK)], axis=0)
    row_head = lax.broadcasted_iota(jnp.int32, (SH_ROWS, Q_DIM), 0) % N_HEADS
    own_q = lax.broadcasted_iota(jnp.int32, (SH_ROWS, Q_DIM), 1) // HEAD_DIM == row_head
    qblk = jnp.dot(jnp.where(own_q, qrep, 0.0).astype(BF16), fold_ref[...],
                   preferred_element_type=F32)
    own_kv = (lax.broadcasted_iota(jnp.int32, (N_HEADS, KV_DIM), 1) // HEAD_DIM
              == lax.broadcasted_iota(jnp.int32, (N_HEADS, KV_DIM), 0) // GROUP)

    brow = brow_ref[...]
    tabt = tabt_ref[...]
    bias_t = jnp.zeros((N_HEADS, W_BUF), F32)
    for b in range(NUM_BUCKETS):
        bias_t = jnp.where(brow == b, tabt[:, b:b + 1], bias_t)
    bias0 = tabt[:, 0:1]
    sink = sinkt_ref[:, layer:layer + 1]
    last_key = lax.broadcasted_iota(jnp.int32, (KV_DIM, W_BUF), 1) == W_BUF - 1

    tile_s = lambda x: jnp.concatenate([x] * S_CHUNK, axis=0)
    rep_s = lambda c0, c1: jnp.concatenate(
        [jnp.broadcast_to(z[s:s + 1, c0:c1], (N_HEADS, c1 - c0)) for s in range(S_CHUNK)], axis=0)
    sample_rows = lambda s: slice(s * KV_DIM, (s + 1) * KV_DIM)
    head_rows = lambda s: slice(s * N_HEADS, (s + 1) * N_HEADS)

    qblk_b = qblk.astype(BF16)
    sc = jnp.concatenate(
        [jnp.dot(qblk_b[head_rows(s), :], kc_ref[sample_rows(s), :].astype(BF16),
                 preferred_element_type=F32) for s in range(S_CHUNK)], axis=0) + tile_s(bias_t)
    sn = (jnp.sum(qblk * rep_s(Q_DIM, Q_DIM + KV_DIM), axis=-1, keepdims=True)
          + tile_s(bias0))
    sink_all = tile_s(sink)
    m = jnp.maximum(jnp.maximum(jnp.max(sc, axis=-1, keepdims=True), sn), sink_all)
    e = jnp.exp(sc - m)
    en = jnp.exp(sn - m)
    den = jnp.sum(e, axis=-1, keepdims=True) + en + jnp.exp(sink_all - m)
    p = (e / den).astype(BF16)
    o_all = jnp.concatenate(
        [lax.dot_general(p[head_rows(s), :], vc_ref[sample_rows(s), :].astype(BF16),
                         (((1,), (1,)), ((), ())), preferred_element_type=F32)
         for s in range(S_CHUNK)], axis=0)
    o_all = jnp.where(tile_s(own_kv), o_all + (en / den) * rep_s(Q_DIM + KV_DIM, Q_DIM + 2 * KV_DIM),
                      0.0)

    for s in range(S_CHUNK):
        rows = sample_rows(s)
        kout_ref[rows, :] = jnp.where(last_key, kvt_ref[0, 0:KV_DIM, s:s + 1],
                                      pltpu.roll(kc_ref[rows, :], W_BUF - 1, 1))
        vout_ref[rows, :] = jnp.where(last_key, kvt_ref[0, KV_DIM:2 * KV_DIM, s:s + 1],
                                      pltpu.roll(vc_ref[rows, :], W_BUF - 1, 1))

    a_all = jnp.dot(o_all.astype(BF16), unfold_ref[...],
                    preferred_element_type=F32)
    a_all = jnp.where(own_q, a_all, 0.0)
    mixs_ref[:, 0:Q_DIM] = jnp.sum(a_all.reshape(S_CHUNK, N_HEADS, Q_DIM), axis=1)

    gb = z[:, Q_DIM + 2 * KV_DIM:Z1_DIM]
    u = us_ref[...]
    s0 = st_ref[:, 0:CONV_DIM]
    s1 = st_ref[:, CONV_DIM:]
    mixs_ref[:, Q_DIM:] = gb * (cw_ref[0:1, :] * s0 + cw_ref[1:2, :] * s1 + cw_ref[2:3, :] * u)
    cs_ref[:, 0:CONV_DIM] = s1
    cs_ref[:, CONV_DIM:] = u


def _sample_mixer(layer, z1, u, kvt, cache_kt, cache_vt, state2d, cw, sinkt, tabt, brow,
                  fold, unfold, k_all, v_all):
    n_chunks = DEC_BATCH // S_CHUNK
    row_blk0 = M_X // S_CHUNK
    full = lambda shape: pl.BlockSpec(shape, lambda c: (0,) * len(shape))
    n_in = 14
    return pl.pallas_call(
        functools.partial(_sample_kernel, layer=layer),
        out_shape=(jax.ShapeDtypeStruct((DEC_BATCH, D_MODEL), F32),
                   jax.ShapeDtypeStruct(k_all.shape, F32),
                   jax.ShapeDtypeStruct(v_all.shape, F32),
                   jax.ShapeDtypeStruct((DEC_BATCH, 2 * CONV_DIM), F32)),
        input_output_aliases={n_in - 2: 1, n_in - 1: 2},
        grid=(n_chunks,),
        in_specs=[pl.BlockSpec((S_CHUNK, Z1_DIM), lambda c: (row_blk0 + c, 0)),
                  pl.BlockSpec((S_CHUNK, CONV_DIM), lambda c: (row_blk0 + c, 0)),
                  pl.BlockSpec((1, 2 * KV_DIM, S_CHUNK), lambda c: (c, 0, 0)),
                  pl.BlockSpec((S_CHUNK * KV_DIM, W_BUF), lambda c: (layer * n_chunks + c, 0)),
                  pl.BlockSpec((S_CHUNK * KV_DIM, W_BUF), lambda c: (layer * n_chunks + c, 0)),
                  pl.BlockSpec((S_CHUNK, 2 * CONV_DIM), lambda c: (layer * n_chunks + c, 0)),
                  full((CONV_WIDTH, CONV_DIM)),
                  full((N_HEADS, DEPTH)),
                  full((N_HEADS, NUM_BUCKETS)),
                  full((1, W_BUF)),
                  full((Q_DIM, KV_DIM)),
                  full((KV_DIM, Q_DIM)),
                  pl.BlockSpec(memory_space=pl.ANY),
                  pl.BlockSpec(memory_space=pl.ANY)],
        out_specs=(pl.BlockSpec((S_CHUNK, D_MODEL), lambda c: (c, 0)),
                   pl.BlockSpec((S_CHUNK * KV_DIM, W_BUF), lambda c: (layer * n_chunks + c, 0)),
                   pl.BlockSpec((S_CHUNK * KV_DIM, W_BUF), lambda c: (layer * n_chunks + c, 0)),
                   pl.BlockSpec((S_CHUNK, 2 * CONV_DIM), lambda c: (c, 0))),
        compiler_params=_cparams(("arbitrary",)),
        name="sample_mixer",
    )(z1, u, kvt, cache_kt, cache_vt, state2d, cw, sinkt, tabt, brow, fold, unfold, k_all, v_all)


def _t5_bucket(d):
    max_exact = NUM_BUCKETS // 2
    df = jnp.maximum(d, 1).astype(F32)
    large = max_exact + (jnp.log(df / max_exact) / math.log(MAX_DISTANCE / max_exact)
                         * (NUM_BUCKETS - max_exact)).astype(jnp.int32)
    large = jnp.minimum(large, NUM_BUCKETS - 1)
    return jnp.where(d < max_exact, d, large)


def _band_tables():
    i = np.arange(BLOCK)[:, None]
    j = np.arange(2 * BLOCK)[None, :]
    d = BLOCK + i - j
    band = (d >= 0) & (d <= WINDOW)
    valid = np.stack([band & (j >= BLOCK + META_ROW0), band & (j >= META_ROW0), band])
    bucket = _t5_bucket(jnp.asarray(np.maximum(d, 0), jnp.int32))
    return bucket, jnp.asarray(valid.astype(np.int32))


def _fold_tables():
    d = np.arange(HEAD_DIM)
    fold = np.zeros((Q_DIM, KV_DIM), np.float32)
    for h in range(N_HEADS):
        fold[h * HEAD_DIM + d, (h // GROUP) * HEAD_DIM + d] = 1.0
    return jnp.asarray(fold, BF16), jnp.asarray(fold.T, BF16)


def kernel(x_prompt, x_sample, cache_k, cache_v, state_conv, meta_tokens, rel_bias, w_in, conv_w,
           attn_sinks, w_out, norm_pre_mix, norm_post_mix, norm_pre_ffn, norm_post_ffn,
           w_gate, w_up, w_down):
    tail = jnp.concatenate([x_sample.reshape(DEC_BATCH, D_MODEL),
                            jnp.zeros((META_ROW0 - DEC_BATCH, D_MODEL), F32),
                            meta_tokens.astype(F32)], axis=0)

    bucket, valid = _band_tables()
    fold, unfold = _fold_tables()
    bias3 = _band_bias(rel_bias, bucket, valid)
    brow = _t5_bucket(jnp.asarray(W_BUF - np.arange(W_BUF), jnp.int32)).reshape(1, W_BUF)

    kv_lanes = lambda c: jnp.transpose(c, (0, 1, 3, 4, 2)).reshape(DEPTH * DEC_BATCH * KV_DIM, W_BUF)
    cache_kt = kv_lanes(cache_k)
    cache_vt = kv_lanes(cache_v)
    state2d = state_conv.reshape(DEPTH * DEC_BATCH, 2 * CONV_DIM)

    x2d = x_prompt.reshape(M_X, D_MODEL)
    xn = _norm(x2d, tail, norm_pre_mix)
    h, h_tail = x2d, tail

    def kv_rows(t, n):
        return jnp.transpose(t.reshape(n, N_KV_HEADS, HEAD_DIM, WINDOW), (0, 3, 1, 2))

    k_all = jnp.zeros(cache_kt.shape, F32)
    v_all = jnp.zeros(cache_vt.shape, F32)
    kp, vp, cp, cs = [], [], [], []
    for l in range(DEPTH):
        z1, w_out_b = _matmul(xn, w_in, w_out, l, Z1_DIM)
        (u,) = _gated(xn, w_in, w_in, None, l, Z1_DIM, Z1_DIM + CONV_DIM, CONV_DIM,
                      False, F32, "conv_in")
        kvt = jnp.transpose(z1[M_X:M_X + DEC_BATCH, Q_DIM:Q_DIM + 2 * KV_DIM]
                            .reshape(DEC_BATCH // S_CHUNK, S_CHUNK, 2 * KV_DIM), (0, 2, 1))
        mix_s, k_all, v_all, c_new = _sample_mixer(
            l, z1, u, kvt, cache_kt, cache_vt, state2d, conv_w[l], attn_sinks.T, rel_bias.T,
            brow, fold, unfold, k_all, v_all)
        h, xn, kp_l, vp_l, cp_l = _mixer(l, attn_sinks[l], z1, u, bias3, conv_w[l], mix_s,
                                         w_out_b, h, h_tail, norm_post_mix, norm_pre_ffn)
        act, w_down_b = _gated(xn, w_gate, w_up, w_down, l, 0, 0, D_FF, True, BF16, "ffn_up")
        if l + 1 < DEPTH:
            h, xn = _proj_norm(act, w_down_b, l, h, norm_post_ffn, norm_pre_mix, l + 1, "ffn_down")
            h_tail = h
        else:
            y2d, y_tail = _proj_last(act, w_down_b, l, h, norm_post_ffn)

        kp.append(kv_rows(kp_l, BATCH))
        vp.append(kv_rows(vp_l, BATCH))
        cp.append(cp_l)
        cs.append(c_new.reshape(DEC_BATCH, CONV_WIDTH - 1, CONV_DIM))

    y_prompt = y2d.reshape(BATCH, SEQ, D_MODEL)
    y_sample = y_tail[0:DEC_BATCH].reshape(DEC_BATCH, 1, D_MODEL)
    k_sample = kv_rows(k_all, DEPTH * DEC_BATCH).reshape(DEPTH, DEC_BATCH, W_BUF, N_KV_HEADS, HEAD_DIM)
    v_sample = kv_rows(v_all, DEPTH * DEC_BATCH).reshape(DEPTH, DEC_BATCH, W_BUF, N_KV_HEADS, HEAD_DIM)
    return (y_prompt, y_sample, jnp.stack(kp), jnp.stack(vp), jnp.stack(cp),
            k_sample, v_sample, jnp.stack(cs))
```

```python
import functools
import math

import numpy as np
import jax
import jax.numpy as jnp
from jax import lax
from jax.experimental import pallas as pl
from jax.experimental.pallas import tpu as pltpu

D_MODEL = 2048
BATCH = 4
SEQ = 2048
DEPTH = 4
DEC_BATCH = 32
HEAD_DIM = 64
N_HEADS = 16
N_KV_HEADS = 4
GROUP = 4
Q_DIM = 1024
KV_DIM = 256
CONV_DIM = 1024
CONV_WIDTH = 3
IN_DIM = 4608
WINDOW = 128
BLOCK = 128
NUM_BUCKETS = 32
MAX_DISTANCE = 128
N_META = 16
D_FF = 5632
RMS_EPS = 1e-6
SCALE = HEAD_DIM ** -0.5
W_BUF = 128

M_X = BATCH * SEQ
N_XBLK = M_X // BLOCK
BLK_PER_SEQ = SEQ // BLOCK
M_ALL = M_X + BLOCK
META_ROW0 = BLOCK - N_META
Z1_DIM = Q_DIM + 2 * KV_DIM + CONV_DIM
NEG = -1e30
LOG2E = math.log2(math.e)

TM_BIG = 1664
TM_EPI = 640
N_EPI = M_ALL // TM_EPI
EPI_LAST_X = M_X - (N_EPI - 1) * TM_EPI
assert (M_X - EPI_LAST_X) % EPI_LAST_X == 0 and EPI_LAST_X + BLOCK == TM_EPI
TM_PROJ = 416
PROJ_ROW_CHUNKS = ((0, 144), (144, 288), (288, 416))
N_PROJ = M_ALL // TM_PROJ
PROJ_LAST_X = M_X - (N_PROJ - 1) * TM_PROJ
assert PROJ_LAST_X + BLOCK == TM_PROJ and PROJ_LAST_X % 8 == 0
TN = 512
GATED_CHUNKS = 13
TM_IN = 832
TN_IN = 1280
VMEM_LIMIT = 56 * 1024 * 1024

F32 = jnp.float32
BF16 = jnp.bfloat16


def _cparams(sem):
    return pltpu.CompilerParams(dimension_semantics=sem, vmem_limit_bytes=VMEM_LIMIT)


def _rms(x, g):
    return x * lax.rsqrt(jnp.mean(x * x, axis=-1, keepdims=True) + RMS_EPS) * g


def _norm_kernel(xa_ref, xb_ref, tail_ref, g_ref, xn_ref):
    last = pl.program_id(0) == N_EPI - 1
    h_last = jnp.concatenate([xb_ref[...], tail_ref[...]], axis=0)
    h = jnp.where(last, h_last, xa_ref[...])
    xn_ref[...] = _rms(h, g_ref[0:1, :]).astype(BF16)


def _norm(x2d, tail, g):
    return pl.pallas_call(
        _norm_kernel,
        out_shape=jax.ShapeDtypeStruct((M_ALL, D_MODEL), BF16),
        grid=(N_EPI,),
        in_specs=[pl.BlockSpec((TM_EPI, D_MODEL), lambda i: (jnp.minimum(i, N_EPI - 2), 0)),
                  pl.BlockSpec((EPI_LAST_X, D_MODEL), lambda i: ((M_X - EPI_LAST_X) // EPI_LAST_X, 0)),
                  pl.BlockSpec((BLOCK, D_MODEL), lambda i: (0, 0)),
                  pl.BlockSpec((DEPTH, D_MODEL), lambda i: (0, 0))],
        out_specs=pl.BlockSpec((TM_EPI, D_MODEL), lambda i: (i, 0)),
        compiler_params=_cparams(("arbitrary",)),
        name="norm0",
    )(x2d, x2d, tail, g)


RIDER_ROWS = 128


def _pack_bf16(w_f32):
    return pltpu.bitcast(w_f32.astype(BF16), jnp.uint32)


def _unpack_bf16(w_u32):
    return pltpu.bitcast(w_u32, BF16)


def _rider(wr, layer, n_outer, n_inner):
    n_chunks = wr.shape[1] // RIDER_ROWS
    assert n_chunks <= n_outer * n_inner
    chunk = lambda j, i: jnp.minimum(j * n_inner + i, n_chunks - 1)
    in_spec = pl.BlockSpec((None, RIDER_ROWS, D_MODEL), lambda j, i: (layer, chunk(j, i), 0))
    out_spec = pl.BlockSpec((RIDER_ROWS // 2, D_MODEL), lambda j, i: (chunk(j, i), 0))
    return in_spec, out_spec, jax.ShapeDtypeStruct((wr.shape[1] // 2, D_MODEL), jnp.uint32)


def _matmul_kernel(x_ref, w_ref, wr_ref, o_ref, wrb_ref, wbf_ref):
    @pl.when(pl.program_id(1) == 0)
    def _():
        wbf_ref[...] = w_ref[...].astype(BF16)

    o_ref[...] = jnp.dot(x_ref[...], wbf_ref[...], preferred_element_type=F32)
    wrb_ref[...] = _pack_bf16(wr_ref[...])


def _matmul(x, w, wr, layer, n_cols):
    k = x.shape[1]
    n_outer, n_inner = n_cols // TN_IN, M_ALL // TM_IN
    r_in, r_out, r_shape = _rider(wr, layer, n_outer, n_inner)
    return pl.pallas_call(
        _matmul_kernel,
        out_shape=(jax.ShapeDtypeStruct((M_ALL, n_cols), F32), r_shape),
        grid=(n_outer, n_inner),
        in_specs=[pl.BlockSpec((TM_IN, k), lambda j, i: (i, 0)),
                  pl.BlockSpec((None, k, TN_IN), lambda j, i: (layer, 0, j)),
                  r_in],
        out_specs=(pl.BlockSpec((TM_IN, TN_IN), lambda j, i: (i, j)), r_out),
        scratch_shapes=[pltpu.VMEM((k, TN_IN), BF16)],
        compiler_params=_cparams(("arbitrary", "arbitrary")),
        name="in_proj",
    )(x, w, wr)


def _gated_kernel(*refs, silu, has_rider):
    if has_rider:
        x_ref, wa_ref, wb_ref, wr_ref, o_ref, wrb_ref, wbf_ref = refs
    else:
        x_ref, wa_ref, wb_ref, o_ref, wbf_ref = refs

    @pl.when(pl.program_id(1) == 0)
    def _():
        wbf_ref[:, 0:TN] = wa_ref[...].astype(BF16)
        wbf_ref[:, TN:2 * TN] = wb_ref[...].astype(BF16)

    rc = TM_BIG // GATED_CHUNKS
    for c in range(GATED_CHUNKS):
        rows = slice(c * rc, (c + 1) * rc)
        ab = jnp.dot(x_ref[rows, :], wbf_ref[...], preferred_element_type=F32)
        a = ab[:, 0:TN]
        b = ab[:, TN:2 * TN]
        if silu:
            a = a * (1.0 / (1.0 + jnp.exp(-a)))
        o_ref[rows, :] = (a * b).astype(o_ref.dtype)
    if has_rider:
        wrb_ref[...] = _pack_bf16(wr_ref[...])


def _gated(x, wa, wb, wr, layer, a_col0, b_col0, n_cols, silu, out_dtype, name):
    k = x.shape[1]
    a0 = a_col0 // TN
    b0 = b_col0 // TN
    n_outer, n_inner = n_cols // TN, M_ALL // TM_BIG
    in_specs = [pl.BlockSpec((TM_BIG, k), lambda j, i: (i, 0)),
                pl.BlockSpec((None, k, TN), lambda j, i: (layer, 0, a0 + j)),
                pl.BlockSpec((None, k, TN), lambda j, i: (layer, 0, b0 + j))]
    out_specs = [pl.BlockSpec((TM_BIG, TN), lambda j, i: (i, j))]
    out_shape = [jax.ShapeDtypeStruct((M_ALL, n_cols), out_dtype)]
    args = [x, wa, wb]
    if wr is not None:
        r_in, r_out, r_shape = _rider(wr, layer, n_outer, n_inner)
        in_specs.append(r_in)
        out_specs.append(r_out)
        out_shape.append(r_shape)
        args.append(wr)
    return pl.pallas_call(
        functools.partial(_gated_kernel, silu=silu, has_rider=wr is not None),
        out_shape=tuple(out_shape),
        grid=(n_outer, n_inner),
        in_specs=in_specs,
        out_specs=tuple(out_specs),
        scratch_shapes=[pltpu.VMEM((k, 2 * TN), BF16)],
        compiler_params=_cparams(("arbitrary", "arbitrary")),
        name=name,
    )(*args)


W_CHUNKS = 4


def _proj_tiles(a_ref, w_hbm, w_scr, sems, emit):
    kr = w_scr.shape[0] // W_CHUNKS

    @pl.when(pl.program_id(0) == 0)
    def _():
        copies = [pltpu.make_async_copy(w_hbm.at[pl.ds(c * kr, kr), :], w_scr.at[pl.ds(c * kr, kr), :],
                                        sems.at[c]) for c in range(W_CHUNKS)]
        for cp in copies:
            cp.start()
        for n, (r0, r1) in enumerate(PROJ_ROW_CHUNKS):
            y = None
            for c, cp in enumerate(copies):
                if n == 0:
                    cp.wait()
                part = jnp.dot(a_ref[r0:r1, 2 * c * kr:2 * (c + 1) * kr],
                               _unpack_bf16(w_scr[c * kr:(c + 1) * kr, :]), preferred_element_type=F32)
                y = part if y is None else y + part
            emit(slice(r0, r1), y)

    @pl.when(pl.program_id(0) > 0)
    def _():
        for r0, r1 in PROJ_ROW_CHUNKS:
            emit(slice(r0, r1), jnp.dot(a_ref[r0:r1, :], _unpack_bf16(w_scr[...]),
                                        preferred_element_type=F32))


def _proj_norm_kernel(a_ref, w_hbm, hin_ref, gpost_ref, gnext_ref, h_ref, xn_ref, w_scr, sems,
                      *, l_post, l_next):
    g_post = gpost_ref[l_post:l_post + 1, :]
    g_next = gnext_ref[l_next:l_next + 1, :]

    def emit(rows, y):
        hn = hin_ref[rows, :] + _rms(y, g_post)
        h_ref[rows, :] = hn
        xn_ref[rows, :] = _rms(hn, g_next).astype(BF16)

    _proj_tiles(a_ref, w_hbm, w_scr, sems, emit)


def _proj_norm(a, w, layer, hin, g_post, g_next, l_next, name):
    k = a.shape[1]
    return pl.pallas_call(
        functools.partial(_proj_norm_kernel, l_post=layer, l_next=l_next),
        out_shape=(jax.ShapeDtypeStruct((M_ALL, D_MODEL), F32),
                   jax.ShapeDtypeStruct((M_ALL, D_MODEL), BF16)),
        grid=(M_ALL // TM_PROJ,),
        in_specs=[pl.BlockSpec((TM_PROJ, k), lambda i: (i, 0)),
                  pl.BlockSpec(memory_space=pl.ANY),
                  pl.BlockSpec((TM_PROJ, D_MODEL), lambda i: (i, 0)),
                  pl.BlockSpec((DEPTH, D_MODEL), lambda i: (0, 0)),
                  pl.BlockSpec((DEPTH, D_MODEL), lambda i: (0, 0))],
        out_specs=(pl.BlockSpec((TM_PROJ, D_MODEL), lambda i: (i, 0)),
                   pl.BlockSpec((TM_PROJ, D_MODEL), lambda i: (i, 0))),
        scratch_shapes=[pltpu.VMEM((k // 2, D_MODEL), jnp.uint32),
                        pltpu.SemaphoreType.DMA((W_CHUNKS,))],
        compiler_params=_cparams(("arbitrary",)),
        name=name,
    )(a, w, hin, g_post, g_next)


def _proj_last_kernel(a_ref, w_hbm, hin_ref, gpost_ref, y_ref, tail_ref, w_scr, sems, *, l_post):
    g_post = gpost_ref[l_post:l_post + 1, :]

    def emit(rows, y):
        y_ref[rows, :] = hin_ref[rows, :] + _rms(y, g_post)

    _proj_tiles(a_ref, w_hbm, w_scr, sems, emit)

    @pl.when(pl.program_id(0) == N_PROJ - 1)
    def _():
        tail_ref[...] = y_ref[PROJ_LAST_X:TM_PROJ, :]


def _proj_last(a, w, layer, hin, g_post):
    k = a.shape[1]
    return pl.pallas_call(
        functools.partial(_proj_last_kernel, l_post=layer),
        out_shape=(jax.ShapeDtypeStruct((M_X, D_MODEL), F32),
                   jax.ShapeDtypeStruct((BLOCK, D_MODEL), F32)),
        grid=(N_PROJ,),
        in_specs=[pl.BlockSpec((TM_PROJ, k), lambda i: (i, 0)),
                  pl.BlockSpec(memory_space=pl.ANY),
                  pl.BlockSpec((TM_PROJ, D_MODEL), lambda i: (i, 0)),
                  pl.BlockSpec((DEPTH, D_MODEL), lambda i: (0, 0))],
        out_specs=(pl.BlockSpec((TM_PROJ, D_MODEL), lambda i: (i, 0)),
                   pl.BlockSpec((BLOCK, D_MODEL), lambda i: (0, 0))),
        scratch_shapes=[pltpu.VMEM((k // 2, D_MODEL), jnp.uint32),
                        pltpu.SemaphoreType.DMA((W_CHUNKS,))],
        compiler_params=_cparams(("arbitrary",)),
        name="ffn_down_last",
    )(a, w, hin, g_post)


def _band_bias_kernel(tab_ref, bucket_ref, valid_ref, o_ref):
    h = pl.program_id(0)
    bucket = bucket_ref[...]
    acc = jnp.zeros((BLOCK, 2 * BLOCK), F32)
    for b in range(NUM_BUCKETS):
        acc = jnp.where(bucket == b, tab_ref[b, h], acc)
    acc = acc * LOG2E
    for v in range(3):
        o_ref[v, 0] = jnp.where(valid_ref[v] != 0, acc, NEG)


def _band_bias(rel_bias, bucket, valid):
    return pl.pallas_call(
        _band_bias_kernel,
        out_shape=jax.ShapeDtypeStruct((3, N_HEADS, BLOCK, 2 * BLOCK), F32),
        grid=(N_HEADS,),
        in_specs=[pl.BlockSpec(memory_space=pltpu.SMEM),
                  pl.BlockSpec((BLOCK, 2 * BLOCK), lambda h: (0, 0)),
                  pl.BlockSpec((3, BLOCK, 2 * BLOCK), lambda h: (0, 0, 0))],
        out_specs=pl.BlockSpec((3, 1, BLOCK, 2 * BLOCK), lambda h: (0, h, 0, 0)),
        compiler_params=_cparams(("arbitrary",)),
        name="band_bias",
    )(rel_bias, bucket, valid)


NB_MIX = 2
TM_MIX = NB_MIX * BLOCK
N_GRP = -(-(N_XBLK + 1) // NB_MIX)


def _mix_block(b, kv_prev, u_prev8, is_tail, bias_ref, sink_ref, z_ref, u_ref, cw_ref, ext_ref,
               mix_ref):
    rows = slice(b * BLOCK, (b + 1) * BLOCK)

    lo_kv = lax.broadcasted_iota(jnp.int32, (2 * BLOCK, BLOCK), 1) < HEAD_DIM
    one_lo = jnp.where(lo_kv, 1.0, 0.0).astype(BF16)
    one_hi = jnp.where(lo_kv, 0.0, 1.0).astype(BF16)
    col_cache = {}

    def kv_tables(kh):
        col, in_hi = kh // 2, kh % 2
        if col not in col_cache:
            c0 = col * BLOCK
            kc = jnp.concatenate([kv_prev[:, c0:c0 + BLOCK],
                                  z_ref[rows, Q_DIM + c0:Q_DIM + c0 + BLOCK]], axis=0)
            vc = jnp.concatenate([kv_prev[:, KV_DIM + c0:KV_DIM + c0 + BLOCK],
                                  z_ref[rows, Q_DIM + KV_DIM + c0:Q_DIM + KV_DIM + c0 + BLOCK]], axis=0)
            col_cache[col] = (kc, vc, pltpu.roll(kc, HEAD_DIM, 1), pltpu.roll(vc, HEAD_DIM, 1))
        kc, vc, ks, vs = col_cache[col]
        own_k, oth_k = (ks, kc) if in_hi else (kc, ks)
        own_v, oth_v = (vs, vc) if in_hi else (vc, vs)
        k2 = jnp.where(lo_kv, own_k, oth_k).astype(BF16)
        rv = jnp.concatenate(
            [jnp.concatenate([jnp.where(lo_kv, own_v, 0.0).astype(BF16), one_lo], axis=1),
             jnp.concatenate([jnp.where(lo_kv, 0.0, oth_v).astype(BF16), one_hi], axis=1)], axis=0)
        return k2, rv

    lane = lax.broadcasted_iota(jnp.int32, (BLOCK, BLOCK), 1)
    lo_half = lane < HEAD_DIM
    zero = jnp.zeros((BLOCK, BLOCK), BF16)
    k2, rv = {}, {}
    for p in range(N_HEADS // 2):
        kh = p // 2
        if kh not in k2:
            k2[kh], rv[kh] = kv_tables(kh)
        qp = (z_ref[rows, p * BLOCK:(p + 1) * BLOCK] * (SCALE * LOG2E)).astype(BF16)
        es, ms, sks = [], [], []
        for half in range(2):
            h = 2 * p + half
            qm = jnp.where(lo_half if half == 0 else jnp.logical_not(lo_half), qp, zero)
            s = lax.dot_general(qm, k2[kh], (((1,), (1,)), ((), ())),
                                preferred_element_type=F32) + bias_ref[0, h]
            sk = sink_ref[h] * LOG2E
            m = jnp.maximum(jnp.max(s, axis=-1, keepdims=True), sk)
            es.append(jnp.exp2(s - m).astype(BF16))
            ms.append(m)
            sks.append(sk)
        acc = jnp.dot(jnp.concatenate(es, axis=1), rv[kh], preferred_element_type=F32)
        sink_term = jnp.exp2(jnp.where(lo_half, sks[0], sks[1]) - jnp.where(lo_half, ms[0], ms[1]))
        den = acc[:, BLOCK:] + sink_term
        mix_ref[rows, p * BLOCK:(p + 1) * BLOCK] = (acc[:, :BLOCK] / den).astype(BF16)

    ext_ref[b, 0:8, :] = u_prev8
    ext_ref[b, 8:8 + BLOCK, :] = u_ref[rows, :]
    u1 = ext_ref[b, 7:7 + BLOCK, :]
    u2 = ext_ref[b, 6:6 + BLOCK, :]
    if is_tail is not None:
        row = lax.broadcasted_iota(jnp.int32, (BLOCK, CONV_DIM), 0)
        first = jnp.where(is_tail, META_ROW0, -8)
        u1 = jnp.where(row >= first + 1, u1, 0.0)
        u2 = jnp.where(row >= first + 2, u2, 0.0)
    gb = z_ref[rows, Q_DIM + 2 * KV_DIM:Z1_DIM]
    c = gb * (cw_ref[0:1, :] * u2 + cw_ref[1:2, :] * u1 + cw_ref[2:3, :] * u_ref[rows, :])
    mix_ref[rows, Q_DIM:] = c.astype(BF16)


def _mixer_kernel(sink_ref, z_ref, zp_ref, u_ref, up_ref, bias0_ref, bias_ref, cw_ref, mixs_ref,
                  w_ref, hin_ref, hint_ref, gpost_ref, gnext_ref,
                  h_ref, xn_ref, kp_ref, vp_ref, cp_ref, ext_ref, mix_ref, lhs_ref, *, layer):
    t = pl.program_id(0)
    i0 = jnp.minimum(t, N_GRP - 1) * NB_MIX
    is_tail = i0 == N_XBLK

    @pl.when(t == 0)
    def _():
        mix_ref[...] = jnp.zeros_like(mix_ref)

    lhs_ref[...] = mix_ref[...]
    half_n = D_MODEL // 2
    y_parts = []

    for b in range(NB_MIX):
        y_parts.append(jnp.dot(lhs_ref[...], _unpack_bf16(w_ref[:, b * half_n:(b + 1) * half_n]),
                               preferred_element_type=F32))
        if b == 0:
            _mix_block(0, zp_ref[...], up_ref[...], is_tail, bias0_ref, sink_ref, z_ref, u_ref,
                       cw_ref, ext_ref, mix_ref)
        else:
            prev = slice((b - 1) * BLOCK, b * BLOCK)
            _mix_block(b, z_ref[prev, Q_DIM:Q_DIM + 2 * KV_DIM], u_ref[b * BLOCK - 8:b * BLOCK, :],
                       None, bias_ref, sink_ref, z_ref, u_ref, cw_ref, ext_ref, mix_ref)

    y = jnp.concatenate(y_parts, axis=1)
    hin = hin_ref[...]
    if layer == 0:
        hin = jnp.where(t - 1 == N_GRP - 1, jnp.concatenate([hint_ref[...]] * NB_MIX, axis=0), hin)
    hn = hin + _rms(y, gpost_ref[layer:layer + 1, :])
    h_ref[...] = hn
    xn_ref[...] = _rms(hn, gnext_ref[layer:layer + 1, :]).astype(BF16)

    @pl.when(is_tail)
    def _():
        mix_ref[0:DEC_BATCH, :] = mixs_ref[...].astype(BF16)

    last = i0 + NB_MIX - 1
    @pl.when(jnp.logical_and(last < N_XBLK, last % BLK_PER_SEQ == BLK_PER_SEQ - 1))
    def _():
        rows = slice((NB_MIX - 1) * BLOCK, NB_MIX * BLOCK)
        kp_ref[0] = z_ref[rows, Q_DIM:Q_DIM + KV_DIM].T
        vp_ref[0] = z_ref[rows, Q_DIM + KV_DIM:Q_DIM + 2 * KV_DIM].T
        cp_ref[0] = u_ref[TM_MIX - (CONV_WIDTH - 1):TM_MIX, :]


def _prev_blk(i):
    return jnp.where(i % BLK_PER_SEQ == 0, N_XBLK, i - 1)


def _bias_variant(i):
    return jnp.where(i == N_XBLK, 0, jnp.where(i % BLK_PER_SEQ == 0, 1, 2))


def _mixer(layer, sinks, z1, u, bias3, cw, mix_s, w_out_b, hin, hin_tail, g_post, g_next):
    assert BLK_PER_SEQ % NB_MIX == 0
    rows8 = BLOCK // 8
    grp = lambda t: jnp.minimum(t, N_GRP - 1)
    blk0 = lambda t: grp(t) * NB_MIX
    prj = lambda t: jnp.maximum(t - 1, 0)
    n_hin = -(-hin.shape[0] // TM_MIX)
    seq_of = lambda t: jnp.minimum(blk0(t) // BLK_PER_SEQ, BATCH - 1)
    bias_spec = lambda f, **kw: pl.BlockSpec((1, N_HEADS, BLOCK, 2 * BLOCK), f, **kw)
    return pl.pallas_call(
        functools.partial(_mixer_kernel, layer=layer),
        out_shape=(jax.ShapeDtypeStruct((M_ALL, D_MODEL), F32),
                   jax.ShapeDtypeStruct((M_ALL, D_MODEL), BF16),
                   jax.ShapeDtypeStruct((BATCH, KV_DIM, WINDOW), F32),
                   jax.ShapeDtypeStruct((BATCH, KV_DIM, WINDOW), F32),
                   jax.ShapeDtypeStruct((BATCH, CONV_WIDTH - 1, CONV_DIM), F32)),
        grid=(N_GRP + 1,),
        in_specs=[pl.BlockSpec(memory_space=pltpu.SMEM),
                  pl.BlockSpec((TM_MIX, Z1_DIM), lambda t: (grp(t), 0)),
                  pl.BlockSpec((BLOCK, 2 * KV_DIM),
                               lambda t: (_prev_blk(blk0(t)), Q_DIM // (2 * KV_DIM))),
                  pl.BlockSpec((TM_MIX, CONV_DIM), lambda t: (grp(t), 0)),
                  pl.BlockSpec((8, CONV_DIM), lambda t: (_prev_blk(blk0(t)) * rows8 + rows8 - 1, 0)),
                  bias_spec(lambda t: (_bias_variant(blk0(t)), 0, 0, 0)),
                  bias_spec(lambda t: (2, 0, 0, 0), pipeline_mode=pl.Buffered(1)),
                  pl.BlockSpec((CONV_WIDTH, CONV_DIM), lambda t: (0, 0)),
                  pl.BlockSpec((DEC_BATCH, D_MODEL), lambda t: (0, 0)),
                  pl.BlockSpec((D_MODEL // 2, D_MODEL), lambda t: (0, 0), pipeline_mode=pl.Buffered(1)),
                  pl.BlockSpec((TM_MIX, D_MODEL), lambda t: (jnp.minimum(prj(t), n_hin - 1), 0)),
                  pl.BlockSpec((BLOCK, D_MODEL), lambda t: (hin_tail.shape[0] // BLOCK - 1, 0)),
                  pl.BlockSpec((DEPTH, D_MODEL), lambda t: (0, 0)),
                  pl.BlockSpec((DEPTH, D_MODEL), lambda t: (0, 0))],
        out_specs=(pl.BlockSpec((TM_MIX, D_MODEL), lambda t: (prj(t), 0)),
                   pl.BlockSpec((TM_MIX, D_MODEL), lambda t: (prj(t), 0)),
                   pl.BlockSpec((1, KV_DIM, WINDOW), lambda t: (seq_of(t), 0, 0)),
                   pl.BlockSpec((1, KV_DIM, WINDOW), lambda t: (seq_of(t), 0, 0)),
                   pl.BlockSpec((1, CONV_WIDTH - 1, CONV_DIM), lambda t: (seq_of(t), 0, 0))),
        scratch_shapes=[pltpu.VMEM((NB_MIX, BLOCK + 8, CONV_DIM), F32),
                        pltpu.VMEM((TM_MIX, D_MODEL), BF16),
                        pltpu.VMEM((TM_MIX, D_MODEL), BF16)],
        compiler_params=_cparams(("arbitrary",)),
        name="mixer",
    )(sinks, z1, z1, u, u, bias3, bias3, cw, mix_s, w_out_b, hin, hin_tail, g_post, g_next)


S_CHUNK = 16
SH_ROWS = S_CHUNK * N_HEADS


def _sample_kernel(zs_ref, us_ref, kvt_ref, kc_ref, vc_ref, st_ref, cw_ref, sinkt_ref, tabt_ref,
                   brow_ref, fold_ref, unfold_ref, kall_ref, vall_ref,
                   mixs_ref, kout_ref, vout_ref, cs_ref, *, layer):
    del kall_ref, vall_ref
    z = zs_ref[...]
    q = z[:, 0:Q_DIM] * SCALE
    qrep = jnp.concatenate([jnp.broadcast_to(q[s:s + 1, :], (N_HEADS, Q_DIM))
                            for s in range(S_CHUNK)], axis=0)
    row_head = lax.broadcasted_iota(jnp.int32, (SH_ROWS, Q_DIM), 0) % N_HEADS
    own_q = lax.broadcasted_iota(jnp.int32, (SH_ROWS, Q_DIM), 1) // HEAD_DIM == row_head
    qblk = jnp.dot(jnp.where(own_q, qrep, 0.0).astype(BF16), fold_ref[...],
                   preferred_element_type=F32)
    own_kv = (lax.broadcasted_iota(jnp.int32, (N_HEADS, KV_DIM), 1) // HEAD_DIM
              == lax.broadcasted_iota(jnp.int32, (N_HEADS, KV_DIM), 0) // GROUP)

    brow = brow_ref[...]
    tabt = tabt_ref[...]
    bias_t = jnp.zeros((N_HEADS, W_BUF), F32)
    for b in range(NUM_BUCKETS):
        bias_t = jnp.where(brow == b, tabt[:, b:b + 1], bias_t)
    bias0 = tabt[:, 0:1]
    sink = sinkt_ref[:, layer:layer + 1]
    last_key = lax.broadcasted_iota(jnp.int32, (KV_DIM, W_BUF), 1) == W_BUF - 1

    tile_s = lambda x: jnp.concatenate([x] * S_CHUNK, axis=0)
    rep_s = lambda c0, c1: jnp.concatenate(
        [jnp.broadcast_to(z[s:s + 1, c0:c1], (N_HEADS, c1 - c0)) for s in range(S_CHUNK)], axis=0)
    sample_rows = lambda s: slice(s * KV_DIM, (s + 1) * KV_DIM)
    head_rows = lambda s: slice(s * N_HEADS, (s + 1) * N_HEADS)

    qblk_b = qblk.astype(BF16)
    sc = jnp.concatenate(
        [jnp.dot(qblk_b[head_rows(s), :], kc_ref[sample_rows(s), :].astype(BF16),
                 preferred_element_type=F32) for s in range(S_CHUNK)], axis=0) + tile_s(bias_t)
    sn = (jnp.sum(qblk * rep_s(Q_DIM, Q_DIM + KV_DIM), axis=-1, keepdims=True)
          + tile_s(bias0))
    sink_all = tile_s(sink)
    m = jnp.maximum(jnp.maximum(jnp.max(sc, axis=-1, keepdims=True), sn), sink_all)
    e = jnp.exp(sc - m)
    en = jnp.exp(sn - m)
    den = jnp.sum(e, axis=-1, keepdims=True) + en + jnp.exp(sink_all - m)
    p = (e / den).astype(BF16)
    o_all = jnp.concatenate(
        [lax.dot_general(p[head_rows(s), :], vc_ref[sample_rows(s), :].astype(BF16),
                         (((1,), (1,)), ((), ())), preferred_element_type=F32)
         for s in range(S_CHUNK)], axis=0)
    o_all = jnp.where(tile_s(own_kv), o_all + (en / den) * rep_s(Q_DIM + KV_DIM, Q_DIM + 2 * KV_DIM),
                      0.0)

    for s in range(S_CHUNK):
        rows = sample_rows(s)
        kout_ref[rows, :] = jnp.where(last_key, kvt_ref[0, 0:KV_DIM, s:s + 1],
                                      pltpu.roll(kc_ref[rows, :], W_BUF - 1, 1))
        vout_ref[rows, :] = jnp.where(last_key, kvt_ref[0, KV_DIM:2 * KV_DIM, s:s + 1],
                                      pltpu.roll(vc_ref[rows, :], W_BUF - 1, 1))

    a_all = jnp.dot(o_all.astype(BF16), unfold_ref[...],
                    preferred_element_type=F32)
    a_all = jnp.where(own_q, a_all, 0.0)
    mixs_ref[:, 0:Q_DIM] = jnp.sum(a_all.reshape(S_CHUNK, N_HEADS, Q_DIM), axis=1)

    gb = z[:, Q_DIM + 2 * KV_DIM:Z1_DIM]
    u = us_ref[...]
    s0 = st_ref[:, 0:CONV_DIM]
    s1 = st_ref[:, CONV_DIM:]
    mixs_ref[:, Q_DIM:] = gb * (cw_ref[0:1, :] * s0 + cw_ref[1:2, :] * s1 + cw_ref[2:3, :] * u)
    cs_ref[:, 0:CONV_DIM] = s1
    cs_ref[:, CONV_DIM:] = u


def _sample_mixer(layer, z1, u, kvt, cache_kt, cache_vt, state2d, cw, sinkt, tabt, brow,
                  fold, unfold, k_all, v_all):
    n_chunks = DEC_BATCH // S_CHUNK
    row_blk0 = M_X // S_CHUNK
    full = lambda shape: pl.BlockSpec(shape, lambda c: (0,) * len(shape))
    n_in = 14
    return pl.pallas_call(
        functools.partial(_sample_kernel, layer=layer),
        out_shape=(jax.ShapeDtypeStruct((DEC_BATCH, D_MODEL), F32),
                   jax.ShapeDtypeStruct(k_all.shape, F32),
                   jax.ShapeDtypeStruct(v_all.shape, F32),
                   jax.ShapeDtypeStruct((DEC_BATCH, 2 * CONV_DIM), F32)),
        input_output_aliases={n_in - 2: 1, n_in - 1: 2},
        grid=(n_chunks,),
        in_specs=[pl.BlockSpec((S_CHUNK, Z1_DIM), lambda c: (row_blk0 + c, 0)),
                  pl.BlockSpec((S_CHUNK, CONV_DIM), lambda c: (row_blk0 + c, 0)),
                  pl.BlockSpec((1, 2 * KV_DIM, S_CHUNK), lambda c: (c, 0, 0)),
                  pl.BlockSpec((S_CHUNK * KV_DIM, W_BUF), lambda c: (layer * n_chunks + c, 0)),
                  pl.BlockSpec((S_CHUNK * KV_DIM, W_BUF), lambda c: (layer * n_chunks + c, 0)),
                  pl.BlockSpec((S_CHUNK, 2 * CONV_DIM), lambda c: (layer * n_chunks + c, 0)),
                  full((CONV_WIDTH, CONV_DIM)),
                  full((N_HEADS, DEPTH)),
                  full((N_HEADS, NUM_BUCKETS)),
                  full((1, W_BUF)),
                  full((Q_DIM, KV_DIM)),
                  full((KV_DIM, Q_DIM)),
                  pl.BlockSpec(memory_space=pl.ANY),
                  pl.BlockSpec(memory_space=pl.ANY)],
        out_specs=(pl.BlockSpec((S_CHUNK, D_MODEL), lambda c: (c, 0)),
                   pl.BlockSpec((S_CHUNK * KV_DIM, W_BUF), lambda c: (layer * n_chunks + c, 0)),
                   pl.BlockSpec((S_CHUNK * KV_DIM, W_BUF), lambda c: (layer * n_chunks + c, 0)),
                   pl.BlockSpec((S_CHUNK, 2 * CONV_DIM), lambda c: (c, 0))),
        compiler_params=_cparams(("arbitrary",)),
        name="sample_mixer",
    )(z1, u, kvt, cache_kt, cache_vt, state2d, cw, sinkt, tabt, brow, fold, unfold, k_all, v_all)


def _t5_bucket(d):
    max_exact = NUM_BUCKETS // 2
    df = jnp.maximum(d, 1).astype(F32)
    large = max_exact + (jnp.log(df / max_exact) / math.log(MAX_DISTANCE / max_exact)
                         * (NUM_BUCKETS - max_exact)).astype(jnp.int32)
    large = jnp.minimum(large, NUM_BUCKETS - 1)
    return jnp.where(d < max_exact, d, large)


def _band_tables():
    i = np.arange(BLOCK)[:, None]
    j = np.arange(2 * BLOCK)[None, :]
    d = BLOCK + i - j
    band = (d >= 0) & (d <= WINDOW)
    valid = np.stack([band & (j >= BLOCK + META_ROW0), band & (j >= META_ROW0), band])
    bucket = _t5_bucket(jnp.asarray(np.maximum(d, 0), jnp.int32))
    return bucket, jnp.asarray(valid.astype(np.int32))


def _fold_tables():
    d = np.arange(HEAD_DIM)
    fold = np.zeros((Q_DIM, KV_DIM), np.float32)
    for h in range(N_HEADS):
        fold[h * HEAD_DIM + d, (h // GROUP) * HEAD_DIM + d] = 1.0
    return jnp.asarray(fold, BF16), jnp.asarray(fold.T, BF16)


def kernel(x_prompt, x_sample, cache_k, cache_v, state_conv, meta_tokens, rel_bias, w_in, conv_w,
           attn_sinks, w_out, norm_pre_mix, norm_post_mix, norm_pre_ffn, norm_post_ffn,
           w_gate, w_up, w_down):
    tail = jnp.concatenate([x_sample.reshape(DEC_BATCH, D_MODEL),
                            jnp.zeros((META_ROW0 - DEC_BATCH, D_MODEL), F32),
                            meta_tokens.astype(F32)], axis=0)

    bucket, valid = _band_tables()
    fold, unfold = _fold_tables()
    bias3 = _band_bias(rel_bias, bucket, valid)
    brow = _t5_bucket(jnp.asarray(W_BUF - np.arange(W_BUF), jnp.int32)).reshape(1, W_BUF)

    kv_lanes = lambda c: jnp.transpose(c, (0, 1, 3, 4, 2)).reshape(DEPTH * DEC_BATCH * KV_DIM, W_BUF)
    cache_kt = kv_lanes(cache_k)
    cache_vt = kv_lanes(cache_v)
    state2d = state_conv.reshape(DEPTH * DEC_BATCH, 2 * CONV_DIM)

    x2d = x_prompt.reshape(M_X, D_MODEL)
    xn = _norm(x2d, tail, norm_pre_mix)
    h, h_tail = x2d, tail

    def kv_rows(t, n):
        return jnp.transpose(t.reshape(n, N_KV_HEADS, HEAD_DIM, WINDOW), (0, 3, 1, 2))

    k_all = jnp.zeros(cache_kt.shape, F32)
    v_all = jnp.zeros(cache_vt.shape, F32)
    kp, vp, cp, cs = [], [], [], []
    for l in range(DEPTH):
        z1, w_out_b = _matmul(xn, w_in, w_out, l, Z1_DIM)
        (u,) = _gated(xn, w_in, w_in, None, l, Z1_DIM, Z1_DIM + CONV_DIM, CONV_DIM,
                      False, F32, "conv_in")
        kvt = jnp.transpose(z1[M_X:M_X + DEC_BATCH, Q_DIM:Q_DIM + 2 * KV_DIM]
                            .reshape(DEC_BATCH // S_CHUNK, S_CHUNK, 2 * KV_DIM), (0, 2, 1))
        mix_s, k_all, v_all, c_new = _sample_mixer(
            l, z1, u, kvt, cache_kt, cache_vt, state2d, conv_w[l], attn_sinks.T, rel_bias.T,
            brow, fold, unfold, k_all, v_all)
        h, xn, kp_l, vp_l, cp_l = _mixer(l, attn_sinks[l], z1, u, bias3, conv_w[l], mix_s,
                                         w_out_b, h, h_tail, norm_post_mix, norm_pre_ffn)
        act, w_down_b = _gated(xn, w_gate, w_up, w_down, l, 0, 0, D_FF, True, BF16, "ffn_up")
        if l + 1 < DEPTH:
            h, xn = _proj_norm(act, w_down_b, l, h, norm_post_ffn, norm_pre_mix, l + 1, "ffn_down")
            h_tail = h
        else:
            y2d, y_tail = _proj_last(act, w_down_b, l, h, norm_post_ffn)

        kp.append(kv_rows(kp_l, BATCH))
        vp.append(kv_rows(vp_l, BATCH))
        cp.append(cp_l)
        cs.append(c_new.reshape(DEC_BATCH, CONV_WIDTH - 1, CONV_DIM))

    y_prompt = y2d.reshape(BATCH, SEQ, D_MODEL)
    y_sample = y_tail[0:DEC_BATCH].reshape(DEC_BATCH, 1, D_MODEL)
    k_sample = kv_rows(k_all, DEPTH * DEC_BATCH).reshape(DEPTH, DEC_BATCH, W_BUF, N_KV_HEADS, HEAD_DIM)
    v_sample = kv_rows(v_all, DEPTH * DEC_BATCH).reshape(DEPTH, DEC_BATCH, W_BUF, N_KV_HEADS, HEAD_DIM)
    return (y_prompt, y_sample, jnp.stack(kp), jnp.stack(vp), jnp.stack(cp),
            k_sample, v_sample, jnp.stack(cs))
```
